```python
import math
import jax, jax.numpy as jnp
from jax import lax
import numpy as np


D_MODEL = 1024
BATCH = 8
SEQ = 2048
DEPTH = 2
DEC_BATCH = 128
DEC_SEQ = 8
PAST_LEN = 16384
PAGE_SIZE = 128

N_MEM = 256
GM_WIDTH = D_MODEL
GM_CHUNK = 128
GM_GROUP = 128
GM_GROUPS = GM_WIDTH // GM_GROUP
SSD_EXPAND = 2
SSD_INNER = SSD_EXPAND * D_MODEL
SSD_HEADDIM = 64
SSD_HEADS = SSD_INNER // SSD_HEADDIM
SSD_STATE = 128
SSD_GROUPS = 4
SSD_HPG = SSD_HEADS // SSD_GROUPS
SSD_CONV = 4
SSD_CHUNK = 128
CONV_DIM = SSD_INNER + 2 * SSD_GROUPS * SSD_STATE
XA_HEADS = 4
XA_HEADDIM = D_MODEL // XA_HEADS
XA_WIDTH = XA_HEADS * XA_HEADDIM
N_BRANCH = 3
IN_DIM = 2 * GM_WIDTH + SSD_INNER + CONV_DIM + SSD_HEADS + XA_WIDTH + N_BRANCH * D_MODEL
D_FF = ((8 * D_MODEL // 3 + 127) // 128) * 128
N_EXPERTS = 8
TOP_K = 2
E_FF = 7 * D_MODEL // 2
N_DENSE = (DEPTH + 1) // 2
N_MOE = DEPTH // 2
ALPHA = (2 * DEPTH) ** 0.25
BETA = (8 * DEPTH) ** -0.25
LN_EPS = 1e-5

kernel_name = 'hybrid_gmlp_ssd_memxattn_deepnorm_step'


def layer_norm(x, g, b):
    xf = x.astype(jnp.float32)
    mu = jnp.mean(xf, -1, keepdims=True)
    var = jnp.mean(jnp.square(xf - mu), -1, keepdims=True)
    return ((xf - mu) * lax.rsqrt(var + LN_EPS)).astype(x.dtype) * g + b


def grouped_rms_norm(y, g):
    bsz, l, _ = y.shape
    yg = y.astype(jnp.float32).reshape(bsz, l, SSD_GROUPS, SSD_INNER // SSD_GROUPS)
    yg = yg * lax.rsqrt(jnp.mean(yg * yg, -1, keepdims=True) + LN_EPS)
    return yg.reshape(bsz, l, SSD_INNER).astype(y.dtype) * g


def gmlp_spatial(v, w_s, b_s):
    bsz, l, _ = v.shape
    n_chunks = -(-l // GM_CHUNK)
    pad = n_chunks * GM_CHUNK - l
    vc = jnp.pad(v, ((0, 0), (0, pad), (0, 0))).reshape(bsz, n_chunks, GM_CHUNK, GM_GROUPS, GM_GROUP)
    causal = jnp.tril(jnp.ones((GM_CHUNK, GM_CHUNK), dtype=bool))
    w = jnp.where(causal, w_s, 0.0)
    z = jnp.einsum('gij,bcjgd->bcigd', w, vc) + b_s.T[None, None, :, :, None]
    return z.reshape(bsz, n_chunks * GM_CHUNK, GM_WIDTH)[:, :l]


def causal_conv(xbc, prev, w, b):
    l = xbc.shape[1]
    full = jnp.concatenate([prev.astype(xbc.dtype), xbc], axis=1)
    out = b + full[:, 0:l] * w[0]
    for k in range(1, SSD_CONV):
        out = out + full[:, k:k + l] * w[k]
    return out, full[:, l:]


def ssd_chunked(x, dt, a, bm, cm, h0):
    f32 = jnp.float32
    bsz, l = x.shape[0], x.shape[1]
    q = min(SSD_CHUNK, l)
    nc = -(-l // q)
    pad = nc * q - l

    def padc(t):
        return jnp.pad(t.astype(f32), [(0, 0), (0, pad)] + [(0, 0)] * (t.ndim - 2))

    xp, dtp, bp, cp = padc(x), padc(dt), padc(bm), padc(cm)
    xdt = (xp * dtp[..., None]).reshape(bsz, nc, q, SSD_GROUPS, SSD_HPG, SSD_HEADDIM)
    bc = bp.reshape(bsz, nc, q, SSD_GROUPS, SSD_STATE)
    cc = cp.reshape(bsz, nc, q, SSD_GROUPS, SSD_STATE)
    adt = (dtp * a.astype(f32)).reshape(bsz, nc, q, SSD_GROUPS, SSD_HPG)
    a_cs = jnp.cumsum(jnp.moveaxis(adt, 2, -1), axis=-1)
    tril = jnp.tril(jnp.ones((q, q), dtype=bool))
    seg = a_cs[..., :, None] - a_cs[..., None, :]
    decay = jnp.exp(jnp.where(tril, seg, -jnp.inf))
    cb = jnp.einsum('bcign,bcjgn->bcgij', cc, bc)
    y_diag = jnp.einsum('bcgij,bcgrij,bcjgrp->bcigrp', cb, decay, xdt)
    to_end = jnp.exp(a_cs[..., -1:] - a_cs)
    chunk_states = jnp.einsum('bcjgn,bcgrj,bcjgrp->bcgrpn', bc, to_end, xdt)
    chunk_decay = jnp.exp(a_cs[..., -1])

    def step(h, inp):
        s_c, d_c = inp
        return d_c[..., None, None] * h + s_c, h

    h_init = h0.astype(f32).reshape(bsz, SSD_GROUPS, SSD_HPG, SSD_HEADDIM, SSD_STATE)
    h_last, h_in = lax.scan(step, h_init, (jnp.moveaxis(chunk_states, 1, 0), jnp.moveaxis(chunk_decay, 1, 0)))
    h_in = jnp.moveaxis(h_in, 0, 1)
    y_off = jnp.einsum('bcign,bcgrpn,bcgri->bcigrp', cc, h_in, jnp.exp(a_cs))
    y = (y_diag + y_off).reshape(bsz, nc * q, SSD_HEADS, SSD_HEADDIM)[:, :l]
    return y.astype(x.dtype), h_last.reshape(bsz, SSD_HEADS, SSD_HEADDIM, SSD_STATE).astype(h0.dtype)


def mixer(x, mem_k, mem_v, conv_prev, ssm_prev, w_in, conv_w, conv_b, dt_bias, a_log, d_skip,
          ssd_norm_g, v_ln_g, v_ln_b, w_s, b_s, p_gm, p_ssd, p_xa, w_out):
    bsz, l, _ = x.shape
    s1 = 2 * GM_WIDTH
    s2 = s1 + SSD_INNER
    s3 = s2 + CONV_DIM
    s4 = s3 + SSD_HEADS
    s5 = s4 + XA_WIDTH
    uv, z, xbc, dt, q, gates = jnp.split(x @ w_in, [s1, s2, s3, s4, s5], axis=-1)

    u, v = jnp.split(jax.nn.gelu(uv), 2, axis=-1)
    v = layer_norm(v, v_ln_g, v_ln_b)
    y_gm = u * gmlp_spatial(v, w_s, b_s)

    xbc_c, conv_new = causal_conv(xbc, conv_prev, conv_w, conv_b)
    xbc_c = jax.nn.silu(xbc_c)
    xs, bm, cm = jnp.split(xbc_c, [SSD_INNER, SSD_INNER + SSD_GROUPS * SSD_STATE], axis=-1)
    xs = xs.reshape(bsz, l, SSD_HEADS, SSD_HEADDIM)
    dt = jax.nn.softplus((dt + dt_bias).astype(jnp.float32))
    a = -jnp.exp(a_log.astype(jnp.float32))
    y, ssm_new = ssd_chunked(xs, dt, a, bm.reshape(bsz, l, SSD_GROUPS, SSD_STATE),
                             cm.reshape(bsz, l, SSD_GROUPS, SSD_STATE), ssm_prev)
    y = (y + xs * d_skip[:, None]).reshape(bsz, l, SSD_INNER) * jax.nn.silu(z)
    y_ssd = grouped_rms_norm(y, ssd_norm_g)

    qh = q.reshape(bsz, l, XA_HEADS, XA_HEADDIM)
    s = jnp.einsum('blhd,bmhd->bhlm', qh, mem_k).astype(jnp.float32) * (XA_HEADDIM ** -0.5)
    p = jax.nn.softmax(s, axis=-1).astype(x.dtype)
    y_xa = jnp.einsum('bhlm,bmhd->blhd', p, mem_v).reshape(bsz, l, XA_WIDTH)

    g = jax.nn.sigmoid(gates).reshape(bsz, l, N_BRANCH, D_MODEL)
    merged = g[:, :, 0] * (y_gm @ p_gm) + g[:, :, 1] * (y_ssd @ p_ssd) + g[:, :, 2] * (y_xa @ p_xa)
    return merged @ w_out, conv_new, ssm_new, v


def swiglu(x, wg, wu, wd):
    return (jax.nn.silu(x @ wg) * (x @ wu)) @ wd


def moe_swiglu(x, router_w, router_b, wg, wu, wd):
    logits = (x @ router_w).astype(jnp.float32) + router_b
    top_v, top_i = lax.top_k(logits, TOP_K)
    gate = jax.nn.softmax(top_v, axis=-1)
    out = jnp.zeros_like(x)
    for e in range(N_EXPERTS):
        w_e = jnp.sum(jnp.where(top_i == e, gate, 0.0), axis=-1).astype(x.dtype)
        out = out + w_e[..., None] * swiglu(x, wg[e], wu[e], wd[e])
    return out


def setup_inputs(seed: int = 0) -> dict:
    key = jax.random.key(seed)
    kit = iter(jax.random.split(key, 64))

    def nrm(shape, scale):
        return jax.random.normal(next(kit), shape, jnp.float32) * scale

    u_dt = jax.random.uniform(next(kit), (DEPTH, SSD_HEADS), jnp.float32)
    dt0 = jnp.exp(u_dt * (math.log(0.1) - math.log(1e-3)) + math.log(1e-3))
    dt_bias = dt0 + jnp.log(-jnp.expm1(-dt0))
    a_log = jnp.log(jax.random.uniform(next(kit), (DEPTH, SSD_HEADS), jnp.float32, minval=1.0, maxval=16.0))
    return {
        'x_prompt': nrm((BATCH, SEQ, D_MODEL), 1.0),
        'x_sample': nrm((DEC_BATCH, DEC_SEQ, D_MODEL), 1.0),
        'mem_prompt': nrm((BATCH, N_MEM, D_MODEL), 1.0),
        'cache_mem_k': nrm((DEPTH, DEC_BATCH, N_MEM, XA_HEADS, XA_HEADDIM), 1.0),
        'cache_mem_v': nrm((DEPTH, DEC_BATCH, N_MEM, XA_HEADS, XA_HEADDIM), 1.0),
        'state_conv': nrm((DEPTH, DEC_BATCH, SSD_CONV - 1, CONV_DIM), 1.0),
        'state_ssm': nrm((DEPTH, DEC_BATCH, SSD_HEADS, SSD_HEADDIM, SSD_STATE), 0.1),
        'w_in': nrm((DEPTH, D_MODEL, IN_DIM), D_MODEL ** -0.5),
        'conv_w': nrm((DEPTH, SSD_CONV, CONV_DIM), SSD_CONV ** -0.5),
        'conv_b': nrm((DEPTH, CONV_DIM), 0.01),
        'dt_bias': dt_bias,
        'a_log': a_log,
        'd_skip': 1.0 + nrm((DEPTH, SSD_HEADS), 0.01),
        'ssd_norm_g': 1.0 + nrm((DEPTH, SSD_INNER), 0.01),
        'v_ln_g': 1.0 + nrm((DEPTH, GM_WIDTH), 0.01),
        'v_ln_b': nrm((DEPTH, GM_WIDTH), 0.01),
        'w_s': nrm((DEPTH, GM_GROUPS, GM_CHUNK, GM_CHUNK), 0.5 * GM_CHUNK ** -0.5),
        'b_s': 1.0 + nrm((DEPTH, GM_GROUPS, GM_CHUNK), 0.01),
        'p_gm': nrm((DEPTH, GM_WIDTH, D_MODEL), GM_WIDTH ** -0.5),
        'p_ssd': nrm((DEPTH, SSD_INNER, D_MODEL), SSD_INNER ** -0.5),
        'p_xa': nrm((DEPTH, XA_WIDTH, D_MODEL), XA_WIDTH ** -0.5),
        'w_out': nrm((DEPTH, D_MODEL, D_MODEL), BETA * D_MODEL ** -0.5),
        'w_mem_k': nrm((DEPTH, D_MODEL, XA_WIDTH), D_MODEL ** -0.5),
        'w_mem_v': nrm((DEPTH, D_MODEL, XA_WIDTH), D_MODEL ** -0.5),
        'ln1_g': 1.0 + nrm((DEPTH, D_MODEL), 0.01),
        'ln1_b': nrm((DEPTH, D_MODEL), 0.01),
        'ln2_g': 1.0 + nrm((DEPTH, D_MODEL), 0.01),
        'ln2_b': nrm((DEPTH, D_MODEL), 0.01),
        'ffn_wg': nrm((N_DENSE, D_MODEL, D_FF), D_MODEL ** -0.5),
        'ffn_wu': nrm((N_DENSE, D_MODEL, D_FF), D_MODEL ** -0.5),
        'ffn_wd': nrm((N_DENSE, D_FF, D_MODEL), BETA * D_FF ** -0.5),
        'router_w': nrm((N_MOE, D_MODEL, N_EXPERTS), D_MODEL ** -0.5),
        'router_b': nrm((N_MOE, N_EXPERTS), 0.01),
        'moe_wg': nrm((N_MOE, N_EXPERTS, D_MODEL, E_FF), D_MODEL ** -0.5),
        'moe_wu': nrm((N_MOE, N_EXPERTS, D_MODEL, E_FF), D_MODEL ** -0.5),
        'moe_wd': nrm((N_MOE, N_EXPERTS, E_FF, D_MODEL), BETA * E_FF ** -0.5),
    }


def reference(x_prompt, x_sample, mem_prompt, cache_mem_k, cache_mem_v, state_conv, state_ssm,
              w_in, conv_w, conv_b, dt_bias, a_log, d_skip, ssd_norm_g, v_ln_g, v_ln_b, w_s, b_s,
              p_gm, p_ssd, p_xa, w_out, w_mem_k, w_mem_v, ln1_g, ln1_b, ln2_g, ln2_b,
              ffn_wg, ffn_wu, ffn_wd, router_w, router_b, moe_wg, moe_wu, moe_wd):

    def run_group(x, mem_ks, mem_vs, conv_prevs, ssm_prevs, keep_v):
        conv_out, ssm_out, v_out = [], [], []
        for i in range(DEPTH):
            h, c_new, s_new, v_rows = mixer(
                x, mem_ks[i], mem_vs[i], conv_prevs[i], ssm_prevs[i], w_in[i], conv_w[i], conv_b[i],
                dt_bias[i], a_log[i], d_skip[i], ssd_norm_g[i], v_ln_g[i], v_ln_b[i], w_s[i], b_s[i],
                p_gm[i], p_ssd[i], p_xa[i], w_out[i])
            x = layer_norm(ALPHA * x + h, ln1_g[i], ln1_b[i])
            j = i // 2
            if i % 2 == 0:
                f = swiglu(x, ffn_wg[j], ffn_wu[j], ffn_wd[j])
            else:
                f = moe_swiglu(x, router_w[j], router_b[j], moe_wg[j], moe_wu[j], moe_wd[j])
            x = layer_norm(ALPHA * x + f, ln2_g[i], ln2_b[i])
            conv_out.append(c_new)
            ssm_out.append(s_new)
            if keep_v:
                v_out.append(v_rows)
        v_stack = jnp.stack(v_out) if keep_v else None
        return x, jnp.stack(conv_out), jnp.stack(ssm_out), v_stack

    bp = x_prompt.shape[0]
    mem_k_prompt = jnp.einsum('bmd,ide->ibme', mem_prompt, w_mem_k).reshape(DEPTH, bp, N_MEM, XA_HEADS, XA_HEADDIM)
    mem_v_prompt = jnp.einsum('bmd,ide->ibme', mem_prompt, w_mem_v).reshape(DEPTH, bp, N_MEM, XA_HEADS, XA_HEADDIM)
    conv0 = jnp.zeros((DEPTH, bp, SSD_CONV - 1, CONV_DIM), x_prompt.dtype)
    ssm0 = jnp.zeros((DEPTH, bp, SSD_HEADS, SSD_HEADDIM, SSD_STATE), x_prompt.dtype)

    y_prompt, conv_prompt, ssm_prompt, _ = run_group(x_prompt, mem_k_prompt, mem_v_prompt, conv0, ssm0, False)
    y_sample, conv_sample, ssm_sample, gmlp_v_sample = run_group(
        x_sample, cache_mem_k, cache_mem_v, state_conv, state_ssm, True)
    return (y_prompt, y_sample, mem_k_prompt, mem_v_prompt, conv_prompt, ssm_prompt, conv_sample, ssm_sample, gmlp_v_sample)
```

```python
import functools

import jax
import jax.numpy as jnp
from jax import lax
from jax.experimental import pallas as pl
from jax.experimental.pallas import tpu as pltpu

F32 = jnp.float32
BF16 = jnp.bfloat16

D_MODEL = 1024
BATCH = 8
SEQ = 2048
DEPTH = 2
DEC_BATCH = 128
DEC_SEQ = 8
N_MEM = 256
GM_WIDTH = D_MODEL
GM_CHUNK = 128
GM_GROUP = 128
GM_GROUPS = GM_WIDTH // GM_GROUP
SSD_INNER = 2 * D_MODEL
SSD_HEADDIM = 64
SSD_HEADS = SSD_INNER // SSD_HEADDIM
SSD_STATE = 128
SSD_GROUPS = 4
SSD_HPG = SSD_HEADS // SSD_GROUPS
SSD_CONV = 4
SSD_CHUNK = 128
CONV_DIM = SSD_INNER + 2 * SSD_GROUPS * SSD_STATE
XA_HEADS = 4
XA_HEADDIM = D_MODEL // XA_HEADS
XA_WIDTH = XA_HEADS * XA_HEADDIM
N_BRANCH = 3
D_FF = ((8 * D_MODEL // 3 + 127) // 128) * 128
N_EXPERTS = 8
E_FF = 7 * D_MODEL // 2
ALPHA = (2 * DEPTH) ** 0.25
LN_EPS = 1e-5

N_PROMPT = BATCH * SEQ
N_SAMPLE = DEC_BATCH * DEC_SEQ
N_TOK = N_PROMPT + N_SAMPLE

OFF_Z = 2 * GM_WIDTH
OFF_XBC = OFF_Z + SSD_INNER
OFF_DT = OFF_XBC + CONV_DIM
OFF_Q = OFF_DT + SSD_HEADS
OFF_GATE = OFF_Q + XA_WIDTH

LANES = 128
ROWS = 128
GROUP_W = SSD_HPG * SSD_HEADDIM
VMEM_LIMIT = 56 * 1024 * 1024


def _params(sem):
    return pltpu.CompilerParams(dimension_semantics=sem, vmem_limit_bytes=VMEM_LIMIT)


def _dot(a, b):
    return jnp.dot(a, b, preferred_element_type=F32)


def _dot_nt(a, b):
    return lax.dot_general(a, b, (((1,), (1,)), ((), ())), preferred_element_type=F32)


def _split2(x):
    hi = x.astype(BF16)
    lo = (x - hi.astype(F32)).astype(BF16)
    return hi, lo


def _split3(x):
    hi = x.astype(BF16)
    r = x - hi.astype(F32)
    mid = r.astype(BF16)
    lo = (r - mid.astype(F32)).astype(BF16)
    return hi, mid, lo


def _dot_exact_lhs(m_bf16, x):
    hi, mid, lo = _split3(x)
    return _dot(m_bf16, hi) + _dot(m_bf16, mid) + _dot(m_bf16, lo)


def _dot_exact_rhs2(x, m_bf16):
    hi, lo = _split2(x)
    return _dot(hi, m_bf16) + _dot(lo, m_bf16)


def _layer_norm(r, g, b):
    mu = jnp.mean(r, axis=-1, keepdims=True)
    c = r - mu
    var = jnp.mean(c * c, axis=-1, keepdims=True)
    return c * lax.rsqrt(var + LN_EPS) * g + b


def _gelu_tanh(x):
    return x * (0.5 * (1.0 + jnp.tanh(0.7978845608028654 * (x + 0.044715 * (x * x * x)))))


def _sigmoid(x):
    return 1.0 / (1.0 + jnp.exp(-x))


def _silu(x):
    return x * _sigmoid(x)


def _softplus(x):
    return jnp.maximum(x, 0.0) + jnp.log(1.0 + jnp.exp(-jnp.abs(x)))


def _mm_kernel(x_ref, w_ref, *rest, epilogue, n_extra):
    extras = [r[...] for r in rest[:n_extra]]
    outs = rest[n_extra:]
    acc = _dot(x_ref[...], w_ref[...])
    res = epilogue(acc, *extras)
    for o, r in zip(outs, res):
        o[...] = r.astype(o.dtype)


def _mm(x, w, col_block0, n_cols, tn, extras, epilogue, out_dtypes, name, tm=1024):
    t, k = x.shape
    grid = (n_cols // tn, t // tm)
    in_specs = [pl.BlockSpec((tm, k), lambda j, i: (i, 0)),
                pl.BlockSpec((k, tn), lambda j, i: (0, j + col_block0))]
    in_specs += [pl.BlockSpec((1, tn), lambda j, i: (0, j)) for _ in extras]
    out_specs = [pl.BlockSpec((tm, tn), lambda j, i: (i, j)) for _ in out_dtypes]
    out_shape = [jax.ShapeDtypeStruct((t, n_cols), d) for d in out_dtypes]
    return pl.pallas_call(
        functools.partial(_mm_kernel, epilogue=epilogue, n_extra=len(extras)),
        grid=grid, in_specs=in_specs, out_specs=out_specs, out_shape=out_shape,
        compiler_params=_params(("parallel", "parallel")), name=name,
    )(x, w, *extras)


def _epi_gelu(acc):
    return (_gelu_tanh(acc),)


def _epi_gelu_ln(acc, g, b):
    return (_layer_norm(_gelu_tanh(acc), g, b),)


def _epi_silu(acc):
    return (_silu(acc),)


def _epi_id(acc):
    return (acc,)


def _epi_id2(acc):
    return (acc, acc)


def _epi_softplus(acc, bias):
    return (_softplus(acc + bias),)


def _epi_qscale(acc):
    return (acc * (XA_HEADDIM ** -0.5),)


def _epi_sigmoid(acc):
    return (_sigmoid(acc),)


GM_TILE = 512


def _gmlp_kernel(v_ref, u_ref, w_ref, b_ref, o_ref):
    for c in range(GM_TILE // ROWS):
        rs = slice(c * ROWS, (c + 1) * ROWS)
        for g in range(GM_GROUPS):
            cs = slice(g * GM_GROUP, (g + 1) * GM_GROUP)
            z = _dot(w_ref[0, g], v_ref[rs, cs].astype(BF16)) + b_ref[0, :, cs]
            o_ref[rs, cs] = (u_ref[rs, cs].astype(F32) * z).astype(o_ref.dtype)


def _gmlp(v, u, w2, b2):
    n_prompt_tiles = N_PROMPT // GM_TILE

    def sel(i):
        return jnp.where(i >= n_prompt_tiles, 1, 0)

    return pl.pallas_call(
        _gmlp_kernel,
        grid=(N_TOK // GM_TILE,),
        in_specs=[pl.BlockSpec((GM_TILE, GM_WIDTH), lambda i: (i, 0)),
                  pl.BlockSpec((GM_TILE, GM_WIDTH), lambda i: (i, 0)),
                  pl.BlockSpec((1, GM_GROUPS, ROWS, ROWS), lambda i: (sel(i), 0, 0, 0)),
                  pl.BlockSpec((1, ROWS, GM_WIDTH), lambda i: (sel(i), 0, 0))],
        out_specs=pl.BlockSpec((GM_TILE, GM_WIDTH), lambda i: (i, 0)),
        out_shape=jax.ShapeDtypeStruct((N_TOK, GM_WIDTH), BF16),
        compiler_params=_params(("parallel",)), name="gmlp_spatial",
    )(v, u, w2, b2)


def _ssd_block(xwin, pwin, dt, alog, conv_w, conv_b, lmat, bones, emat, dskip, row_in_seq):
    conv = conv_b + xwin(0) * conv_w[3:4, :]
    for s in range(1, SSD_CONV):
        xs_s = xwin(s)
        if pwin is not None:
            xs_s = jnp.where(row_in_seq >= s, xs_s, pwin(s))
        conv = conv + xs_s * conv_w[3 - s:4 - s, :]
    xc = _silu(conv)
    xs = xc[:, :SSD_INNER]
    bm = xc[:, SSD_INNER:SSD_INNER + SSD_GROUPS * SSD_STATE]
    cm = xc[:, SSD_INNER + SSD_GROUPS * SSD_STATE:]

    lane = lax.broadcasted_iota(jnp.int32, (1, LANES), 1)
    a_neg = jnp.where(lane < SSD_HEADS, -jnp.exp(alog), 0.0)
    adt = dt * a_neg
    a_cs = _dot_exact_lhs(lmat, adt)
    a_tot = _dot_exact_lhs(bones, adt)
    a_cs_t = a_cs.T
    mask = lmat.astype(F32) > 0.5

    dt_x = _dot_exact_rhs2(dt, emat)
    eacs_x = _dot_exact_rhs2(jnp.exp(a_cs), emat)
    te_x = _dot_exact_rhs2(jnp.exp(a_tot - a_cs), emat)
    xdt = xs * dt_x
    xw_t = (xdt * te_x).T.astype(BF16)

    lane_r = lax.broadcasted_iota(jnp.int32, (ROWS, LANES), 1)
    lo_half = lane_r < SSD_HEADDIM
    yd = []
    for g in range(SSD_GROUPS):
        ns = slice(g * SSD_STATE, (g + 1) * SSD_STATE)
        cb = _dot_nt(cm[:, ns].astype(BF16), bm[:, ns].astype(BF16))
        for hp in range(SSD_HPG // 2):
            h0 = g * SSD_HPG + 2 * hp
            ms = []
            for h in (h0, h0 + 1):
                seg = a_cs[:, h:h + 1] - a_cs_t[h:h + 1, :]
                ms.append(cb * jnp.exp(jnp.where(mask, seg, -1e30)))
            lhs = jnp.concatenate(ms, axis=1).astype(BF16)
            xp = xdt[:, h0 * SSD_HEADDIM:(h0 + 2) * SSD_HEADDIM]
            rhs = jnp.concatenate([jnp.where(lo_half, xp, 0.0), jnp.where(lo_half, 0.0, xp)],
                                  axis=0).astype(BF16)
            yd.append(_dot(lhs, rhs))
    y_pre = jnp.concatenate(yd, axis=1) + xs * dskip
    return dict(y_pre=y_pre, eacs_x=eacs_x, cm=cm, bm=bm, xw_t=xw_t, a_tot=a_tot)


def _ssd_finish(y, zs, norm_g):
    y = y * zs
    outs = []
    for g in range(SSD_GROUPS):
        yg = y[:, g * GROUP_W:(g + 1) * GROUP_W]
        ms = jnp.mean(yg * yg, axis=-1, keepdims=True)
        outs.append(yg * lax.rsqrt(ms + LN_EPS))
    return jnp.concatenate(outs, axis=1) * norm_g


def _ssd_prompt_kernel(xbc_ref, dt_ref, zs_ref, alog_ref, cw_ref, cb_ref, l_ref, ones_ref, e_ref,
                       et_ref, dskip_ref, ng_ref, y_ref, st_ref, xpad, state):
    c = pl.program_id(1)

    @pl.when(c == 0)
    def _():
        xpad[0:8, :] = jnp.zeros((8, CONV_DIM), F32)
        state[...] = jnp.zeros_like(state)

    xpad[8:8 + ROWS, :] = xbc_ref[...].astype(F32)

    def xwin(s):
        return xpad[8 - s:8 - s + ROWS, :]

    r = _ssd_block(xwin, None, dt_ref[...], alog_ref[...], cw_ref[...], cb_ref[...], l_ref[...],
                   ones_ref[...], e_ref[...], dskip_ref[...], None)
    xpad[0:8, :] = xpad[ROWS:ROWS + 8, :]

    st = state[...]
    st_b = st.astype(BF16)
    y_off, s_new = [], []
    for g in range(SSD_GROUPS):
        ns = slice(g * SSD_STATE, (g + 1) * SSD_STATE)
        gs = slice(g * GROUP_W, (g + 1) * GROUP_W)
        y_off.append(_dot_nt(r["cm"][:, ns].astype(BF16), st_b[gs, :]))
        s_new.append(_dot(r["xw_t"][gs, :], r["bm"][:, ns].astype(BF16)))
    y = r["y_pre"] + jnp.concatenate(y_off, axis=1) * r["eacs_x"]
    y_ref[...] = _ssd_finish(y, zs_ref[...].astype(F32), ng_ref[...]).astype(y_ref.dtype)

    decay = _dot_exact_lhs(et_ref[...], jnp.exp(r["a_tot"].T))
    new_state = decay * st + jnp.concatenate(s_new, axis=0)
    state[...] = new_state
    st_ref[0] = new_state


def _ssd_consts(kind):
    tril = jnp.tril(jnp.ones((ROWS, ROWS), F32))
    if kind == "prompt":
        lmat, bones = tril, jnp.ones((ROWS, ROWS), F32)
    else:
        eye = jnp.eye(ROWS // DEC_SEQ, dtype=F32)
        blk = jnp.kron(eye, jnp.ones((DEC_SEQ, DEC_SEQ), F32))
        lmat, bones = tril * blk, blk
    head = jnp.arange(SSD_INNER) // SSD_HEADDIM
    emat = (jnp.arange(LANES)[:, None] == head[None, :]).astype(BF16)
    return lmat.astype(BF16), bones.astype(BF16), emat, emat.T


def _const_spec(shape):
    nd = len(shape)
    return pl.BlockSpec(shape, lambda *_: (0,) * nd)


def _ssd_prompt(xbc, dt, zs, alog, conv_w, conv_b, dskip, norm_g):
    lmat, bones, emat, emat_t = _ssd_consts("prompt")
    nc = SEQ // ROWS
    row = lambda b, c: (b * nc + c, 0)
    consts = [alog, conv_w, conv_b, lmat, bones, emat, emat_t, dskip, norm_g]
    return pl.pallas_call(
        _ssd_prompt_kernel,
        grid=(BATCH, nc),
        in_specs=[pl.BlockSpec((ROWS, CONV_DIM), row), pl.BlockSpec((ROWS, LANES), row),
                  pl.BlockSpec((ROWS, SSD_INNER), row)] + [_const_spec(a.shape) for a in consts],
        out_specs=[pl.BlockSpec((ROWS, SSD_INNER), row),
                   pl.BlockSpec((1, SSD_INNER, SSD_STATE), lambda b, c: (b, 0, 0))],
        out_shape=[jax.ShapeDtypeStruct((N_PROMPT, SSD_INNER), BF16),
                   jax.ShapeDtypeStruct((BATCH, SSD_INNER, SSD_STATE), F32)],
        scratch_shapes=[pltpu.VMEM((ROWS + 8, CONV_DIM), F32), pltpu.VMEM((SSD_INNER, SSD_STATE), F32)],
        compiler_params=_params(("parallel", "arbitrary")), name="ssd_prompt",
    )(xbc, dt, zs, *consts)


SEQ_PER_STEP = 4
Q_ROWS = SEQ_PER_STEP * DEC_SEQ
N_QUARTER = ROWS // Q_ROWS


def _ssd_sample_kernel(xbc_ref, prev_ref, dt_ref, zs_ref, st_in, alog_ref, cw_ref, cb_ref, l_ref,
                       ones_ref, e_ref, et_ref, dskip_ref, ng_ref, y_ref, st_out,
                       xpad, ppad, ypre_s, eacs_s, cm_s):
    q = pl.program_id(1)
    xpad[0:8, :] = jnp.zeros((8, CONV_DIM), F32)
    xpad[8:8 + ROWS, :] = xbc_ref[...].astype(F32)
    ppad[0:ROWS, :] = prev_ref[...]
    ppad[ROWS:ROWS + 8, :] = jnp.zeros((8, CONV_DIM), F32)
    row_in_seq = lax.broadcasted_iota(jnp.int32, (ROWS, 1), 0) % DEC_SEQ

    def xwin(s):
        return xpad[8 - s:8 - s + ROWS, :]

    def pwin(s):
        return ppad[8 - s:8 - s + ROWS, :]

    r = _ssd_block(xwin, pwin, dt_ref[...], alog_ref[...], cw_ref[...], cb_ref[...], l_ref[...],
                   ones_ref[...], e_ref[...], dskip_ref[...], row_in_seq)
    ypre_s[...] = r["y_pre"]
    eacs_s[...] = r["eacs_x"]
    cm_s[...] = r["cm"]

    q0 = pl.multiple_of(q * Q_ROWS, Q_ROWS)
    cq = cm_s[pl.ds(q0, Q_ROWS), :].astype(BF16)
    row_q = lax.broadcasted_iota(jnp.int32, (Q_ROWS, 1), 0) // DEC_SEQ
    row_b = lax.broadcasted_iota(jnp.int32, (ROWS, 1), 0) // DEC_SEQ
    lane_b = lax.broadcasted_iota(jnp.int32, (1, LANES), 1) // DEC_SEQ
    e_atot_t = jnp.exp(r["a_tot"].T)
    et = et_ref[...]
    y_off = jnp.zeros((Q_ROWS, SSD_INNER), F32)
    for s in range(SEQ_PER_STEP):
        seq = q * SEQ_PER_STEP + s
        st = st_in[s]
        st_b = st.astype(BF16)
        bsel = row_b == seq
        yo, s_new = [], []
        for g in range(SSD_GROUPS):
            ns = slice(g * SSD_STATE, (g + 1) * SSD_STATE)
            gs = slice(g * GROUP_W, (g + 1) * GROUP_W)
            yo.append(_dot_nt(cq[:, ns], st_b[gs, :]))
            bm_s = jnp.where(bsel, r["bm"][:, ns], 0.0).astype(BF16)
            s_new.append(_dot(r["xw_t"][gs, :], bm_s))
        y_off = y_off + jnp.where(row_q == s, jnp.concatenate(yo, axis=1), 0.0)
        dec_col = jnp.sum(jnp.where(lane_b == seq, e_atot_t, 0.0), axis=1, keepdims=True) * (1.0 / DEC_SEQ)
        decay = _dot_exact_lhs(et, jnp.broadcast_to(dec_col, (LANES, SSD_STATE)))
        st_out[s] = decay * st + jnp.concatenate(s_new, axis=0)

    y = ypre_s[pl.ds(q0, Q_ROWS), :] + y_off * eacs_s[pl.ds(q0, Q_ROWS), :]
    y_ref[...] = _ssd_finish(y, zs_ref[...].astype(F32), ng_ref[...]).astype(y_ref.dtype)


def _ssd_sample(xbc, prev_rows, dt, zs, st_in, alog, conv_w, conv_b, dskip, norm_g):
    lmat, bones, emat, emat_t = _ssd_consts("sample")
    blk0 = N_PROMPT // ROWS
    qblk0 = N_PROMPT // Q_ROWS
    consts = [alog, conv_w, conv_b, lmat, bones, emat, emat_t, dskip, norm_g]
    return pl.pallas_call(
        _ssd_sample_kernel,
        grid=(N_SAMPLE // ROWS, N_QUARTER),
        in_specs=[pl.BlockSpec((ROWS, CONV_DIM), lambda b, q: (blk0 + b, 0)),
                  pl.BlockSpec((ROWS, CONV_DIM), lambda b, q: (b, 0)),
                  pl.BlockSpec((ROWS, LANES), lambda b, q: (blk0 + b, 0)),
                  pl.BlockSpec((Q_ROWS, SSD_INNER), lambda b, q: (qblk0 + b * N_QUARTER + q, 0)),
                  pl.BlockSpec((SEQ_PER_STEP, SSD_INNER, SSD_STATE), lambda b, q: (b * N_QUARTER + q, 0, 0)),
                  ] + [_const_spec(a.shape) for a in consts],
        out_specs=[pl.BlockSpec((Q_ROWS, SSD_INNER), lambda b, q: (b * N_QUARTER + q, 0)),
                   pl.BlockSpec((SEQ_PER_STEP, SSD_INNER, SSD_STATE), lambda b, q: (b * N_QUARTER + q, 0, 0))],
        out_shape=[jax.ShapeDtypeStruct((N_SAMPLE, SSD_INNER), BF16),
                   jax.ShapeDtypeStruct((DEC_BATCH, SSD_INNER, SSD_STATE), F32)],
        scratch_shapes=[pltpu.VMEM((ROWS + 8, CONV_DIM), F32), pltpu.VMEM((ROWS + 8, CONV_DIM), F32),
                        pltpu.VMEM((ROWS, SSD_INNER), F32), pltpu.VMEM((ROWS, SSD_INNER), F32),
                        pltpu.VMEM((ROWS, SSD_GROUPS * SSD_STATE), F32)],
        compiler_params=_params(("parallel", "arbitrary")), name="ssd_sample",
    )(xbc, prev_rows, dt, zs, st_in, *consts)


def _xattn_kernel(q_ref, k_ref, v_ref, o_ref, *, n_seq, rows_per_seq):
    q = q_ref[...]
    rows = q.shape[0]
    row_seq = lax.broadcasted_iota(jnp.int32, (rows, 1), 0) // rows_per_seq
    for h in range(XA_HEADS):
        hs = slice(h * XA_HEADDIM, (h + 1) * XA_HEADDIM)
        out = None
        for s in range(n_seq):
            k = k_ref[s, :, hs].astype(BF16)
            v = v_ref[s, :, hs].astype(BF16)
            sc = _dot_nt(q[:, hs], k)
            sc = sc - jnp.max(sc, axis=-1, keepdims=True)
            p = jnp.exp(sc)
            p = (p / jnp.sum(p, axis=-1, keepdims=True)).astype(BF16)
            y = _dot(p, v)
            out = y if n_seq == 1 else (jnp.where(row_seq == s, y, 0.0) + (0.0 if out is None else out))
        o_ref[:, hs] = out.astype(o_ref.dtype)


XA_Q_TILE = 512
XA_SAMPLE_SEQS = 4


def _xattn_prompt(q, mem_k, mem_v):
    nq = SEQ // XA_Q_TILE
    return pl.pallas_call(
        functools.partial(_xattn_kernel, n_seq=1, rows_per_seq=XA_Q_TILE),
        grid=(BATCH, nq),
        in_specs=[pl.BlockSpec((XA_Q_TILE, XA_WIDTH), lambda b, j: (b * nq + j, 0)),
                  pl.BlockSpec((1, N_MEM, XA_WIDTH), lambda b, j: (b, 0, 0)),
                  pl.BlockSpec((1, N_MEM, XA_WIDTH), lambda b, j: (b, 0, 0))],
        out_specs=pl.BlockSpec((XA_Q_TILE, XA_WIDTH), lambda b, j: (b * nq + j, 0)),
        out_shape=jax.ShapeDtypeStruct((N_PROMPT, XA_WIDTH), BF16),
        compiler_params=_params(("parallel", "parallel")), name="xattn_prompt",
    )(q, mem_k, mem_v)


def _xattn_sample(q, mem_k, mem_v):
    rows = XA_SAMPLE_SEQS * DEC_SEQ
    blk0 = N_PROMPT // rows
    return pl.pallas_call(
        functools.partial(_xattn_kernel, n_seq=XA_SAMPLE_SEQS, rows_per_seq=DEC_SEQ),
        grid=(DEC_BATCH // XA_SAMPLE_SEQS,),
        in_specs=[pl.BlockSpec((rows, XA_WIDTH), lambda j: (blk0 + j, 0)),
                  pl.BlockSpec((XA_SAMPLE_SEQS, N_MEM, XA_WIDTH), lambda j: (j, 0, 0)),
                  pl.BlockSpec((XA_SAMPLE_SEQS, N_MEM, XA_WIDTH), lambda j: (j, 0, 0))],
        out_specs=pl.BlockSpec((rows, XA_WIDTH), lambda j: (j, 0)),
        out_shape=jax.ShapeDtypeStruct((N_SAMPLE, XA_WIDTH), BF16),
        compiler_params=_params(("parallel",)), name="xattn_sample",
    )(q, mem_k, mem_v)


MERGE_TILE = 256


def _merge_kernel(ygm, yssd, yxa, gates, x, pgm, pssd, pxa, wout, lng, lnb, o_f, o_b):
    g = gates[...].astype(F32)
    m = _dot(ygm[...], pgm[...]) * g[:, :D_MODEL]
    m = m + _dot(yssd[...], pssd[...]) * g[:, D_MODEL:2 * D_MODEL]
    m = m + _dot(yxa[...], pxa[...]) * g[:, 2 * D_MODEL:]
    h = _dot(m.astype(BF16), wout[...])
    y = _layer_norm(ALPHA * x[...] + h, lng[...], lnb[...])
    o_f[...] = y
    o_b[...] = y.astype(BF16)


def _merge(ygm, yssd, yxa, gates, x, pgm, pssd, pxa, wout, lng, lnb):
    tm = MERGE_TILE
    row = lambda i: (i, 0)
    acts = [ygm, yssd, yxa, gates, x]
    consts = [pgm, pssd, pxa, wout, lng, lnb]
    return pl.pallas_call(
        _merge_kernel,
        grid=(N_TOK // tm,),
        in_specs=[pl.BlockSpec((tm, a.shape[1]), row) for a in acts] + [_const_spec(a.shape) for a in consts],
        out_specs=[pl.BlockSpec((tm, D_MODEL), row), pl.BlockSpec((tm, D_MODEL), row)],
        out_shape=[jax.ShapeDtypeStruct((N_TOK, D_MODEL), F32), jax.ShapeDtypeStruct((N_TOK, D_MODEL), BF16)],
        compiler_params=_params(("parallel",)), name="merge_out_ln",
    )(*acts, *consts)


FFN_TILE = 512
FFN_SPLIT = 2


def _ffn_kernel(xb, xf, wg, wu, wd, lng, lnb, o_f, o_b, acc):
    k = pl.program_id(1)

    @pl.when(k == 0)
    def _():
        acc[...] = jnp.zeros_like(acc)

    h = (_silu(_dot(xb[...], wg[...])) * _dot(xb[...], wu[...])).astype(BF16)
    acc[...] += _dot(h, wd[...])

    @pl.when(k == pl.num_programs(1) - 1)
    def _():
        y = _layer_norm(ALPHA * xf[...] + acc[...], lng[...], lnb[...])
        o_f[...] = y
        o_b[...] = y.astype(BF16)


def _ffn(xb, xf, wg, wu, wd, lng, lnb):
    tm, tf = FFN_TILE, D_FF // FFN_SPLIT
    row = lambda i, k: (i, 0)
    return pl.pallas_call(
        _ffn_kernel,
        grid=(N_TOK // tm, FFN_SPLIT),
        in_specs=[pl.BlockSpec((tm, D_MODEL), row), pl.BlockSpec((tm, D_MODEL), row),
                  pl.BlockSpec((D_MODEL, tf), lambda i, k: (0, k)),
                  pl.BlockSpec((D_MODEL, tf), lambda i, k: (0, k)),
                  pl.BlockSpec((tf, D_MODEL), lambda i, k: (k, 0)),
                  _const_spec(lng.shape), _const_spec(lnb.shape)],
        out_specs=[pl.BlockSpec((tm, D_MODEL), row), pl.BlockSpec((tm, D_MODEL), row)],
        out_shape=[jax.ShapeDtypeStruct((N_TOK, D_MODEL), F32), jax.ShapeDtypeStruct((N_TOK, D_MODEL), BF16)],
        scratch_shapes=[pltpu.VMEM((tm, D_MODEL), F32)],
        compiler_params=_params(("parallel", "arbitrary")), name="ffn_swiglu_ln",
    )(xb, xf, wg, wu, wd, lng, lnb)


ROUTER_TILE = 1024


def _router_kernel(x_ref, w_ref, b_ref, o_ref):
    xs = _split3(x_ref[...])
    ws = _split3(w_ref[...])
    logits = b_ref[...] + _dot(xs[0], ws[0])
    for a, b in ((0, 1), (1, 0), (0, 2), (2, 0), (1, 1)):
        logits = logits + _dot(xs[a], ws[b])
    lane = lax.broadcasted_iota(jnp.int32, logits.shape, 1)
    logits = jnp.where(lane < N_EXPERTS, logits, -1e30)
    m1 = jnp.max(logits, axis=-1, keepdims=True)
    i1 = jnp.min(jnp.where(logits == m1, lane, LANES), axis=-1, keepdims=True)
    rest = jnp.where(lane == i1, -1e30, logits)
    m2 = jnp.max(rest, axis=-1, keepdims=True)
    i2 = jnp.min(jnp.where(rest == m2, lane, LANES), axis=-1, keepdims=True)
    e2 = jnp.exp(m2 - m1)
    den = 1.0 + e2
    o_ref[...] = jnp.where(lane == i1, 1.0 / den, 0.0) + jnp.where(lane == i2, e2 / den, 0.0)


def _router(x, w_pad, b_pad):
    tm = ROUTER_TILE
    return pl.pallas_call(
        _router_kernel,
        grid=(N_TOK // tm,),
        in_specs=[pl.BlockSpec((tm, D_MODEL), lambda i: (i, 0)), _const_spec(w_pad.shape),
                  _const_spec(b_pad.shape)],
        out_specs=pl.BlockSpec((tm, LANES), lambda i: (i, 0)),
        out_shape=jax.ShapeDtypeStruct((N_TOK, LANES), F32),
        compiler_params=_params(("parallel",)), name="moe_router",
    )(x, w_pad, b_pad)


MOE_TILE = 512
MOE_SPLIT = 2


def _moe_kernel(xb, xf, comb, wg, wu, wd, lng, lnb, o_f, o_b, acc):
    e = pl.program_id(1)
    k = pl.program_id(2)

    @pl.when((e == 0) & (k == 0))
    def _():
        acc[...] = jnp.zeros_like(acc)

    lane = lax.broadcasted_iota(jnp.int32, (1, LANES), 1)
    w_e = jnp.sum(jnp.where(lane == e, comb[...], 0.0), axis=-1, keepdims=True)
    h = (_silu(_dot(xb[...], wg[0])) * _dot(xb[...], wu[0])).astype(BF16)
    acc[...] += w_e * _dot(h, wd[0])

    @pl.when((e == pl.num_programs(1) - 1) & (k == pl.num_programs(2) - 1))
    def _():
        y = _layer_norm(ALPHA * xf[...] + acc[...], lng[...], lnb[...])
        o_f[...] = y
        o_b[...] = y.astype(BF16)


def _moe(xb, xf, comb, wg, wu, wd, lng, lnb):
    tm, tf = MOE_TILE, E_FF // MOE_SPLIT
    row = lambda i, e, k: (i, 0)
    return pl.pallas_call(
        _moe_kernel,
        grid=(N_TOK // tm, N_EXPERTS, MOE_SPLIT),
        in_specs=[pl.BlockSpec((tm, D_MODEL), row), pl.BlockSpec((tm, D_MODEL), row),
                  pl.BlockSpec((tm, LANES), row),
                  pl.BlockSpec((1, D_MODEL, tf), lambda i, e, k: (e, 0, k)),
                  pl.BlockSpec((1, D_MODEL, tf), lambda i, e, k: (e, 0, k)),
                  pl.BlockSpec((1, tf, D_MODEL), lambda i, e, k: (e, k, 0)),
                  _const_spec(lng.shape), _const_spec(lnb.shape)],
        out_specs=[pl.BlockSpec((tm, D_MODEL), row), pl.BlockSpec((tm, D_MODEL), row)],
        out_shape=[jax.ShapeDtypeStruct((N_TOK, D_MODEL), F32), jax.ShapeDtypeStruct((N_TOK, D_MODEL), BF16)],
        scratch_shapes=[pltpu.VMEM((tm, D_MODEL), F32)],
        compiler_params=_params(("parallel", "arbitrary", "arbitrary")), name="moe_swiglu_ln",
    )(xb, xf, comb, wg, wu, wd, lng, lnb)


def _row(a):
    return a.reshape(1, -1).astype(F32)


def _pad_lanes(a):
    return jnp.pad(a, ((0, 0), (0, LANES - a.shape[1])))


def _gmlp_weights(w_s, b_s):
    tril = jnp.tril(jnp.ones((GM_CHUNK, GM_CHUNK), dtype=bool))
    w_p = jnp.where(tril, w_s, 0.0)
    n_seq = ROWS // DEC_SEQ
    w_8 = w_p[:, :DEC_SEQ, :DEC_SEQ]
    w_d = jnp.einsum("ab,gij->gaibj", jnp.eye(n_seq, dtype=F32), w_8).reshape(GM_GROUPS, ROWS, ROWS)
    bias_p = jnp.repeat(b_s.T, GM_GROUP, axis=1)
    bias_d = jnp.tile(bias_p[:DEC_SEQ], (n_seq, 1))
    return jnp.stack([w_p, w_d]).astype(BF16), jnp.stack([bias_p, bias_d])


def kernel(x_prompt, x_sample, mem_prompt, cache_mem_k, cache_mem_v, state_conv, state_ssm, w_in, conv_w, conv_b, dt_bias, a_log, d_skip, ssd_norm_g, v_ln_g, v_ln_b, w_s, b_s, p_gm, p_ssd, p_xa, w_out, w_mem_k, w_mem_v, ln1_g, ln1_b, ln2_g, ln2_b, ffn_wg, ffn_wu, ffn_wd, router_w, router_b, moe_wg, moe_wu, moe_wd):
    x = jnp.concatenate([x_prompt.reshape(N_PROMPT, D_MODEL), x_sample.reshape(N_SAMPLE, D_MODEL)], axis=0)
    xb = x.astype(BF16)
    mem_b = mem_prompt.reshape(BATCH * N_MEM, D_MODEL).astype(BF16)

    mem_k_out, mem_v_out, ssm_p_out, ssm_s_out, conv_p_out, conv_s_out, v_out = [], [], [], [], [], [], []
    for i in range(DEPTH):
        mk_f, mk_b = _mm(mem_b, w_mem_k[i].astype(BF16), 0, XA_WIDTH, 1024, [], _epi_id2, [F32, BF16], "mem_k")
        mv_f, mv_b = _mm(mem_b, w_mem_v[i].astype(BF16), 0, XA_WIDTH, 1024, [], _epi_id2, [F32, BF16], "mem_v")
        mem_k_out.append(mk_f.reshape(BATCH, N_MEM, XA_HEADS, XA_HEADDIM))
        mem_v_out.append(mv_f.reshape(BATCH, N_MEM, XA_HEADS, XA_HEADDIM))

        w_main = w_in[i].astype(BF16)
        w_tail = w_in[i][:, OFF_Q:].astype(BF16)
        (u,) = _mm(xb, w_main, 0, GM_WIDTH, 1024, [], _epi_gelu, [BF16], "in_u")
        (v,) = _mm(xb, w_main, 1, GM_WIDTH, 1024, [_row(v_ln_g[i]), _row(v_ln_b[i])], _epi_gelu_ln, [F32], "in_v")
        (zs,) = _mm(xb, w_main, OFF_Z // 1024, SSD_INNER, 1024, [], _epi_silu, [BF16], "in_z")
        (xbc,) = _mm(xb, w_main, OFF_XBC // 1024, CONV_DIM, 1024, [], _epi_id, [BF16], "in_xbc")
        (dt,) = _mm(xb, w_main, OFF_DT // LANES, LANES, LANES, [_pad_lanes(_row(dt_bias[i]))], _epi_softplus,
                    [F32], "in_dt")
        (q,) = _mm(xb, w_tail, 0, XA_WIDTH, 1024, [], _epi_qscale, [BF16], "in_q")
        (gates,) = _mm(xb, w_tail, 1, N_BRANCH * D_MODEL, 1024, [], _epi_sigmoid, [BF16], "in_gates")

        gm_w, gm_b = _gmlp_weights(w_s[i], b_s[i])
        y_gm = _gmlp(v, u, gm_w, gm_b)

        alog = _pad_lanes(_row(a_log[i]))
        dskip = _row(jnp.repeat(d_skip[i], SSD_HEADDIM))
        ssd_args = (alog, conv_w[i], _row(conv_b[i]), dskip, _row(ssd_norm_g[i]))
        y_ssd_p, ssm_p = _ssd_prompt(xbc, dt, zs, *ssd_args)
        prev_rows = jnp.pad(state_conv[i], ((0, 0), (DEC_SEQ - (SSD_CONV - 1), 0), (0, 0))).reshape(N_SAMPLE, CONV_DIM)
        y_ssd_s, ssm_s = _ssd_sample(xbc, prev_rows, dt, zs, state_ssm[i].reshape(DEC_BATCH, SSD_INNER, SSD_STATE),
                                     *ssd_args)
        y_ssd = jnp.concatenate([y_ssd_p, y_ssd_s], axis=0)
        ssm_p_out.append(ssm_p.reshape(BATCH, SSD_HEADS, SSD_HEADDIM, SSD_STATE))
        ssm_s_out.append(ssm_s.reshape(DEC_BATCH, SSD_HEADS, SSD_HEADDIM, SSD_STATE))
        xbc_f = xbc.astype(F32)
        conv_p_out.append(xbc_f[:N_PROMPT].reshape(BATCH, SEQ, CONV_DIM)[:, SEQ - (SSD_CONV - 1):])
        conv_s_out.append(xbc_f[N_PROMPT:].reshape(DEC_BATCH, DEC_SEQ, CONV_DIM)[:, DEC_SEQ - (SSD_CONV - 1):])
        v_out.append(v[N_PROMPT:].reshape(DEC_BATCH, DEC_SEQ, GM_WIDTH))

        y_xa_p = _xattn_prompt(q, mk_b.reshape(BATCH, N_MEM, XA_WIDTH), mv_b.reshape(BATCH, N_MEM, XA_WIDTH))
        y_xa_s = _xattn_sample(q, cache_mem_k[i].reshape(DEC_BATCH, N_MEM, XA_WIDTH),
                               cache_mem_v[i].reshape(DEC_BATCH, N_MEM, XA_WIDTH))
        y_xa = jnp.concatenate([y_xa_p, y_xa_s], axis=0)

        x, xb = _merge(y_gm, y_ssd, y_xa, gates, x, p_gm[i].astype(BF16), p_ssd[i].astype(BF16),
                       p_xa[i].astype(BF16), w_out[i].astype(BF16), _row(ln1_g[i]), _row(ln1_b[i]))

        j = i // 2
        if i % 2 == 0:
            x, xb = _ffn(xb, x, ffn_wg[j].astype(BF16), ffn_wu[j].astype(BF16), ffn_wd[j].astype(BF16),
                         _row(ln2_g[i]), _row(ln2_b[i]))
        else:
            comb = _router(x, _pad_lanes(router_w[j]), _pad_lanes(_row(router_b[j])))
            x, xb = _moe(xb, x, comb, moe_wg[j].astype(BF16), moe_wu[j].astype(BF16), moe_wd[j].astype(BF16),
                         _row(ln2_g[i]), _row(ln2_b[i]))

    y_prompt = x[:N_PROMPT].reshape(BATCH, SEQ, D_MODEL)
    y_sample = x[N_PROMPT:].reshape(DEC_BATCH, DEC_SEQ, D_MODEL)
    return (y_prompt, y_sample, jnp.stack(mem_k_out), jnp.stack(mem_v_out), jnp.stack(conv_p_out),
            jnp.stack(ssm_p_out), jnp.stack(conv_s_out), jnp.stack(ssm_s_out), jnp.stack(v_out))
```

```python
import functools

import jax
import jax.numpy as jnp
from jax import lax
from jax.experimental import pallas as pl
from jax.experimental.pallas import tpu as pltpu

F32 = jnp.float32
BF16 = jnp.bfloat16

D_MODEL = 1024
BATCH = 8
SEQ = 2048
DEPTH = 2
DEC_BATCH = 128
DEC_SEQ = 8
N_MEM = 256
GM_WIDTH = D_MODEL
GM_CHUNK = 128
GM_GROUP = 128
GM_GROUPS = GM_WIDTH // GM_GROUP
SSD_INNER = 2 * D_MODEL
SSD_HEADDIM = 64
SSD_HEADS = SSD_INNER // SSD_HEADDIM
SSD_STATE = 128
SSD_GROUPS = 4
SSD_HPG = SSD_HEADS // SSD_GROUPS
SSD_CONV = 4
SSD_CHUNK = 128
CONV_DIM = SSD_INNER + 2 * SSD_GROUPS * SSD_STATE
XA_HEADS = 4
XA_HEADDIM = D_MODEL // XA_HEADS
XA_WIDTH = XA_HEADS * XA_HEADDIM
N_BRANCH = 3
D_FF = ((8 * D_MODEL // 3 + 127) // 128) * 128
N_EXPERTS = 8
E_FF = 7 * D_MODEL // 2
ALPHA = (2 * DEPTH) ** 0.25
LN_EPS = 1e-5

N_PROMPT = BATCH * SEQ
N_SAMPLE = DEC_BATCH * DEC_SEQ
N_TOK = N_PROMPT + N_SAMPLE

OFF_Z = 2 * GM_WIDTH
OFF_XBC = OFF_Z + SSD_INNER
OFF_DT = OFF_XBC + CONV_DIM
OFF_Q = OFF_DT + SSD_HEADS
OFF_GATE = OFF_Q + XA_WIDTH

LANES = 128
ROWS = 128
GROUP_W = SSD_HPG * SSD_HEADDIM
VMEM_LIMIT = 56 * 1024 * 1024


def _params(sem):
    return pltpu.CompilerParams(dimension_semantics=sem, vmem_limit_bytes=VMEM_LIMIT)


def _dot(a, b):
    return jnp.dot(a, b, preferred_element_type=F32)


def _dot_nt(a, b):
    return lax.dot_general(a, b, (((1,), (1,)), ((), ())), preferred_element_type=F32)


def _split2(x):
    hi = x.astype(BF16)
    lo = (x - hi.astype(F32)).astype(BF16)
    return hi, lo


def _split3(x):
    hi = x.astype(BF16)
    r = x - hi.astype(F32)
    mid = r.astype(BF16)
    lo = (r - mid.astype(F32)).astype(BF16)
    return hi, mid, lo


def _dot_exact_lhs(m_bf16, x):
    hi, mid, lo = _split3(x)
    return _dot(m_bf16, hi) + _dot(m_bf16, mid) + _dot(m_bf16, lo)


def _dot_exact_rhs2(x, m_bf16):
    hi, lo = _split2(x)
    return _dot(hi, m_bf16) + _dot(lo, m_bf16)


def _layer_norm(r, g, b):
    mu = jnp.mean(r, axis=-1, keepdims=True)
    c = r - mu
    var = jnp.mean(c * c, axis=-1, keepdims=True)
    return c * lax.rsqrt(var + LN_EPS) * g + b


def _gelu_tanh(x):
    return x * (0.5 * (1.0 + jnp.tanh(0.7978845608028654 * (x + 0.044715 * (x * x * x)))))


def _sigmoid(x):
    return 1.0 / (1.0 + jnp.exp(-x))


def _silu(x):
    return x * _sigmoid(x)


def _softplus(x):
    return jnp.maximum(x, 0.0) + jnp.log(1.0 + jnp.exp(-jnp.abs(x)))


def _mm_kernel(x_ref, w_ref, *rest, epilogue, n_extra, n_out):
    extras = [r[...] for r in rest[:n_extra]]
    outs = rest[n_extra:n_extra + n_out]
    w_bf16 = rest[n_extra + n_out]

    @pl.when(pl.program_id(1) == 0)
    def _():
        w_bf16[...] = w_ref[0].astype(BF16)

    acc = _dot(x_ref[...], w_bf16[...])
    res = epilogue(acc, *extras)
    for o, r in zip(outs, res):
        o[...] = r.astype(o.dtype)


def _mm(x, w, layer, col_block0, n_cols, tn, extras, epilogue, out_dtypes, name, tm=1024):
    t, k = x.shape
    grid = (n_cols // tn, t // tm)
    in_specs = [pl.BlockSpec((tm, k), lambda j, i: (i, 0)),
                pl.BlockSpec((1, k, tn), lambda j, i: (layer, 0, j + col_block0))]
    in_specs += [pl.BlockSpec((1, tn), lambda j, i: (0, j)) for _ in extras]
    out_specs = [pl.BlockSpec((tm, tn), lambda j, i: (i, j)) for _ in out_dtypes]
    out_shape = [jax.ShapeDtypeStruct((t, n_cols), d) for d in out_dtypes]
    return pl.pallas_call(
        functools.partial(_mm_kernel, epilogue=epilogue, n_extra=len(extras), n_out=len(out_dtypes)),
        grid=grid, in_specs=in_specs, out_specs=out_specs, out_shape=out_shape,
        scratch_shapes=[pltpu.VMEM((k, tn), BF16)],
        compiler_params=_params(("parallel", "arbitrary")), name=name,
    )(x, w, *extras)


def _epi_gelu(acc):
    return (_gelu_tanh(acc),)


def _epi_gelu_ln(acc, g, b):
    return (_layer_norm(_gelu_tanh(acc), g, b),)


def _epi_silu(acc):
    return (_silu(acc),)


def _epi_id(acc):
    return (acc,)


def _epi_id2(acc):
    return (acc, acc)


def _epi_softplus(acc, bias):
    return (_softplus(acc + bias),)


def _epi_qscale(acc):
    return (acc * (XA_HEADDIM ** -0.5),)


def _epi_sigmoid(acc):
    return (_sigmoid(acc),)


GM_TILE = 512


def _gmlp_kernel(v_ref, u_ref, w_ref, b_ref, o_ref):
    for c in range(GM_TILE // ROWS):
        rs = slice(c * ROWS, (c + 1) * ROWS)
        for g in range(GM_GROUPS):
            cs = slice(g * GM_GROUP, (g + 1) * GM_GROUP)
            z = _dot(w_ref[0, g], v_ref[rs, cs].astype(BF16)) + b_ref[0, :, cs]
            o_ref[rs, cs] = (u_ref[rs, cs].astype(F32) * z).astype(o_ref.dtype)


def _gmlp(v, u, w2, b2):
    n_prompt_tiles = N_PROMPT // GM_TILE

    def sel(i):
        return jnp.where(i >= n_prompt_tiles, 1, 0)

    return pl.pallas_call(
        _gmlp_kernel,
        grid=(N_TOK // GM_TILE,),
        in_specs=[pl.BlockSpec((GM_TILE, GM_WIDTH), lambda i: (i, 0)),
                  pl.BlockSpec((GM_TILE, GM_WIDTH), lambda i: (i, 0)),
                  pl.BlockSpec((1, GM_GROUPS, ROWS, ROWS), lambda i: (sel(i), 0, 0, 0)),
                  pl.BlockSpec((1, ROWS, GM_WIDTH), lambda i: (sel(i), 0, 0))],
        out_specs=pl.BlockSpec((GM_TILE, GM_WIDTH), lambda i: (i, 0)),
        out_shape=jax.ShapeDtypeStruct((N_TOK, GM_WIDTH), BF16),
        compiler_params=_params(("parallel",)), name="gmlp_spatial",
    )(v, u, w2, b2)


def _ssd_block(xwin, pwin, dt, alog, conv_w, conv_b, lmat, bones, emat, dskip, row_in_seq):
    conv = conv_b + xwin(0) * conv_w[3:4, :]
    for s in range(1, SSD_CONV):
        xs_s = xwin(s)
        if pwin is not None:
            xs_s = jnp.where(row_in_seq >= s, xs_s, pwin(s))
        conv = conv + xs_s * conv_w[3 - s:4 - s, :]
    xc = _silu(conv)
    xs = xc[:, :SSD_INNER]
    bm = xc[:, SSD_INNER:SSD_INNER + SSD_GROUPS * SSD_STATE]
    cm = xc[:, SSD_INNER + SSD_GROUPS * SSD_STATE:]

    lane = lax.broadcasted_iota(jnp.int32, (1, LANES), 1)
    a_neg = jnp.where(lane < SSD_HEADS, -jnp.exp(alog), 0.0)
    adt = dt * a_neg
    a_cs = _dot_exact_lhs(lmat, adt)
    a_tot = _dot_exact_lhs(bones, adt)
    a_cs_t = a_cs.T
    mask = lmat.astype(F32) > 0.5

    dt_x = _dot_exact_rhs2(dt, emat)
    eacs_x = _dot_exact_rhs2(jnp.exp(a_cs), emat)
    te_x = _dot_exact_rhs2(jnp.exp(a_tot - a_cs), emat)
    xdt = xs * dt_x
    xw_t = (xdt * te_x).T.astype(BF16)

    lane_r = lax.broadcasted_iota(jnp.int32, (ROWS, LANES), 1)
    lo_half = lane_r < SSD_HEADDIM
    yd = []
    for g in range(SSD_GROUPS):
        ns = slice(g * SSD_STATE, (g + 1) * SSD_STATE)
        cb = _dot_nt(cm[:, ns].astype(BF16), bm[:, ns].astype(BF16))
        for hp in range(SSD_HPG // 2):
            h0 = g * SSD_HPG + 2 * hp
            ms = []
            for h in (h0, h0 + 1):
                seg = a_cs[:, h:h + 1] - a_cs_t[h:h + 1, :]
                ms.append(cb * jnp.exp(jnp.where(mask, seg, -1e30)))
            lhs = jnp.concatenate(ms, axis=1).astype(BF16)
            xp = xdt[:, h0 * SSD_HEADDIM:(h0 + 2) * SSD_HEADDIM]
            rhs = jnp.concatenate([jnp.where(lo_half, xp, 0.0), jnp.where(lo_half, 0.0, xp)],
                                  axis=0).astype(BF16)
            yd.append(_dot(lhs, rhs))
    y_pre = jnp.concatenate(yd, axis=1) + xs * dskip
    return dict(y_pre=y_pre, eacs_x=eacs_x, cm=cm, bm=bm, xw_t=xw_t, a_tot=a_tot)


def _ssd_finish(y, zs, norm_g):
    y = y * zs
    outs = []
    for g in range(SSD_GROUPS):
        yg = y[:, g * GROUP_W:(g + 1) * GROUP_W]
        ms = jnp.mean(yg * yg, axis=-1, keepdims=True)
        outs.append(yg * lax.rsqrt(ms + LN_EPS))
    return jnp.concatenate(outs, axis=1) * norm_g


def _ssd_prompt_kernel(*refs):
    y_ref = refs[12]

    @pl.when(pl.program_id(0) < BATCH)
    def _():
        _ssd_prompt_body(*refs)

    @pl.when(pl.program_id(0) == BATCH)
    def _():
        y_ref[...] = jnp.zeros_like(y_ref)


def _ssd_prompt_body(xbc_ref, dt_ref, zs_ref, alog_ref, cw_ref, cb_ref, l_ref, ones_ref, e_ref,
                     et_ref, dskip_ref, ng_ref, y_ref, st_ref, xpad, state):
    c = pl.program_id(1)

    @pl.when(c == 0)
    def _():
        xpad[0:8, :] = jnp.zeros((8, CONV_DIM), F32)
        state[...] = jnp.zeros_like(state)

    xpad[8:8 + ROWS, :] = xbc_ref[...].astype(F32)

    def xwin(s):
        return xpad[8 - s:8 - s + ROWS, :]

    r = _ssd_block(xwin, None, dt_ref[...], alog_ref[...], cw_ref[...], cb_ref[...], l_ref[...],
                   ones_ref[...], e_ref[...], dskip_ref[...], None)
    xpad[0:8, :] = xpad[ROWS:ROWS + 8, :]

    st = state[...]
    st_b = st.astype(BF16)
    y_off, s_new = [], []
    for g in range(SSD_GROUPS):
        ns = slice(g * SSD_STATE, (g + 1) * SSD_STATE)
        gs = slice(g * GROUP_W, (g + 1) * GROUP_W)
        y_off.append(_dot_nt(r["cm"][:, ns].astype(BF16), st_b[gs, :]))
        s_new.append(_dot(r["xw_t"][gs, :], r["bm"][:, ns].astype(BF16)))
    y = r["y_pre"] + jnp.concatenate(y_off, axis=1) * r["eacs_x"]
    y_ref[...] = _ssd_finish(y, zs_ref[...].astype(F32), ng_ref[...]).astype(y_ref.dtype)

    decay = _dot_exact_lhs(et_ref[...], jnp.exp(r["a_tot"].T))
    new_state = decay * st + jnp.concatenate(s_new, axis=0)
    state[...] = new_state
    st_ref[0] = new_state


def _ssd_consts(kind):
    tril = jnp.tril(jnp.ones((ROWS, ROWS), F32))
    if kind == "prompt":
        lmat, bones = tril, jnp.ones((ROWS, ROWS), F32)
    else:
        eye = jnp.eye(ROWS // DEC_SEQ, dtype=F32)
        blk = jnp.kron(eye, jnp.ones((DEC_SEQ, DEC_SEQ), F32))
        lmat, bones = tril * blk, blk
    head = jnp.arange(SSD_INNER) // SSD_HEADDIM
    emat = (jnp.arange(LANES)[:, None] == head[None, :]).astype(BF16)
    return lmat.astype(BF16), bones.astype(BF16), emat, emat.T


def _const_spec(shape):
    nd = len(shape)
    return pl.BlockSpec(shape, lambda *_: (0,) * nd)


def _ssd_prompt(xbc, dt, zs, alog, conv_w, conv_b, dskip, norm_g):
    lmat, bones, emat, emat_t = _ssd_consts("prompt")
    nc = SEQ // ROWS
    n_blk = N_PROMPT // ROWS
    row = lambda b, c: (jnp.minimum(b * nc + c, n_blk - 1), 0)
    row_out = lambda b, c: (jnp.where(b < BATCH, b * nc + c, n_blk + jnp.minimum(c, N_SAMPLE // ROWS - 1)), 0)
    consts = [alog, conv_w, conv_b, lmat, bones, emat, emat_t, dskip, norm_g]
    return pl.pallas_call(
        _ssd_prompt_kernel,
        grid=(BATCH + 1, nc),
        in_specs=[pl.BlockSpec((ROWS, CONV_DIM), row), pl.BlockSpec((ROWS, LANES), row),
                  pl.BlockSpec((ROWS, SSD_INNER), row)] + [_const_spec(a.shape) for a in consts],
        out_specs=[pl.BlockSpec((ROWS, SSD_INNER), row_out),
                   pl.BlockSpec((1, SSD_INNER, SSD_STATE), lambda b, c: (jnp.minimum(b, BATCH - 1), 0, 0))],
        out_shape=[jax.ShapeDtypeStruct((N_TOK, SSD_INNER), BF16),
                   jax.ShapeDtypeStruct((BATCH, SSD_INNER, SSD_STATE), F32)],
        scratch_shapes=[pltpu.VMEM((ROWS + 8, CONV_DIM), F32), pltpu.VMEM((SSD_INNER, SSD_STATE), F32)],
        compiler_params=_params(("arbitrary", "arbitrary")), name="ssd_prompt",
    )(xbc, dt, zs, *consts)


SEQ_PER_STEP = 4
Q_ROWS = SEQ_PER_STEP * DEC_SEQ
N_QUARTER = ROWS // Q_ROWS


def _ssd_sample_kernel(*refs, n_alias):
    (xbc_ref, prev_ref, dt_ref, zs_ref, st_in, alog_ref, cw_ref, cb_ref, l_ref, ones_ref, e_ref, et_ref,
     dskip_ref, ng_ref) = refs[:14]
    (y_ref, st_out, xpad, ppad, ypre_s, eacs_s, cm_s, bm_s, xwt_s, eat_s) = refs[14 + n_alias:]
    q = pl.program_id(1)

    @pl.when(q == 0)
    def _():
        xpad[0:8, :] = jnp.zeros((8, CONV_DIM), F32)
        xpad[8:8 + ROWS, :] = xbc_ref[...].astype(F32)
        ppad[0:ROWS, :] = prev_ref[...]
        ppad[ROWS:ROWS + 8, :] = jnp.zeros((8, CONV_DIM), F32)
        row_in_seq = lax.broadcasted_iota(jnp.int32, (ROWS, 1), 0) % DEC_SEQ

        def xwin(s):
            return xpad[8 - s:8 - s + ROWS, :]

        def pwin(s):
            return ppad[8 - s:8 - s + ROWS, :]

        r = _ssd_block(xwin, pwin, dt_ref[...], alog_ref[...], cw_ref[...], cb_ref[...], l_ref[...],
                       ones_ref[...], e_ref[...], dskip_ref[...], row_in_seq)
        ypre_s[...] = r["y_pre"]
        eacs_s[...] = r["eacs_x"]
        cm_s[...] = r["cm"]
        bm_s[...] = r["bm"]
        xwt_s[...] = r["xw_t"]
        eat_s[...] = jnp.exp(r["a_tot"].T)

    q0 = pl.multiple_of(q * Q_ROWS, Q_ROWS)
    cq = cm_s[pl.ds(q0, Q_ROWS), :].astype(BF16)
    row_q = lax.broadcasted_iota(jnp.int32, (Q_ROWS, 1), 0) // DEC_SEQ
    row_b = lax.broadcasted_iota(jnp.int32, (ROWS, 1), 0) // DEC_SEQ
    lane_b = lax.broadcasted_iota(jnp.int32, (1, LANES), 1) // DEC_SEQ
    e_atot_t = eat_s[...]
    et = et_ref[...]
    y_off = jnp.zeros((Q_ROWS, SSD_INNER), F32)
    for s in range(SEQ_PER_STEP):
        seq = q * SEQ_PER_STEP + s
        st = st_in[0, s]
        st_b = st.astype(BF16)
        bsel = row_b == seq
        yo, s_new = [], []
        for g in range(SSD_GROUPS):
            ns = slice(g * SSD_STATE, (g + 1) * SSD_STATE)
            gs = slice(g * GROUP_W, (g + 1) * GROUP_W)
            yo.append(_dot_nt(cq[:, ns], st_b[gs, :]))
            bm_g = jnp.where(bsel, bm_s[:, ns], 0.0).astype(BF16)
            s_new.append(_dot(xwt_s[gs, :], bm_g))
        y_off = y_off + jnp.where(row_q == s, jnp.concatenate(yo, axis=1), 0.0)
        dec_col = jnp.sum(jnp.where(lane_b == seq, e_atot_t, 0.0), axis=1, keepdims=True) * (1.0 / DEC_SEQ)
        decay = _dot_exact_lhs(et, jnp.broadcast_to(dec_col, (LANES, SSD_STATE)))
        new_state = decay * st + jnp.concatenate(s_new, axis=0)
        for d in range(st_out.shape[0]):
            st_out[d, s] = new_state

    y = ypre_s[pl.ds(q0, Q_ROWS), :] + y_off * eacs_s[pl.ds(q0, Q_ROWS), :]
    y_ref[...] = _ssd_finish(y, zs_ref[...].astype(F32), ng_ref[...]).astype(y_ref.dtype)


def _ssd_sample(layer, xbc, prev_rows, dt, zs, st_all, y_all, st_out_prev, alog, conv_w, conv_b, dskip, norm_g):
    lmat, bones, emat, emat_t = _ssd_consts("sample")
    blk0 = N_PROMPT // ROWS
    qblk0 = N_PROMPT // Q_ROWS
    consts = [alog, conv_w, conv_b, lmat, bones, emat, emat_t, dskip, norm_g]
    st_spec = pl.BlockSpec((1, SEQ_PER_STEP, SSD_INNER, SSD_STATE), lambda b, q: (layer, b * N_QUARTER + q, 0, 0))
    if st_out_prev is None:
        assert layer == 0
        st_out_spec = pl.BlockSpec((DEPTH, SEQ_PER_STEP, SSD_INNER, SSD_STATE),
                                   lambda b, q: (0, b * N_QUARTER + q, 0, 0))
    else:
        st_out_spec = st_spec
    aliased = [y_all] + ([] if st_out_prev is None else [st_out_prev])
    n_in = 5 + len(consts)
    aliases = {n_in: 0} if st_out_prev is None else {n_in: 0, n_in + 1: 1}
    return pl.pallas_call(
        functools.partial(_ssd_sample_kernel, n_alias=len(aliased)),
        grid=(N_SAMPLE // ROWS, N_QUARTER),
        in_specs=[pl.BlockSpec((ROWS, CONV_DIM), lambda b, q: (blk0 + b, 0)),
                  pl.BlockSpec((ROWS, CONV_DIM), lambda b, q: (b, 0)),
                  pl.BlockSpec((ROWS, LANES), lambda b, q: (blk0 + b, 0)),
                  pl.BlockSpec((Q_ROWS, SSD_INNER), lambda b, q: (qblk0 + b * N_QUARTER + q, 0)),
                  st_spec] + [_const_spec(a.shape) for a in consts]
                 + [pl.BlockSpec(memory_space=pl.ANY) for _ in aliased],
        out_specs=[pl.BlockSpec((Q_ROWS, SSD_INNER), lambda b, q: (qblk0 + b * N_QUARTER + q, 0)), st_out_spec],
        out_shape=[jax.ShapeDtypeStruct((N_TOK, SSD_INNER), BF16),
                   jax.ShapeDtypeStruct((DEPTH, DEC_BATCH, SSD_INNER, SSD_STATE), F32)],
        scratch_shapes=[pltpu.VMEM((ROWS + 8, CONV_DIM), F32), pltpu.VMEM((ROWS + 8, CONV_DIM), F32),
                        pltpu.VMEM((ROWS, SSD_INNER), F32), pltpu.VMEM((ROWS, SSD_INNER), F32),
                        pltpu.VMEM((ROWS, SSD_GROUPS * SSD_STATE), F32),
                        pltpu.VMEM((ROWS, SSD_GROUPS * SSD_STATE), F32),
                        pltpu.VMEM((SSD_INNER, ROWS), BF16), pltpu.VMEM((LANES, ROWS), F32)],
        input_output_aliases=aliases,
        compiler_params=_params(("arbitrary", "arbitrary")), name="ssd_sample",
    )(xbc, prev_rows, dt, zs, st_all, *consts, *aliased)


def _xattn_prompt_kernel(q_ref, k_ref, v_ref, o_ref):
    @pl.when(pl.program_id(0) < BATCH)
    def _():
        q = q_ref[...]
        for h in range(XA_HEADS):
            hs = slice(h * XA_HEADDIM, (h + 1) * XA_HEADDIM)
            sc = _dot_nt(q[:, hs], k_ref[0, :, hs])
            p = jnp.exp(sc - jnp.max(sc, axis=-1, keepdims=True))
            p = (p / jnp.sum(p, axis=-1, keepdims=True)).astype(BF16)
            o_ref[:, hs] = _dot(p, v_ref[0, :, hs]).astype(o_ref.dtype)

    @pl.when(pl.program_id(0) == BATCH)
    def _():
        o_ref[...] = jnp.zeros_like(o_ref)


XA_Q_TILE = 512
XA_SAMPLE_SEQS = 4
XA_SAMPLE_ROWS = XA_SAMPLE_SEQS * DEC_SEQ


def _xattn_prompt(q, mem_k, mem_v):
    nq = SEQ // XA_Q_TILE
    n_blk = N_PROMPT // XA_Q_TILE
    kv_spec = pl.BlockSpec((1, N_MEM, XA_WIDTH), lambda b, j: (jnp.minimum(b, BATCH - 1), 0, 0))
    row_out = lambda b, j: (jnp.where(b < BATCH, b * nq + j, n_blk + jnp.minimum(j, N_SAMPLE // XA_Q_TILE - 1)), 0)
    return pl.pallas_call(
        _xattn_prompt_kernel,
        grid=(BATCH + 1, nq),
        in_specs=[pl.BlockSpec((XA_Q_TILE, XA_WIDTH), lambda b, j: (jnp.minimum(b * nq + j, n_blk - 1), 0)),
                  kv_spec, kv_spec],
        out_specs=pl.BlockSpec((XA_Q_TILE, XA_WIDTH), row_out),
        out_shape=jax.ShapeDtypeStruct((N_TOK, XA_WIDTH), BF16),
        compiler_params=_params(("arbitrary", "arbitrary")), name="xattn_prompt",
    )(q, mem_k, mem_v)


def _xattn_sample_kernel(q_ref, k_ref, v_ref, y_all, o_ref):
    del y_all
    rows = XA_SAMPLE_ROWS
    q = q_ref[...]
    qblk = jnp.concatenate([q[:, h * XA_HEADDIM:(h + 1) * XA_HEADDIM] for h in range(XA_HEADS)], axis=0)
    mem_head = lax.broadcasted_iota(jnp.int32, (N_MEM * XA_HEADS, 1), 0) % XA_HEADS
    col_head = lax.broadcasted_iota(jnp.int32, (1, XA_HEADS * rows), 1) // rows
    same_head = mem_head == col_head
    row_seq = (lax.broadcasted_iota(jnp.int32, (XA_HEADS * rows, 1), 0) % rows) // DEC_SEQ
    out = jnp.zeros((XA_HEADS * rows, XA_HEADDIM), F32)
    for s in range(XA_SAMPLE_SEQS):
        k2 = k_ref[0, s].reshape(N_MEM * XA_HEADS, XA_HEADDIM).astype(BF16)
        v2 = v_ref[0, s].reshape(N_MEM * XA_HEADS, XA_HEADDIM).astype(BF16)
        sc = jnp.where(same_head, _dot_nt(k2, qblk), -1e30)
        p = jnp.exp(sc - jnp.max(sc, axis=0, keepdims=True))
        p = (p / jnp.sum(p, axis=0, keepdims=True)).astype(BF16)
        y = lax.dot_general(p, v2, (((0,), (0,)), ((), ())), preferred_element_type=F32)
        out = jnp.where(row_seq == s, y, out)
    for h in range(XA_HEADS):
        o_ref[:, h * XA_HEADDIM:(h + 1) * XA_HEADDIM] = out[h * rows:(h + 1) * rows].astype(o_ref.dtype)


def _xattn_sample(layer, q, cache_k, cache_v, y_all):
    rows = XA_SAMPLE_ROWS
    blk0 = N_PROMPT // rows
    kv_spec = pl.BlockSpec((1, XA_SAMPLE_SEQS, N_MEM, XA_HEADS, XA_HEADDIM), lambda j: (layer, j, 0, 0, 0))
    return pl.pallas_call(
        _xattn_sample_kernel,
        grid=(DEC_BATCH // XA_SAMPLE_SEQS,),
        in_specs=[pl.BlockSpec((rows, XA_WIDTH), lambda j: (blk0 + j, 0)), kv_spec, kv_spec,
                  pl.BlockSpec(memory_space=pl.ANY)],
        out_specs=pl.BlockSpec((rows, XA_WIDTH), lambda j: (blk0 + j, 0)),
        out_shape=jax.ShapeDtypeStruct((N_TOK, XA_WIDTH), BF16),
        input_output_aliases={3: 0},
        compiler_params=_params(("parallel",)), name="xattn_sample",
    )(q, cache_k, cache_v, y_all)


MERGE_TILE = 256


def _merge_kernel(ygm, yssd, yxa, gates, x_p, x_s, pgm, pssd, pxa, wout, lng, lnb, o_f, o_b):
    g = gates[...].astype(F32)
    m = _dot(ygm[...], pgm[...]) * g[:, :D_MODEL]
    m = m + _dot(yssd[...], pssd[...]) * g[:, D_MODEL:2 * D_MODEL]
    m = m + _dot(yxa[...], pxa[...]) * g[:, 2 * D_MODEL:]
    h = _dot(m.astype(BF16), wout[...])
    x = jnp.where(pl.program_id(0) < N_PROMPT // MERGE_TILE, x_p[...], x_s[...])
    y = _layer_norm(ALPHA * x + h, lng[...], lnb[...])
    o_f[...] = y
    o_b[...] = y.astype(BF16)


def _merge(ygm, yssd, yxa, gates, x_p, x_s, x_s_row0, pgm, pssd, pxa, wout, lng, lnb):
    tm = MERGE_TILE
    npt = N_PROMPT // tm
    row = lambda i: (i, 0)
    acts = [ygm, yssd, yxa, gates]
    consts = [pgm, pssd, pxa, wout, lng, lnb]
    return pl.pallas_call(
        _merge_kernel,
        grid=(N_TOK // tm,),
        in_specs=[pl.BlockSpec((tm, a.shape[1]), row) for a in acts]
                 + [pl.BlockSpec((tm, D_MODEL), lambda i: (jnp.minimum(i, npt - 1), 0)),
                    pl.BlockSpec((tm, D_MODEL), lambda i: (x_s_row0 // tm + jnp.maximum(i - npt, 0), 0))]
                 + [_const_spec(a.shape) for a in consts],
        out_specs=[pl.BlockSpec((tm, D_MODEL), row), pl.BlockSpec((tm, D_MODEL), row)],
        out_shape=[jax.ShapeDtypeStruct((N_TOK, D_MODEL), F32), jax.ShapeDtypeStruct((N_TOK, D_MODEL), BF16)],
        compiler_params=_params(("parallel",)), name="merge_out_ln",
    )(*acts, x_p, x_s, *consts)


FFN_TILE = 512
FFN_SPLIT = 2


def _ffn_kernel(xb, xf, wg, wu, wd, lng, lnb, o_f, o_b, acc):
    k = pl.program_id(1)

    @pl.when(k == 0)
    def _():
        acc[...] = jnp.zeros_like(acc)

    h = (_silu(_dot(xb[...], wg[...])) * _dot(xb[...], wu[...])).astype(BF16)
    acc[...] += _dot(h, wd[...])

    @pl.when(k == pl.num_programs(1) - 1)
    def _():
        y = _layer_norm(ALPHA * xf[...] + acc[...], lng[...], lnb[...])
        o_f[...] = y
        o_b[...] = y.astype(BF16)


def _ffn(xb, xf, wg, wu, wd, lng, lnb):
    tm, tf = FFN_TILE, D_FF // FFN_SPLIT
    row = lambda i, k: (i, 0)
    return pl.pallas_call(
        _ffn_kernel,
        grid=(N_TOK // tm, FFN_SPLIT),
        in_specs=[pl.BlockSpec((tm, D_MODEL), row), pl.BlockSpec((tm, D_MODEL), row),
                  pl.BlockSpec((D_MODEL, tf), lambda i, k: (0, k)),
                  pl.BlockSpec((D_MODEL, tf), lambda i, k: (0, k)),
                  pl.BlockSpec((tf, D_MODEL), lambda i, k: (k, 0)),
                  _const_spec(lng.shape), _const_spec(lnb.shape)],
        out_specs=[pl.BlockSpec((tm, D_MODEL), row), pl.BlockSpec((tm, D_MODEL), row)],
        out_shape=[jax.ShapeDtypeStruct((N_TOK, D_MODEL), F32), jax.ShapeDtypeStruct((N_TOK, D_MODEL), BF16)],
        scratch_shapes=[pltpu.VMEM((tm, D_MODEL), F32)],
        compiler_params=_params(("parallel", "arbitrary")), name="ffn_swiglu_ln",
    )(xb, xf, wg, wu, wd, lng, lnb)


ROUTER_TILE = 1024


def _router_kernel(x_ref, w_ref, b_ref, l_ref, ri_ref, rg_ref, cnt_ref, count):
    @pl.when(pl.program_id(0) == 0)
    def _():
        count[...] = jnp.zeros_like(count)

    xs = _split3(x_ref[...])
    ws = _split3(w_ref[...])
    logits = b_ref[...] + _dot(xs[0], ws[0])
    for a, b in ((0, 1), (1, 0), (0, 2), (2, 0), (1, 1)):
        logits = logits + _dot(xs[a], ws[b])
    lane = lax.broadcasted_iota(jnp.int32, logits.shape, 1)
    logits = jnp.where(lane < N_EXPERTS, logits, -1e30)
    m1 = jnp.max(logits, axis=-1, keepdims=True)
    i1 = jnp.min(jnp.where(logits == m1, lane, LANES), axis=-1, keepdims=True)
    rest = jnp.where(lane == i1, -1e30, logits)
    m2 = jnp.max(rest, axis=-1, keepdims=True)
    i2 = jnp.min(jnp.where(rest == m2, lane, LANES), axis=-1, keepdims=True)
    e2 = jnp.exp(m2 - m1)
    den = 1.0 + e2

    hit1, hit2 = lane == i1, lane == i2
    assigned = jnp.where(hit1 | hit2, 1.0, 0.0)
    rank = _dot(l_ref[...], assigned.astype(BF16)) + count[...]
    r1 = jnp.sum(jnp.where(hit1, rank, 0.0), axis=-1, keepdims=True).astype(jnp.int32)
    r2 = jnp.sum(jnp.where(hit2, rank, 0.0), axis=-1, keepdims=True).astype(jnp.int32)
    ri_ref[...] = jnp.where(lane == 0, i1, jnp.where(lane == 1, i2, jnp.where(lane == 2, r1,
                            jnp.where(lane == 3, r2, 0))))
    rg_ref[...] = jnp.where(lane == 0, 1.0 / den, jnp.where(lane == 1, e2 / den, 0.0))
    count[...] = count[...] + jnp.sum(assigned, axis=0, keepdims=True)
    cnt_ref[...] = count[...]


def _router(x, w_pad, b_pad):
    tm = ROUTER_TILE
    strict_lower = jnp.tril(jnp.ones((tm, tm), F32), k=-1).astype(BF16)
    return pl.pallas_call(
        _router_kernel,
        grid=(N_TOK // tm,),
        in_specs=[pl.BlockSpec((tm, D_MODEL), lambda i: (i, 0)), _const_spec(w_pad.shape),
                  _const_spec(b_pad.shape), _const_spec(strict_lower.shape)],
        out_specs=[pl.BlockSpec((tm, LANES), lambda i: (i, 0)), pl.BlockSpec((tm, LANES), lambda i: (i, 0)),
                   _const_spec((1, LANES))],
        out_shape=[jax.ShapeDtypeStruct((N_TOK, LANES), jnp.int32), jax.ShapeDtypeStruct((N_TOK, LANES), F32),
                   jax.ShapeDtypeStruct((1, LANES), F32)],
        scratch_shapes=[pltpu.VMEM((1, LANES), F32)],
        compiler_params=_params(("arbitrary",)), name="moe_router",
    )(x, w_pad, b_pad, strict_lower)


EXPERT_TILE = 1024
N_SLOTS = N_TOK * 2 + N_EXPERTS * EXPERT_TILE
N_SLOT_TILES = N_SLOTS // EXPERT_TILE
MOE_TOK_TILE = 1024
MOE_FF_TILE = 512


def _row_copy(src, src_row, dst, dst_row, sem):
    return pltpu.make_async_copy(src.at[pl.ds(src_row, 1)], dst.at[pl.ds(dst_row, 1)], sem)


def _dispatch_kernel(s1_ref, s2_ref, x_ref, xs_in, xs_out, sem):
    del xs_in

    def body(r, carry):
        _row_copy(x_ref, r, xs_out, s1_ref[r], sem).start()
        _row_copy(x_ref, r, xs_out, s2_ref[r], sem).start()
        return carry

    lax.fori_loop(0, MOE_TOK_TILE, body, 0, unroll=8)
    for _ in range(2):
        pltpu.make_async_copy(x_ref, xs_out.at[pl.ds(0, MOE_TOK_TILE)], sem).wait()


def _dispatch(slot1, slot2, x, xs_zero):
    tm = MOE_TOK_TILE
    smem = lambda: pl.BlockSpec((tm,), lambda i: (i,), memory_space=pltpu.SMEM)
    return pl.pallas_call(
        _dispatch_kernel,
        grid=(N_TOK // tm,),
        in_specs=[smem(), smem(), pl.BlockSpec((tm, D_MODEL), lambda i: (i, 0)),
                  pl.BlockSpec(memory_space=pl.ANY)],
        out_specs=pl.BlockSpec(memory_space=pl.ANY),
        out_shape=jax.ShapeDtypeStruct((N_SLOTS, D_MODEL), F32),
        scratch_shapes=[pltpu.SemaphoreType.DMA(())],
        input_output_aliases={3: 0},
        compiler_params=_params(("arbitrary",)), name="moe_dispatch",
    )(slot1, slot2, x, xs_zero)


def _expert_ffn_kernel(te_ref, nv_ref, xs_ref, wg, wu, wd, ys_ref, xb_s):
    del te_ref
    t = pl.program_id(0)
    k = pl.program_id(1)

    @pl.when(t < nv_ref[0])
    def _():
        @pl.when(k == 0)
        def _():
            xb_s[...] = xs_ref[...].astype(BF16)
            ys_ref[...] = jnp.zeros_like(ys_ref)

        xb = xb_s[...]
        h = _silu(_dot(xb, wg[0, 0].astype(BF16))) * _dot(xb, wu[0, 0].astype(BF16))
        ys_ref[...] += _dot(h.astype(BF16), wd[0, 0].astype(BF16))

    @pl.when((t >= nv_ref[0]) & (k == 0))
    def _():
        ys_ref[...] = jnp.zeros_like(ys_ref)


def _expert_ffn(layer, tile_expert, n_valid, xs, wg, wu, wd):
    tm, tf = EXPERT_TILE, MOE_FF_TILE
    nk = E_FF // tf

    def tile(t, nv):
        return jnp.minimum(t, nv[0] - 1)

    def chunk(t, k, nv):
        return jnp.where(t < nv[0], k, nk - 1)

    grid_spec = pltpu.PrefetchScalarGridSpec(
        num_scalar_prefetch=2,
        grid=(N_SLOT_TILES, nk),
        in_specs=[pl.BlockSpec((tm, D_MODEL), lambda t, k, te, nv: (tile(t, nv), 0)),
                  pl.BlockSpec((1, 1, D_MODEL, tf), lambda t, k, te, nv: (layer, te[tile(t, nv)], 0, chunk(t, k, nv))),
                  pl.BlockSpec((1, 1, D_MODEL, tf), lambda t, k, te, nv: (layer, te[tile(t, nv)], 0, chunk(t, k, nv))),
                  pl.BlockSpec((1, 1, tf, D_MODEL), lambda t, k, te, nv: (layer, te[tile(t, nv)], chunk(t, k, nv), 0))],
        out_specs=pl.BlockSpec((tm, D_MODEL), lambda t, k, te, nv: (t, 0)),
        scratch_shapes=[pltpu.VMEM((tm, D_MODEL), BF16)],
    )
    return pl.pallas_call(
        _expert_ffn_kernel, grid_spec=grid_spec,
        out_shape=jax.ShapeDtypeStruct((N_SLOTS, D_MODEL), F32),
        compiler_params=_params(("arbitrary", "arbitrary")), name="moe_expert_ffn",
    )(tile_expert, n_valid, xs, wg, wu, wd)


def _combine_kernel(s1_ref, s2_ref, x_ref, rg_ref, ys_hbm, lng, lnb, o_p, o_s, buf1, buf2, sem):
    tm = MOE_TOK_TILE

    def body(r, carry):
        _row_copy(ys_hbm, s1_ref[r], buf1, r, sem).start()
        _row_copy(ys_hbm, s2_ref[r], buf2, r, sem).start()
        return carry

    lax.fori_loop(0, tm, body, 0, unroll=8)
    pltpu.make_async_copy(ys_hbm.at[pl.ds(0, tm)], buf1, sem).wait()
    pltpu.make_async_copy(ys_hbm.at[pl.ds(0, tm)], buf2, sem).wait()

    g = rg_ref[...]
    f = g[:, 0:1] * buf1[...] + g[:, 1:2] * buf2[...]
    y = _layer_norm(ALPHA * x_ref[...] + f, lng[...], lnb[...])
    i = pl.program_id(0)

    @pl.when(i < N_PROMPT // tm)
    def _():
        o_p[...] = y

    @pl.when(i >= N_PROMPT // tm)
    def _():
        o_s[...] = y


def _combine(slot1, slot2, x, rg, ys, lng, lnb):
    tm = MOE_TOK_TILE
    npt = N_PROMPT // tm
    smem = lambda: pl.BlockSpec((tm,), lambda i: (i,), memory_space=pltpu.SMEM)
    return pl.pallas_call(
        _combine_kernel,
        grid=(N_TOK // tm,),
        in_specs=[smem(), smem(), pl.BlockSpec((tm, D_MODEL), lambda i: (i, 0)),
                  pl.BlockSpec((tm, LANES), lambda i: (i, 0)), pl.BlockSpec(memory_space=pl.ANY),
                  _const_spec(lng.shape), _const_spec(lnb.shape)],
        out_specs=[pl.BlockSpec((tm, D_MODEL), lambda i: (jnp.minimum(i, npt - 1), 0)),
                   pl.BlockSpec((tm, D_MODEL), lambda i: (jnp.maximum(i - npt, 0), 0))],
        out_shape=[jax.ShapeDtypeStruct((N_PROMPT, D_MODEL), F32), jax.ShapeDtypeStruct((N_SAMPLE, D_MODEL), F32)],
        scratch_shapes=[pltpu.VMEM((tm, D_MODEL), F32), pltpu.VMEM((tm, D_MODEL), F32),
                        pltpu.SemaphoreType.DMA(())],
        compiler_params=_params(("arbitrary",)), name="moe_combine_ln",
    )(slot1, slot2, x, rg, ys, lng, lnb)


def _moe(layer, x, router_w, router_b, wg, wu, wd, lng, lnb):
    ri, rg, cnt = _router(x, _pad_lanes(router_w), _pad_lanes(_row(router_b)))
    counts = cnt[0, :N_EXPERTS].astype(jnp.int32)
    padded = (counts + EXPERT_TILE - 1) // EXPERT_TILE * EXPERT_TILE
    ends = jnp.cumsum(padded)
    starts = ends - padded
    slot1 = starts[ri[:, 0]] + ri[:, 2]
    slot2 = starts[ri[:, 1]] + ri[:, 3]
    n_valid = (ends[-1:] // EXPERT_TILE).astype(jnp.int32)
    tile_start = jnp.arange(N_SLOT_TILES, dtype=jnp.int32) * EXPERT_TILE
    tile_expert = jnp.minimum(jnp.sum(tile_start[:, None] >= ends[None, :], axis=1), N_EXPERTS - 1).astype(jnp.int32)

    xs = _dispatch(slot1, slot2, x, jnp.zeros((N_SLOTS, D_MODEL), F32))
    ys = _expert_ffn(layer, tile_expert, n_valid, xs, wg, wu, wd)
    return _combine(slot1, slot2, x, rg, ys, lng, lnb)


def _row(a):
    return a.reshape(1, -1).astype(F32)


def _pad_lanes(a):
    return jnp.pad(a, ((0, 0), (0, LANES - a.shape[1])))


def _gmlp_weights(w_s, b_s):
    tril = jnp.tril(jnp.ones((GM_CHUNK, GM_CHUNK), dtype=bool))
    w_p = jnp.where(tril, w_s, 0.0)
    n_seq = ROWS // DEC_SEQ
    w_8 = w_p[:, :DEC_SEQ, :DEC_SEQ]
    w_d = jnp.einsum("ab,gij->gaibj", jnp.eye(n_seq, dtype=F32), w_8).reshape(GM_GROUPS, ROWS, ROWS)
    bias_p = jnp.repeat(b_s.T, GM_GROUP, axis=1)
    bias_d = jnp.tile(bias_p[:DEC_SEQ], (n_seq, 1))
    return jnp.stack([w_p, w_d]).astype(BF16), jnp.stack([bias_p, bias_d])


def kernel(x_prompt, x_sample, mem_prompt, cache_mem_k, cache_mem_v, state_conv, state_ssm, w_in, conv_w, conv_b, dt_bias, a_log, d_skip, ssd_norm_g, v_ln_g, v_ln_b, w_s, b_s, p_gm, p_ssd, p_xa, w_out, w_mem_k, w_mem_v, ln1_g, ln1_b, ln2_g, ln2_b, ffn_wg, ffn_wu, ffn_wd, router_w, router_b, moe_wg, moe_wu, moe_wd):
    assert DEPTH % 2 == 0
    x_p = x_prompt.reshape(N_PROMPT, D_MODEL)
    x_s = x_sample.reshape(N_SAMPLE, D_MODEL)
    x_s_row0 = 0
    xb = jnp.concatenate([x_p.astype(BF16), x_s.astype(BF16)], axis=0)
    mem_b = mem_prompt.reshape(BATCH * N_MEM, D_MODEL).astype(BF16)
    w_tail = w_in[:, :, OFF_Q:]
    st_all = state_ssm.reshape(DEPTH, DEC_BATCH, SSD_INNER, SSD_STATE)
    rows8 = N_TOK // DEC_SEQ

    mem_k_out, mem_v_out, ssm_p_out, conv_p_out, conv_s_out, v_out = [], [], [], [], [], []
    ssm_s = None
    for i in range(DEPTH):
        mk_f, mk_b = _mm(mem_b, w_mem_k, i, 0, XA_WIDTH, 1024, [], _epi_id2, [F32, BF16], "mem_k")
        mv_f, mv_b = _mm(mem_b, w_mem_v, i, 0, XA_WIDTH, 1024, [], _epi_id2, [F32, BF16], "mem_v")
        mem_k_out.append(mk_f.reshape(BATCH, N_MEM, XA_HEADS, XA_HEADDIM))
        mem_v_out.append(mv_f.reshape(BATCH, N_MEM, XA_HEADS, XA_HEADDIM))

        (u,) = _mm(xb, w_in, i, 0, GM_WIDTH, 1024, [], _epi_gelu, [BF16], "in_u")
        (v,) = _mm(xb, w_in, i, 1, GM_WIDTH, 1024, [_row(v_ln_g[i]), _row(v_ln_b[i])], _epi_gelu_ln, [F32], "in_v")
        (zs,) = _mm(xb, w_in, i, OFF_Z // 1024, SSD_INNER, 1024, [], _epi_silu, [BF16], "in_z")
        (xbc,) = _mm(xb, w_in, i, OFF_XBC // 1024, CONV_DIM, 1024, [], _epi_id, [BF16], "in_xbc")
        (dt,) = _mm(xb, w_in, i, OFF_DT // LANES, LANES, LANES, [_pad_lanes(_row(dt_bias[i]))], _epi_softplus,
                    [F32], "in_dt")
        (q,) = _mm(xb, w_tail, i, 0, XA_WIDTH, 1024, [], _epi_qscale, [BF16], "in_q")
        (gates,) = _mm(xb, w_tail, i, 1, N_BRANCH * D_MODEL, 1024, [], _epi_sigmoid, [BF16], "in_gates")

        gm_w, gm_b = _gmlp_weights(w_s[i], b_s[i])
        y_gm = _gmlp(v, u, gm_w, gm_b)

        alog = _pad_lanes(_row(a_log[i]))
        dskip = _row(jnp.repeat(d_skip[i], SSD_HEADDIM))
        ssd_args = (alog, conv_w[i], _row(conv_b[i]), dskip, _row(ssd_norm_g[i]))
        y_ssd, ssm_p = _ssd_prompt(xbc, dt, zs, *ssd_args)
        prev_rows = jnp.pad(state_conv[i], ((0, 0), (DEC_SEQ - (SSD_CONV - 1), 0), (0, 0))).reshape(N_SAMPLE, CONV_DIM)
        y_ssd, ssm_s = _ssd_sample(i, xbc, prev_rows, dt, zs, st_all, y_ssd, ssm_s, *ssd_args)
        ssm_p_out.append(ssm_p.reshape(BATCH, SSD_HEADS, SSD_HEADDIM, SSD_STATE))
        xbc8 = xbc.reshape(rows8, DEC_SEQ, CONV_DIM)[:, DEC_SEQ - (SSD_CONV - 1):]
        conv_p_out.append(xbc8[SEQ // DEC_SEQ - 1:N_PROMPT // DEC_SEQ:SEQ // DEC_SEQ].astype(F32))
        conv_s_out.append(xbc8[N_PROMPT // DEC_SEQ:].astype(F32))
        v_out.append(v[N_PROMPT:].reshape(DEC_BATCH, DEC_SEQ, GM_WIDTH))

        y_xa = _xattn_prompt(q, mk_b.reshape(BATCH, N_MEM, XA_WIDTH), mv_b.reshape(BATCH, N_MEM, XA_WIDTH))
        y_xa = _xattn_sample(i, q, cache_mem_k, cache_mem_v, y_xa)

        x, xb = _merge(y_gm, y_ssd, y_xa, gates, x_p, x_s, x_s_row0, p_gm[i].astype(BF16), p_ssd[i].astype(BF16),
                       p_xa[i].astype(BF16), w_out[i].astype(BF16), _row(ln1_g[i]), _row(ln1_b[i]))

        j = i // 2
        if i % 2 == 0:
            x, xb = _ffn(xb, x, ffn_wg[j].astype(BF16), ffn_wu[j].astype(BF16), ffn_wd[j].astype(BF16),
                         _row(ln2_g[i]), _row(ln2_b[i]))
            x_p, x_s, x_s_row0 = x, x, N_PROMPT
        else:
            x_p, x_s = _moe(j, x, router_w[j], router_b[j], moe_wg, moe_wu, moe_wd, _row(ln2_g[i]), _row(ln2_b[i]))
            x_s_row0 = 0
            if i + 1 < DEPTH:
                xb = jnp.concatenate([x_p.astype(BF16), x_s.astype(BF16)], axis=0)

    y_prompt = x_p.reshape(BATCH, SEQ, D_MODEL)
    y_sample = x_s.reshape(DEC_BATCH, DEC_SEQ, D_MODEL)
    ssm_s_out = ssm_s.reshape(DEPTH, DEC_BATCH, SSD_HEADS, SSD_HEADDIM, SSD_STATE)
    return (y_prompt, y_sample, jnp.stack(mem_k_out), jnp.stack(mem_v_out), jnp.stack(conv_p_out),
            jnp.stack(ssm_p_out), jnp.stack(conv_s_out), ssm_s_out, jnp.stack(v_out))
```

```python
import functools

import jax
import jax.numpy as jnp
from jax import lax
from jax.experimental import pallas as pl
from jax.experimental.pallas import tpu as pltpu

F32 = jnp.float32
BF16 = jnp.bfloat16

D_MODEL = 1024
BATCH = 8
SEQ = 2048
DEPTH = 2
DEC_BATCH = 128
DEC_SEQ = 8
N_MEM = 256
GM_WIDTH = D_MODEL
GM_CHUNK = 128
GM_GROUP = 128
GM_GROUPS = GM_WIDTH // GM_GROUP
SSD_INNER = 2 * D_MODEL
SSD_HEADDIM = 64
SSD_HEADS = SSD_INNER // SSD_HEADDIM
SSD_STATE = 128
SSD_GROUPS = 4
SSD_HPG = SSD_HEADS // SSD_GROUPS
SSD_CONV = 4
SSD_CHUNK = 128
CONV_DIM = SSD_INNER + 2 * SSD_GROUPS * SSD_STATE
XA_HEADS = 4
XA_HEADDIM = D_MODEL // XA_HEADS
XA_WIDTH = XA_HEADS * XA_HEADDIM
N_BRANCH = 3
D_FF = ((8 * D_MODEL // 3 + 127) // 128) * 128
N_EXPERTS = 8
E_FF = 7 * D_MODEL // 2
ALPHA = (2 * DEPTH) ** 0.25
LN_EPS = 1e-5

N_PROMPT = BATCH * SEQ
N_SAMPLE = DEC_BATCH * DEC_SEQ
N_TOK = N_PROMPT + N_SAMPLE

OFF_Z = 2 * GM_WIDTH
OFF_XBC = OFF_Z + SSD_INNER
OFF_DT = OFF_XBC + CONV_DIM
OFF_Q = OFF_DT + SSD_HEADS
OFF_GATE = OFF_Q + XA_WIDTH

LANES = 128
ROWS = 128
GROUP_W = SSD_HPG * SSD_HEADDIM
VMEM_LIMIT = 56 * 1024 * 1024


def _params(sem):
    return pltpu.CompilerParams(dimension_semantics=sem, vmem_limit_bytes=VMEM_LIMIT)


def _dot(a, b):
    return jnp.dot(a, b, preferred_element_type=F32)


def _dot_nt(a, b):
    return lax.dot_general(a, b, (((1,), (1,)), ((), ())), preferred_element_type=F32)


def _split2(x):
    hi = x.astype(BF16)
    lo = (x - hi.astype(F32)).astype(BF16)
    return hi, lo


def _split3(x):
    hi = x.astype(BF16)
    r = x - hi.astype(F32)
    mid = r.astype(BF16)
    lo = (r - mid.astype(F32)).astype(BF16)
    return hi, mid, lo


def _dot_exact_lhs(m_bf16, x, pieces=3):
    parts = _split3(x) if pieces == 3 else _split2(x)
    n = x.shape[1]
    out = _dot(m_bf16, jnp.concatenate(parts, axis=1))
    return sum(out[:, i * n:(i + 1) * n] for i in range(1, pieces)) + out[:, :n]


def _expand_rhs2(xs, m_bf16):
    rows = xs[0].shape[0]
    out = _dot(jnp.concatenate([p for x in xs for p in _split2(x)], axis=0), m_bf16)
    return [out[2 * i * rows:(2 * i + 1) * rows] + out[(2 * i + 1) * rows:(2 * i + 2) * rows] for i in range(len(xs))]


def _layer_norm(r, g, b):
    mu = jnp.mean(r, axis=-1, keepdims=True)
    c = r - mu
    var = jnp.mean(c * c, axis=-1, keepdims=True)
    return c * lax.rsqrt(var + LN_EPS) * g + b


def _gelu_tanh(x):
    return x * (0.5 * (1.0 + jnp.tanh(0.7978845608028654 * (x + 0.044715 * (x * x * x)))))


def _sigmoid(x):
    return 1.0 / (1.0 + jnp.exp(-x))


def _silu(x):
    return x * _sigmoid(x)


def _softplus(x):
    return jnp.maximum(x, 0.0) + jnp.log(1.0 + jnp.exp(-jnp.abs(x)))


def _mm_kernel(x_ref, w_ref, *rest, epilogue, n_extra, n_out, w_transposed):
    extras = [r[...] for r in rest[:n_extra]]
    outs = rest[n_extra:n_extra + n_out]
    w_bf16 = rest[n_extra + n_out]

    @pl.when(pl.program_id(1) == 0)
    def _():
        w = w_ref[0]
        w_bf16[...] = (w.T if w_transposed else w).astype(BF16)

    acc = _dot(x_ref[...], w_bf16[...])
    res = epilogue(acc, *extras)
    for o, r in zip(outs, res):
        o[...] = r.astype(o.dtype)


MM_TILE = 2176


def _mm(x, w, layer, col_block0, n_cols, tn, extras, epilogue, out_dtypes, name, tm=MM_TILE, w_transposed=False):
    t, k = x.shape
    grid = (n_cols // tn, t // tm)
    if w_transposed:
        w_spec = pl.BlockSpec((1, tn, k), lambda j, i: (layer, j + col_block0, 0))
    else:
        w_spec = pl.BlockSpec((1, k, tn), lambda j, i: (layer, 0, j + col_block0))
    in_specs = [pl.BlockSpec((tm, k), lambda j, i: (i, 0)), w_spec]
    in_specs += [pl.BlockSpec((1, tn), lambda j, i: (0, j)) for _ in extras]
    out_specs = [pl.BlockSpec((tm, tn), lambda j, i: (i, j)) for _ in out_dtypes]
    out_shape = [jax.ShapeDtypeStruct((t, n_cols), d) for d in out_dtypes]
    return pl.pallas_call(
        functools.partial(_mm_kernel, epilogue=epilogue, n_extra=len(extras), n_out=len(out_dtypes),
                          w_transposed=w_transposed),
        grid=grid, in_specs=in_specs, out_specs=out_specs, out_shape=out_shape,
        scratch_shapes=[pltpu.VMEM((k, tn), BF16)],
        compiler_params=_params(("parallel", "arbitrary")), name=name,
    )(x, w, *extras)


def _epi_gelu(acc):
    return (_gelu_tanh(acc),)


def _epi_gelu_ln(acc, g, b):
    return (_layer_norm(_gelu_tanh(acc), g, b),)


def _epi_silu(acc):
    return (_silu(acc),)


def _epi_id(acc):
    return (acc,)


def _epi_id2(acc):
    return (acc, acc)


def _epi_softplus(acc, bias):
    return (_softplus(acc + bias),)


def _epi_qscale(acc):
    return (acc * (XA_HEADDIM ** -0.5),)


def _epi_sigmoid(acc):
    return (_sigmoid(acc),)


GM_TILE = 512


def _gmlp_kernel(v_ref, u_ref, w_ref, b_ref, o_ref):
    for c in range(GM_TILE // ROWS):
        rs = slice(c * ROWS, (c + 1) * ROWS)
        for g in range(GM_GROUPS):
            cs = slice(g * GM_GROUP, (g + 1) * GM_GROUP)
            z = _dot(w_ref[0, g], v_ref[rs, cs].astype(BF16)) + b_ref[0, :, cs]
            o_ref[rs, cs] = (u_ref[rs, cs].astype(F32) * z).astype(o_ref.dtype)


def _gmlp(v, u, w2, b2):
    n_prompt_tiles = N_PROMPT // GM_TILE

    def sel(i):
        return jnp.where(i >= n_prompt_tiles, 1, 0)

    return pl.pallas_call(
        _gmlp_kernel,
        grid=(N_TOK // GM_TILE,),
        in_specs=[pl.BlockSpec((GM_TILE, GM_WIDTH), lambda i: (i, 0)),
                  pl.BlockSpec((GM_TILE, GM_WIDTH), lambda i: (i, 0)),
                  pl.BlockSpec((1, GM_GROUPS, ROWS, ROWS), lambda i: (sel(i), 0, 0, 0)),
                  pl.BlockSpec((1, ROWS, GM_WIDTH), lambda i: (sel(i), 0, 0))],
        out_specs=pl.BlockSpec((GM_TILE, GM_WIDTH), lambda i: (i, 0)),
        out_shape=jax.ShapeDtypeStruct((N_TOK, GM_WIDTH), BF16),
        compiler_params=_params(("parallel",)), name="gmlp_spatial",
    )(v, u, w2, b2)


def _ssd_block(x_back, dt, alog, conv_w, conv_b, lmat, bones, emat, dskip):
    conv = conv_b + x_back(0) * conv_w[3:4, :]
    for s in range(1, SSD_CONV):
        conv = conv + x_back(s) * conv_w[3 - s:4 - s, :]
    xc = _silu(conv)
    xs = xc[:, :SSD_INNER]
    bm = xc[:, SSD_INNER:SSD_INNER + SSD_GROUPS * SSD_STATE]
    cm = xc[:, SSD_INNER + SSD_GROUPS * SSD_STATE:]

    lane = lax.broadcasted_iota(jnp.int32, (1, LANES), 1)
    a_neg = jnp.where(lane < SSD_HEADS, -jnp.exp(alog), 0.0)
    adt = dt * a_neg
    sums = _dot_exact_lhs(jnp.concatenate([lmat, bones], axis=0), adt)
    a_cs, a_tot = sums[:ROWS], sums[ROWS:]
    a_cs_t = a_cs.T
    mask = lmat.astype(F32) > 0.5

    dt_x, eacs_x, te_x = _expand_rhs2([dt, jnp.exp(a_cs), jnp.exp(a_tot - a_cs)], emat)
    xdt = xs * dt_x
    xw_t = (xdt * te_x).T.astype(BF16)

    lane_r = lax.broadcasted_iota(jnp.int32, (ROWS, LANES), 1)
    lo_half = lane_r < SSD_HEADDIM
    yd = []
    for g in range(SSD_GROUPS):
        ns = slice(g * SSD_STATE, (g + 1) * SSD_STATE)
        cb = _dot_nt(cm[:, ns].astype(BF16), bm[:, ns].astype(BF16))
        for hp in range(SSD_HPG // 2):
            h0 = g * SSD_HPG + 2 * hp
            ms = []
            for h in (h0, h0 + 1):
                seg = a_cs[:, h:h + 1] - a_cs_t[h:h + 1, :]
                ms.append(cb * jnp.exp(jnp.where(mask, seg, -1e30)))
            lhs = jnp.concatenate(ms, axis=1).astype(BF16)
            xp = xdt[:, h0 * SSD_HEADDIM:(h0 + 2) * SSD_HEADDIM]
            rhs = jnp.concatenate([jnp.where(lo_half, xp, 0.0), jnp.where(lo_half, 0.0, xp)],
                                  axis=0).astype(BF16)
            yd.append(_dot(lhs, rhs))
    y_pre = jnp.concatenate(yd, axis=1) + xs * dskip
    return dict(y_pre=y_pre, eacs_x=eacs_x, cm=cm, bm=bm, xw_t=xw_t, a_tot=a_tot)


def _ssd_finish(y, zs, norm_g):
    y = y * zs
    outs = []
    for g in range(SSD_GROUPS):
        yg = y[:, g * GROUP_W:(g + 1) * GROUP_W]
        ms = jnp.mean(yg * yg, axis=-1, keepdims=True)
        outs.append(yg * lax.rsqrt(ms + LN_EPS))
    return jnp.concatenate(outs, axis=1) * norm_g


def _ssd_prompt_kernel(*refs):
    y_ref = refs[12]

    @pl.when(pl.program_id(0) < BATCH)
    def _():
        _ssd_prompt_body(*refs)

    @pl.when(pl.program_id(0) == BATCH)
    def _():
        y_ref[...] = jnp.zeros_like(y_ref)


def _ssd_prompt_body(xbc_ref, dt_ref, zs_ref, alog_ref, cw_ref, cb_ref, l_ref, ones_ref, e_ref,
                     et_ref, dskip_ref, ng_ref, y_ref, st_ref, xpad, state):
    c = pl.program_id(1)

    @pl.when(c == 0)
    def _():
        xpad[0:8, :] = jnp.zeros((8, CONV_DIM), F32)
        state[...] = jnp.zeros_like(state)

    xpad[8:8 + ROWS, :] = xbc_ref[...].astype(F32)
    r = _ssd_block(lambda s: xpad[8 - s:8 - s + ROWS, :], dt_ref[...], alog_ref[...], cw_ref[...], cb_ref[...],
                   l_ref[...], ones_ref[...], e_ref[...], dskip_ref[...])
    xpad[0:8, :] = xpad[ROWS:ROWS + 8, :]

    st = state[...]
    st_b = st.astype(BF16)
    y_off, s_new = [], []
    for g in range(SSD_GROUPS):
        ns = slice(g * SSD_STATE, (g + 1) * SSD_STATE)
        gs = slice(g * GROUP_W, (g + 1) * GROUP_W)
        y_off.append(_dot_nt(r["cm"][:, ns].astype(BF16), st_b[gs, :]))
        s_new.append(_dot(r["xw_t"][gs, :], r["bm"][:, ns].astype(BF16)))
    y = r["y_pre"] + jnp.concatenate(y_off, axis=1) * r["eacs_x"]
    y_ref[...] = _ssd_finish(y, zs_ref[...].astype(F32), ng_ref[...]).astype(y_ref.dtype)

    decay = _dot_exact_lhs(et_ref[...], jnp.exp(r["a_tot"].T), pieces=2)
    new_state = decay * st + jnp.concatenate(s_new, axis=0)
    state[...] = new_state
    st_ref[0] = new_state


def _ssd_consts(kind):
    tril = jnp.tril(jnp.ones((ROWS, ROWS), F32))
    if kind == "prompt":
        lmat, bones = tril, jnp.ones((ROWS, ROWS), F32)
    else:
        eye = jnp.eye(ROWS // DEC_SEQ, dtype=F32)
        blk = jnp.kron(eye, jnp.ones((DEC_SEQ, DEC_SEQ), F32))
        lmat, bones = tril * blk, blk
    head = jnp.arange(SSD_INNER) // SSD_HEADDIM
    emat = (jnp.arange(LANES)[:, None] == head[None, :]).astype(BF16)
    if kind == "prompt":
        return lmat.astype(BF16), bones.astype(BF16), emat, emat.T, None
    t = jnp.arange(ROWS)[:, None]
    col = jnp.arange(3 * ROWS)[None, :]
    shifts = []
    for s in range(1, SSD_CONV):
        inside = t % DEC_SEQ >= s
        earlier = (col % ROWS == t + DEC_SEQ - s) & (col >= ROWS)
        shifts.append(jnp.where(inside, col == t - s, earlier))
    shift = jnp.concatenate(shifts, axis=0).astype(BF16)
    return lmat.astype(BF16), bones.astype(BF16), emat, emat.T, shift


def _const_spec(shape):
    nd = len(shape)
    return pl.BlockSpec(shape, lambda *_: (0,) * nd)


def _ssd_prompt(xbc, dt, zs, alog, conv_w, conv_b, dskip, norm_g):
    lmat, bones, emat, emat_t, _ = _ssd_consts("prompt")
    nc = SEQ // ROWS
    n_blk = N_PROMPT // ROWS
    row = lambda b, c: (jnp.minimum(b * nc + c, n_blk - 1), 0)
    row_out = lambda b, c: (jnp.where(b < BATCH, b * nc + c, n_blk + jnp.minimum(c, N_SAMPLE // ROWS - 1)), 0)
    consts = [alog, conv_w, conv_b, lmat, bones, emat, emat_t, dskip, norm_g]
    return pl.pallas_call(
        _ssd_prompt_kernel,
        grid=(BATCH + 1, nc),
        in_specs=[pl.BlockSpec((ROWS, CONV_DIM), row), pl.BlockSpec((ROWS, LANES), row),
                  pl.BlockSpec((ROWS, SSD_INNER), row)] + [_const_spec(a.shape) for a in consts],
        out_specs=[pl.BlockSpec((ROWS, SSD_INNER), row_out),
                   pl.BlockSpec((1, SSD_INNER, SSD_STATE), lambda b, c: (jnp.minimum(b, BATCH - 1), 0, 0))],
        out_shape=[jax.ShapeDtypeStruct((N_TOK, SSD_INNER), BF16),
                   jax.ShapeDtypeStruct((BATCH, SSD_INNER, SSD_STATE), F32)],
        scratch_shapes=[pltpu.VMEM((ROWS + 8, CONV_DIM), F32), pltpu.VMEM((SSD_INNER, SSD_STATE), F32)],
        compiler_params=_params(("arbitrary", "arbitrary")), name="ssd_prompt",
    )(xbc, dt, zs, *consts)


SEQ_PER_STEP = 4
Q_ROWS = SEQ_PER_STEP * DEC_SEQ
N_QUARTER = ROWS // Q_ROWS


def _ssd_sample_kernel(*refs, n_alias):
    (xbc_ref, prev_ref, dt_ref, zs_ref, st_in, alog_ref, cw_ref, cb_ref, sh_ref, l_ref, ones_ref, e_ref, et_ref,
     dskip_ref, ng_ref) = refs[:15]
    (y_ref, st_out, ypre_s, eacs_s, cm_s, bm_s, xwt_s, eat_s) = refs[15 + n_alias:]
    q = pl.program_id(1)

    @pl.when(q == 0)
    def _():
        x_cur = xbc_ref[...]
        p_hi, p_lo = _split2(prev_ref[...])
        back = _dot(sh_ref[...], jnp.concatenate([x_cur, p_hi, p_lo], axis=0))
        r = _ssd_block(lambda s: x_cur.astype(F32) if s == 0 else back[(s - 1) * ROWS:s * ROWS, :], dt_ref[...],
                       alog_ref[...], cw_ref[...], cb_ref[...], l_ref[...], ones_ref[...], e_ref[...],
                       dskip_ref[...])
        ypre_s[...] = r["y_pre"]
        eacs_s[...] = r["eacs_x"]
        cm_s[...] = r["cm"]
        bm_s[...] = r["bm"]
        xwt_s[...] = r["xw_t"]
        eat_s[...] = jnp.exp(r["a_tot"].T)

    q0 = pl.multiple_of(q * Q_ROWS, Q_ROWS)
    cq = cm_s[pl.ds(q0, Q_ROWS), :].astype(BF16)
    row_q = lax.broadcasted_iota(jnp.int32, (Q_ROWS, 1), 0) // DEC_SEQ
    row_b = lax.broadcasted_iota(jnp.int32, (ROWS, 1), 0) // DEC_SEQ
    lane_b = lax.broadcasted_iota(jnp.int32, (1, LANES), 1) // DEC_SEQ
    e_atot_t = eat_s[...]
    et = et_ref[...]
    y_off = jnp.zeros((Q_ROWS, SSD_INNER), F32)
    for s in range(SEQ_PER_STEP):
        seq = q * SEQ_PER_STEP + s
        st = st_in[0, s]
        st_b = st.astype(BF16)
        bsel = row_b == seq
        yo, s_new = [], []
        for g in range(SSD_GROUPS):
            ns = slice(g * SSD_STATE, (g + 1) * SSD_STATE)
            gs = slice(g * GROUP_W, (g + 1) * GROUP_W)
            yo.append(_dot_nt(cq[:, ns], st_b[gs, :]))
            bm_g = jnp.where(bsel, bm_s[:, ns], 0.0).astype(BF16)
            s_new.append(_dot(xwt_s[gs, :], bm_g))
        y_off = y_off + jnp.where(row_q == s, jnp.concatenate(yo, axis=1), 0.0)
        dec_col = jnp.sum(jnp.where(lane_b == seq, e_atot_t, 0.0), axis=1, keepdims=True) * (1.0 / DEC_SEQ)
        decay = _dot_exact_lhs(et, jnp.broadcast_to(dec_col, (LANES, SSD_STATE)), pieces=2)
        new_state = decay * st + jnp.concatenate(s_new, axis=0)
        for d in range(st_out.shape[0]):
            st_out[d, s] = new_state

    y = ypre_s[pl.ds(q0, Q_ROWS), :] + y_off * eacs_s[pl.ds(q0, Q_ROWS), :]
    y_ref[...] = _ssd_finish(y, zs_ref[...].astype(F32), ng_ref[...]).astype(y_ref.dtype)


def _ssd_sample(layer, xbc, prev_rows, dt, zs, st_all, y_all, st_out_prev, alog, conv_w, conv_b, dskip, norm_g):
    lmat, bones, emat, emat_t, shift = _ssd_consts("sample")
    blk0 = N_PROMPT // ROWS
    qblk0 = N_PROMPT // Q_ROWS
    consts = [alog, conv_w, conv_b, shift, lmat, bones, emat, emat_t, dskip, norm_g]
    st_spec = pl.BlockSpec((1, SEQ_PER_STEP, SSD_INNER, SSD_STATE), lambda b, q: (layer, b * N_QUARTER + q, 0, 0))
    if st_out_prev is None:
        assert layer == 0
        st_out_spec = pl.BlockSpec((DEPTH, SEQ_PER_STEP, SSD_INNER, SSD_STATE),
                                   lambda b, q: (0, b * N_QUARTER + q, 0, 0))
    else:
        st_out_spec = st_spec
    aliased = [y_all] + ([] if st_out_prev is None else [st_out_prev])
    n_in = 5 + len(consts)
    aliases = {n_in: 0} if st_out_prev is None else {n_in: 0, n_in + 1: 1}
    return pl.pallas_call(
        functools.partial(_ssd_sample_kernel, n_alias=len(aliased)),
        grid=(N_SAMPLE // ROWS, N_QUARTER),
        in_specs=[pl.BlockSpec((ROWS, CONV_DIM), lambda b, q: (blk0 + b, 0)),
                  pl.BlockSpec((ROWS, CONV_DIM), lambda b, q: (b, 0)),
                  pl.BlockSpec((ROWS, LANES), lambda b, q: (blk0 + b, 0)),
                  pl.BlockSpec((Q_ROWS, SSD_INNER), lambda b, q: (qblk0 + b * N_QUARTER + q, 0)),
                  st_spec] + [_const_spec(a.shape) for a in consts]
                 + [pl.BlockSpec(memory_space=pl.ANY) for _ in aliased],
        out_specs=[pl.BlockSpec((Q_ROWS, SSD_INNER), lambda b, q: (qblk0 + b * N_QUARTER + q, 0)), st_out_spec],
        out_shape=[jax.ShapeDtypeStruct((N_TOK, SSD_INNER), BF16),
                   jax.ShapeDtypeStruct((DEPTH, DEC_BATCH, SSD_INNER, SSD_STATE), F32)],
        scratch_shapes=[pltpu.VMEM((ROWS, SSD_INNER), F32), pltpu.VMEM((ROWS, SSD_INNER), F32),
                        pltpu.VMEM((ROWS, SSD_GROUPS * SSD_STATE), F32),
                        pltpu.VMEM((ROWS, SSD_GROUPS * SSD_STATE), F32),
                        pltpu.VMEM((SSD_INNER, ROWS), BF16), pltpu.VMEM((LANES, ROWS), F32)],
        input_output_aliases=aliases,
        compiler_params=_params(("arbitrary", "arbitrary")), name="ssd_sample",
    )(xbc, prev_rows, dt, zs, st_all, *consts, *aliased)


def _xattn_prompt_kernel(q_ref, k_ref, v_ref, o_ref):
    @pl.when(pl.program_id(0) < BATCH)
    def _():
        q = q_ref[...]
        for h in range(XA_HEADS):
            hs = slice(h * XA_HEADDIM, (h + 1) * XA_HEADDIM)
            sc = _dot_nt(q[:, hs], k_ref[0, :, hs])
            p = jnp.exp(sc - jnp.max(sc, axis=-1, keepdims=True))
            p = (p / jnp.sum(p, axis=-1, keepdims=True)).astype(BF16)
            o_ref[:, hs] = _dot(p, v_ref[0, :, hs]).astype(o_ref.dtype)

    @pl.when(pl.program_id(0) == BATCH)
    def _():
        o_ref[...] = jnp.zeros_like(o_ref)


XA_Q_TILE = 512
XA_SAMPLE_SEQS = 4
XA_SAMPLE_ROWS = XA_SAMPLE_SEQS * DEC_SEQ


def _xattn_prompt(q, mem_k, mem_v):
    nq = SEQ // XA_Q_TILE
    n_blk = N_PROMPT // XA_Q_TILE
    kv_spec = pl.BlockSpec((1, N_MEM, XA_WIDTH), lambda b, j: (jnp.minimum(b, BATCH - 1), 0, 0))
    row_out = lambda b, j: (jnp.where(b < BATCH, b * nq + j, n_blk + jnp.minimum(j, N_SAMPLE // XA_Q_TILE - 1)), 0)
    return pl.pallas_call(
        _xattn_prompt_kernel,
        grid=(BATCH + 1, nq),
        in_specs=[pl.BlockSpec((XA_Q_TILE, XA_WIDTH), lambda b, j: (jnp.minimum(b * nq + j, n_blk - 1), 0)),
                  kv_spec, kv_spec],
        out_specs=pl.BlockSpec((XA_Q_TILE, XA_WIDTH), row_out),
        out_shape=jax.ShapeDtypeStruct((N_TOK, XA_WIDTH), BF16),
        compiler_params=_params(("arbitrary", "arbitrary")), name="xattn_prompt",
    )(q, mem_k, mem_v)


def _xattn_sample_kernel(q_ref, k_ref, v_ref, y_all, o_ref):
    del y_all
    rows = XA_SAMPLE_ROWS
    q = q_ref[...]
    qblk = jnp.concatenate([q[:, h * XA_HEADDIM:(h + 1) * XA_HEADDIM] for h in range(XA_HEADS)], axis=0)
    mem_head = lax.broadcasted_iota(jnp.int32, (N_MEM * XA_HEADS, 1), 0) % XA_HEADS
    col_head = lax.broadcasted_iota(jnp.int32, (1, XA_HEADS * rows), 1) // rows
    same_head = mem_head == col_head
    row_seq = (lax.broadcasted_iota(jnp.int32, (XA_HEADS * rows, 1), 0) % rows) // DEC_SEQ
    out = jnp.zeros((XA_HEADS * rows, XA_HEADDIM), F32)
    for s in range(XA_SAMPLE_SEQS):
        k2 = k_ref[0, s].reshape(N_MEM * XA_HEADS, XA_HEADDIM).astype(BF16)
        v2 = v_ref[0, s].reshape(N_MEM * XA_HEADS, XA_HEADDIM).astype(BF16)
        sc = jnp.where(same_head, _dot_nt(k2, qblk), -1e30)
        p = jnp.exp(sc - jnp.max(sc, axis=0, keepdims=True))
        p = (p / jnp.sum(p, axis=0, keepdims=True)).astype(BF16)
        y = lax.dot_general(p, v2, (((0,), (0,)), ((), ())), preferred_element_type=F32)
        out = jnp.where(row_seq == s, y, out)
    for h in range(XA_HEADS):
        o_ref[:, h * XA_HEADDIM:(h + 1) * XA_HEADDIM] = out[h * rows:(h + 1) * rows].astype(o_ref.dtype)


def _xattn_sample(layer, q, cache_k, cache_v, y_all):
    rows = XA_SAMPLE_ROWS
    blk0 = N_PROMPT // rows
    kv_spec = pl.BlockSpec((1, XA_SAMPLE_SEQS, N_MEM, XA_HEADS, XA_HEADDIM), lambda j: (layer, j, 0, 0, 0))
    return pl.pallas_call(
        _xattn_sample_kernel,
        grid=(DEC_BATCH // XA_SAMPLE_SEQS,),
        in_specs=[pl.BlockSpec((rows, XA_WIDTH), lambda j: (blk0 + j, 0)), kv_spec, kv_spec,
                  pl.BlockSpec(memory_space=pl.ANY)],
        out_specs=pl.BlockSpec((rows, XA_WIDTH), lambda j: (blk0 + j, 0)),
        out_shape=jax.ShapeDtypeStruct((N_TOK, XA_WIDTH), BF16),
        input_output_aliases={3: 0},
        compiler_params=_params(("parallel",)), name="xattn_sample",
    )(q, cache_k, cache_v, y_all)


MERGE_TILE = 256


def _merge_kernel(ygm, yssd, yxa, gates, x_p, x_s, pgm, pssd, pxa, wout, lng, lnb, o_f, o_b):
    g = gates[...].astype(F32)
    m = _dot(ygm[...], pgm[...]) * g[:, :D_MODEL]
    m = m + _dot(yssd[...], pssd[...]) * g[:, D_MODEL:2 * D_MODEL]
    m = m + _dot(yxa[...], pxa[...]) * g[:, 2 * D_MODEL:]
    h = _dot(m.astype(BF16), wout[...])
    x = jnp.where(pl.program_id(0) < N_PROMPT // MERGE_TILE, x_p[...], x_s[...])
    y = _layer_norm(ALPHA * x + h, lng[...], lnb[...])
    o_f[...] = y
    o_b[...] = y.astype(BF16)


def _merge(ygm, yssd, yxa, gates, x_p, x_s, x_s_row0, pgm, pssd, pxa, wout, lng, lnb):
    tm = MERGE_TILE
    npt = N_PROMPT // tm
    row = lambda i: (i, 0)
    acts = [ygm, yssd, yxa, gates]
    consts = [pgm, pssd, pxa, wout, lng, lnb]
    return pl.pallas_call(
        _merge_kernel,
        grid=(N_TOK // tm,),
        in_specs=[pl.BlockSpec((tm, a.shape[1]), row) for a in acts]
                 + [pl.BlockSpec((tm, D_MODEL), lambda i: (jnp.minimum(i, npt - 1), 0)),
                    pl.BlockSpec((tm, D_MODEL), lambda i: (x_s_row0 // tm + jnp.maximum(i - npt, 0), 0))]
                 + [_const_spec(a.shape) for a in consts],
        out_specs=[pl.BlockSpec((tm, D_MODEL), row), pl.BlockSpec((tm, D_MODEL), row)],
        out_shape=[jax.ShapeDtypeStruct((N_TOK, D_MODEL), F32), jax.ShapeDtypeStruct((N_TOK, D_MODEL), BF16)],
        compiler_params=_params(("parallel",)), name="merge_out_ln",
    )(*acts, x_p, x_s, *consts)


FFN_TILE = 512
FFN_SPLIT = 2


def _ffn_kernel(xb, xf, wg, wu, wd, lng, lnb, o_f, o_b, acc):
    k = pl.program_id(1)

    @pl.when(k == 0)
    def _():
        acc[...] = jnp.zeros_like(acc)

    h = (_silu(_dot(xb[...], wg[...])) * _dot(xb[...], wu[...])).astype(BF16)
    acc[...] += _dot(h, wd[...])

    @pl.when(k == pl.num_programs(1) - 1)
    def _():
        y = _layer_norm(ALPHA * xf[...] + acc[...], lng[...], lnb[...])
        o_f[...] = y
        o_b[...] = y.astype(BF16)


def _ffn(xb, xf, wg, wu, wd, lng, lnb):
    tm, tf = FFN_TILE, D_FF // FFN_SPLIT
    row = lambda i, k: (i, 0)
    return pl.pallas_call(
        _ffn_kernel,
        grid=(N_TOK // tm, FFN_SPLIT),
        in_specs=[pl.BlockSpec((tm, D_MODEL), row), pl.BlockSpec((tm, D_MODEL), row),
                  pl.BlockSpec((D_MODEL, tf), lambda i, k: (0, k)),
                  pl.BlockSpec((D_MODEL, tf), lambda i, k: (0, k)),
                  pl.BlockSpec((tf, D_MODEL), lambda i, k: (k, 0)),
                  _const_spec(lng.shape), _const_spec(lnb.shape)],
        out_specs=[pl.BlockSpec((tm, D_MODEL), row), pl.BlockSpec((tm, D_MODEL), row)],
        out_shape=[jax.ShapeDtypeStruct((N_TOK, D_MODEL), F32), jax.ShapeDtypeStruct((N_TOK, D_MODEL), BF16)],
        scratch_shapes=[pltpu.VMEM((tm, D_MODEL), F32)],
        compiler_params=_params(("parallel", "arbitrary")), name="ffn_swiglu_ln",
    )(xb, xf, wg, wu, wd, lng, lnb)


ROUTER_TILE = 1024


def _router_kernel(x_ref, w_ref, b_ref, l_ref, ri_ref, rg_ref, cnt_ref, count):
    @pl.when(pl.program_id(0) == 0)
    def _():
        count[...] = jnp.zeros_like(count)

    xs = _split3(x_ref[...])
    ws = _split3(w_ref[...])
    logits = b_ref[...] + _dot(xs[0], ws[0])
    for a, b in ((0, 1), (1, 0), (0, 2), (2, 0), (1, 1)):
        logits = logits + _dot(xs[a], ws[b])
    lane = lax.broadcasted_iota(jnp.int32, logits.shape, 1)
    logits = jnp.where(lane < N_EXPERTS, logits, -1e30)
    m1 = jnp.max(logits, axis=-1, keepdims=True)
    i1 = jnp.min(jnp.where(logits == m1, lane, LANES), axis=-1, keepdims=True)
    rest = jnp.where(lane == i1, -1e30, logits)
    m2 = jnp.max(rest, axis=-1, keepdims=True)
    i2 = jnp.min(jnp.where(rest == m2, lane, LANES), axis=-1, keepdims=True)
    e2 = jnp.exp(m2 - m1)
    den = 1.0 + e2

    hit1, hit2 = lane == i1, lane == i2
    assigned = jnp.where(hit1 | hit2, 1.0, 0.0)
    rank = _dot(l_ref[...], assigned.astype(BF16)) + count[...]
    r1 = jnp.sum(jnp.where(hit1, rank, 0.0), axis=-1, keepdims=True).astype(jnp.int32)
    r2 = jnp.sum(jnp.where(hit2, rank, 0.0), axis=-1, keepdims=True).astype(jnp.int32)
    ri_ref[...] = jnp.where(lane == 0, i1, jnp.where(lane == 1, i2, jnp.where(lane == 2, r1,
                            jnp.where(lane == 3, r2, 0))))
    rg_ref[...] = jnp.where(lane == 0, 1.0 / den, jnp.where(lane == 1, e2 / den, 0.0))
    count[...] = count[...] + jnp.sum(assigned, axis=0, keepdims=True)
    cnt_ref[...] = count[...]


def _router(x, w_pad, b_pad):
    tm = ROUTER_TILE
    strict_lower = jnp.tril(jnp.ones((tm, tm), F32), k=-1).astype(BF16)
    return pl.pallas_call(
        _router_kernel,
        grid=(N_TOK // tm,),
        in_specs=[pl.BlockSpec((tm, D_MODEL), lambda i: (i, 0)), _const_spec(w_pad.shape),
                  _const_spec(b_pad.shape), _const_spec(strict_lower.shape)],
        out_specs=[pl.BlockSpec((tm, LANES), lambda i: (i, 0)), pl.BlockSpec((tm, LANES), lambda i: (i, 0)),
                   _const_spec((1, LANES))],
        out_shape=[jax.ShapeDtypeStruct((N_TOK, LANES), jnp.int32), jax.ShapeDtypeStruct((N_TOK, LANES), F32),
                   jax.ShapeDtypeStruct((1, LANES), F32)],
        scratch_shapes=[pltpu.VMEM((1, LANES), F32)],
        compiler_params=_params(("arbitrary",)), name="moe_router",
    )(x, w_pad, b_pad, strict_lower)


EXPERT_TILE = 1024
N_SLOTS = N_TOK * 2 + N_EXPERTS * EXPERT_TILE
N_SLOT_TILES = N_SLOTS // EXPERT_TILE
MOE_TOK_TILE = 1024
MOE_FF_TILE = 512


def _row_copy(src, src_row, dst, dst_row, sem):
    return pltpu.make_async_copy(src.at[pl.ds(src_row, 1)], dst.at[pl.ds(dst_row, 1)], sem)


def _dispatch_kernel(s1_ref, s2_ref, x_ref, xs_in, xs_out, sem):
    del xs_in

    def body(r, carry):
        _row_copy(x_ref, r, xs_out, s1_ref[r], sem).start()
        _row_copy(x_ref, r, xs_out, s2_ref[r], sem).start()
        return carry

    lax.fori_loop(0, MOE_TOK_TILE, body, 0, unroll=8)
    for _ in range(2):
        pltpu.make_async_copy(x_ref, xs_out.at[pl.ds(0, MOE_TOK_TILE)], sem).wait()


def _dispatch(slot1, slot2, x, xs_zero):
    tm = MOE_TOK_TILE
    smem = lambda: pl.BlockSpec((tm,), lambda i: (i,), memory_space=pltpu.SMEM)
    return pl.pallas_call(
        _dispatch_kernel,
        grid=(N_TOK // tm,),
        in_specs=[smem(), smem(), pl.BlockSpec((tm, D_MODEL), lambda i: (i, 0)),
                  pl.BlockSpec(memory_space=pl.ANY)],
        out_specs=pl.BlockSpec(memory_space=pl.ANY),
        out_shape=jax.ShapeDtypeStruct((N_SLOTS, D_MODEL), F32),
        scratch_shapes=[pltpu.SemaphoreType.DMA(())],
        input_output_aliases={3: 0},
        compiler_params=_params(("arbitrary",)), name="moe_dispatch",
    )(slot1, slot2, x, xs_zero)


def _expert_ffn_kernel(te_ref, nv_ref, xs_ref, wg, wu, wd, ys_ref, xb_s):
    del te_ref
    t = pl.program_id(0)
    k = pl.program_id(1)

    @pl.when(t < nv_ref[0])
    def _():
        @pl.when(k == 0)
        def _():
            xb_s[...] = xs_ref[...].astype(BF16)
            ys_ref[...] = jnp.zeros_like(ys_ref)

        xb = xb_s[...]
        h = _silu(_dot(xb, wg[0, 0].astype(BF16))) * _dot(xb, wu[0, 0].astype(BF16))
        ys_ref[...] += _dot(h.astype(BF16), wd[0, 0].astype(BF16))

    @pl.when((t >= nv_ref[0]) & (k == 0))
    def _():
        ys_ref[...] = jnp.zeros_like(ys_ref)


def _expert_ffn(layer, tile_expert, n_valid, xs, wg, wu, wd):
    tm, tf = EXPERT_TILE, MOE_FF_TILE
    nk = E_FF // tf

    def tile(t, nv):
        return jnp.minimum(t, nv[0] - 1)

    def chunk(t, k, nv):
        return jnp.where(t < nv[0], k, nk - 1)

    grid_spec = pltpu.PrefetchScalarGridSpec(
        num_scalar_prefetch=2,
        grid=(N_SLOT_TILES, nk),
        in_specs=[pl.BlockSpec((tm, D_MODEL), lambda t, k, te, nv: (tile(t, nv), 0)),
                  pl.BlockSpec((1, 1, D_MODEL, tf), lambda t, k, te, nv: (layer, te[tile(t, nv)], 0, chunk(t, k, nv))),
                  pl.BlockSpec((1, 1, D_MODEL, tf), lambda t, k, te, nv: (layer, te[tile(t, nv)], 0, chunk(t, k, nv))),
                  pl.BlockSpec((1, 1, tf, D_MODEL), lambda t, k, te, nv: (layer, te[tile(t, nv)], chunk(t, k, nv), 0))],
        out_specs=pl.BlockSpec((tm, D_MODEL), lambda t, k, te, nv: (t, 0)),
        scratch_shapes=[pltpu.VMEM((tm, D_MODEL), BF16)],
    )
    return pl.pallas_call(
        _expert_ffn_kernel, grid_spec=grid_spec,
        out_shape=jax.ShapeDtypeStruct((N_SLOTS, D_MODEL), F32),
        compiler_params=_params(("arbitrary", "arbitrary")), name="moe_expert_ffn",
    )(tile_expert, n_valid, xs, wg, wu, wd)


def _combine_kernel(s1_ref, s2_ref, x_ref, rg_ref, ys_hbm, lng, lnb, o_p, o_s, buf1, buf2, sem):
    tm = MOE_TOK_TILE

    def body(r, carry):
        _row_copy(ys_hbm, s1_ref[r], buf1, r, sem).start()
        _row_copy(ys_hbm, s2_ref[r], buf2, r, sem).start()
        return carry

    lax.fori_loop(0, tm, body, 0, unroll=8)
    pltpu.make_async_copy(ys_hbm.at[pl.ds(0, tm)], buf1, sem).wait()
    pltpu.make_async_copy(ys_hbm.at[pl.ds(0, tm)], buf2, sem).wait()

    g = rg_ref[...]
    f = g[:, 0:1] * buf1[...] + g[:, 1:2] * buf2[...]
    y = _layer_norm(ALPHA * x_ref[...] + f, lng[...], lnb[...])
    i = pl.program_id(0)

    @pl.when(i < N_PROMPT // tm)
    def _():
        o_p[...] = y

    @pl.when(i >= N_PROMPT // tm)
    def _():
        o_s[...] = y


def _combine(slot1, slot2, x, rg, ys, lng, lnb):
    tm = MOE_TOK_TILE
    npt = N_PROMPT // tm
    smem = lambda: pl.BlockSpec((tm,), lambda i: (i,), memory_space=pltpu.SMEM)
    return pl.pallas_call(
        _combine_kernel,
        grid=(N_TOK // tm,),
        in_specs=[smem(), smem(), pl.BlockSpec((tm, D_MODEL), lambda i: (i, 0)),
                  pl.BlockSpec((tm, LANES), lambda i: (i, 0)), pl.BlockSpec(memory_space=pl.ANY),
                  _const_spec(lng.shape), _const_spec(lnb.shape)],
        out_specs=[pl.BlockSpec((tm, D_MODEL), lambda i: (jnp.minimum(i, npt - 1), 0)),
                   pl.BlockSpec((tm, D_MODEL), lambda i: (jnp.maximum(i - npt, 0), 0))],
        out_shape=[jax.ShapeDtypeStruct((N_PROMPT, D_MODEL), F32), jax.ShapeDtypeStruct((N_SAMPLE, D_MODEL), F32)],
        scratch_shapes=[pltpu.VMEM((tm, D_MODEL), F32), pltpu.VMEM((tm, D_MODEL), F32),
                        pltpu.SemaphoreType.DMA(())],
        compiler_params=_params(("arbitrary",)), name="moe_combine_ln",
    )(slot1, slot2, x, rg, ys, lng, lnb)


def _moe(layer, x, router_w, router_b, wg, wu, wd, lng, lnb):
    ri, rg, cnt = _router(x, _pad_lanes(router_w), _pad_lanes(_row(router_b)))
    counts = cnt[0, :N_EXPERTS].astype(jnp.int32)
    padded = (counts + EXPERT_TILE - 1) // EXPERT_TILE * EXPERT_TILE
    ends = jnp.cumsum(padded)
    starts = ends - padded
    slot1 = starts[ri[:, 0]] + ri[:, 2]
    slot2 = starts[ri[:, 1]] + ri[:, 3]
    n_valid = (ends[-1:] // EXPERT_TILE).astype(jnp.int32)
    tile_start = jnp.arange(N_SLOT_TILES, dtype=jnp.int32) * EXPERT_TILE
    tile_expert = jnp.minimum(jnp.sum(tile_start[:, None] >= ends[None, :], axis=1), N_EXPERTS - 1).astype(jnp.int32)

    xs = _dispatch(slot1, slot2, x, jnp.zeros((N_SLOTS, D_MODEL), F32))
    ys = _expert_ffn(layer, tile_expert, n_valid, xs, wg, wu, wd)
    return _combine(slot1, slot2, x, rg, ys, lng, lnb)


def _row(a):
    return a.reshape(1, -1).astype(F32)


def _pad_lanes(a):
    return jnp.pad(a, ((0, 0), (0, LANES - a.shape[1])))


def _gmlp_weights(w_s, b_s):
    tril = jnp.tril(jnp.ones((GM_CHUNK, GM_CHUNK), dtype=bool))
    w_p = jnp.where(tril, w_s, 0.0)
    n_seq = ROWS // DEC_SEQ
    w_8 = w_p[:, :DEC_SEQ, :DEC_SEQ]
    w_d = jnp.einsum("ab,gij->gaibj", jnp.eye(n_seq, dtype=F32), w_8).reshape(GM_GROUPS, ROWS, ROWS)
    bias_p = jnp.repeat(b_s.T, GM_GROUP, axis=1)
    bias_d = jnp.tile(bias_p[:DEC_SEQ], (n_seq, 1))
    return jnp.stack([w_p, w_d]).astype(BF16), jnp.stack([bias_p, bias_d])


def kernel(x_prompt, x_sample, mem_prompt, cache_mem_k, cache_mem_v, state_conv, state_ssm, w_in, conv_w, conv_b, dt_bias, a_log, d_skip, ssd_norm_g, v_ln_g, v_ln_b, w_s, b_s, p_gm, p_ssd, p_xa, w_out, w_mem_k, w_mem_v, ln1_g, ln1_b, ln2_g, ln2_b, ffn_wg, ffn_wu, ffn_wd, router_w, router_b, moe_wg, moe_wu, moe_wd):
    assert DEPTH % 2 == 0
    x_p = x_prompt.reshape(N_PROMPT, D_MODEL)
    x_s = x_sample.reshape(N_SAMPLE, D_MODEL)
    x_s_row0 = 0
    xb = jnp.concatenate([x_p.astype(BF16), x_s.astype(BF16)], axis=0)
    mem_b = mem_prompt.reshape(BATCH * N_MEM, D_MODEL).astype(BF16)
    w_in_t = jnp.swapaxes(w_in, 1, 2)
    w_tail_t = w_in_t[:, OFF_Q:]
    st_all = state_ssm.reshape(DEPTH, DEC_BATCH, SSD_INNER, SSD_STATE)

    mem_k_out, mem_v_out, ssm_p_out, conv_p_out, conv_s_out, v_out = [], [], [], [], [], []
    ssm_s = None
    for i in range(DEPTH):
        mk_f, mk_b = _mm(mem_b, w_mem_k, i, 0, XA_WIDTH, 1024, [], _epi_id2, [F32, BF16], "mem_k", tm=1024)
        mv_f, mv_b = _mm(mem_b, w_mem_v, i, 0, XA_WIDTH, 1024, [], _epi_id2, [F32, BF16], "mem_v", tm=1024)
        mem_k_out.append(mk_f.reshape(BATCH, N_MEM, XA_HEADS, XA_HEADDIM))
        mem_v_out.append(mv_f.reshape(BATCH, N_MEM, XA_HEADS, XA_HEADDIM))

        wt = dict(w_transposed=True)
        (u,) = _mm(xb, w_in_t, i, 0, GM_WIDTH, 1024, [], _epi_gelu, [BF16], "in_u", **wt)
        (v,) = _mm(xb, w_in_t, i, 1, GM_WIDTH, 1024, [_row(v_ln_g[i]), _row(v_ln_b[i])], _epi_gelu_ln, [F32], "in_v",
                   tm=1024, **wt)
        (zs,) = _mm(xb, w_in_t, i, OFF_Z // 1024, SSD_INNER, 1024, [], _epi_silu, [BF16], "in_z", **wt)
        (xbc,) = _mm(xb, w_in_t, i, OFF_XBC // 1024, CONV_DIM, 1024, [], _epi_id, [BF16], "in_xbc", **wt)
        (dt,) = _mm(xb, w_in_t, i, OFF_DT // LANES, LANES, LANES, [_pad_lanes(_row(dt_bias[i]))], _epi_softplus,
                    [F32], "in_dt", **wt)
        (q,) = _mm(xb, w_tail_t, i, 0, XA_WIDTH, 1024, [], _epi_qscale, [BF16], "in_q", **wt)
        (gates,) = _mm(xb, w_tail_t, i, 1, N_BRANCH * D_MODEL, 1024, [], _epi_sigmoid, [BF16], "in_gates", **wt)

        gm_w, gm_b = _gmlp_weights(w_s[i], b_s[i])
        y_gm = _gmlp(v, u, gm_w, gm_b)

        alog = _pad_lanes(_row(a_log[i]))
        dskip = _row(jnp.repeat(d_skip[i], SSD_HEADDIM))
        ssd_args = (alog, conv_w[i], _row(conv_b[i]), dskip, _row(ssd_norm_g[i]))
        y_ssd, ssm_p = _ssd_prompt(xbc, dt, zs, *ssd_args)
        prev_rows = jnp.pad(state_conv[i], ((0, 0), (DEC_SEQ - (SSD_CONV - 1), 0), (0, 0))).reshape(N_SAMPLE, CONV_DIM)
        y_ssd, ssm_s = _ssd_sample(i, xbc, prev_rows, dt, zs, st_all, y_ssd, ssm_s, *ssd_args)
        ssm_p_out.append(ssm_p.reshape(BATCH, SSD_HEADS, SSD_HEADDIM, SSD_STATE))
        keep = SSD_CONV - 1
        conv_p_out.append(xbc[:N_PROMPT].reshape(BATCH, SEQ, CONV_DIM)[:, SEQ - DEC_SEQ:][:, DEC_SEQ - keep:].astype(F32))
        conv_s_out.append(xbc[N_PROMPT:].reshape(DEC_BATCH, DEC_SEQ, CONV_DIM)[:, DEC_SEQ - keep:].astype(F32))
        v_out.append(v[N_PROMPT:].reshape(DEC_BATCH, DEC_SEQ, GM_WIDTH))

        y_xa = _xattn_prompt(q, mk_b.reshape(BATCH, N_MEM, XA_WIDTH), mv_b.reshape(BATCH, N_MEM, XA_WIDTH))
        y_xa = _xattn_sample(i, q, cache_mem_k, cache_mem_v, y_xa)

        x, xb = _merge(y_gm, y_ssd, y_xa, gates, x_p, x_s, x_s_row0, p_gm[i].astype(BF16), p_ssd[i].astype(BF16),
                       p_xa[i].astype(BF16), w_out[i].astype(BF16), _row(ln1_g[i]), _row(ln1_b[i]))

        j = i // 2
        if i % 2 == 0:
            x, xb = _ffn(xb, x, ffn_wg[j].astype(BF16), ffn_wu[j].astype(BF16), ffn_wd[j].astype(BF16),
                         _row(ln2_g[i]), _row(ln2_b[i]))
            x_p, x_s, x_s_row0 = x, x, N_PROMPT
        else:
            x_p, x_s = _moe(j, x, router_w[j], router_b[j], moe_wg, moe_wu, moe_wd, _row(ln2_g[i]), _row(ln2_b[i]))
            x_s_row0 = 0
            if i + 1 < DEPTH:
                xb = jnp.concatenate([x_p.astype(BF16), x_s.astype(BF16)], axis=0)

    y_prompt = x_p.reshape(BATCH, SEQ, D_MODEL)
    y_sample = x_s.reshape(DEC_BATCH, DEC_SEQ, D_MODEL)
    ssm_s_out = ssm_s.reshape(DEPTH, DEC_BATCH, SSD_HEADS, SSD_HEADDIM, SSD_STATE)
    return (y_prompt, y_sample, jnp.stack(mem_k_out), jnp.stack(mem_v_out), jnp.stack(conv_p_out),
            jnp.stack(ssm_p_out), jnp.stack(conv_s_out), ssm_s_out, jnp.stack(v_out))
```

```python
import functools

import jax
import jax.numpy as jnp
from jax import lax
from jax.experimental import pallas as pl
from jax.experimental.pallas import tpu as pltpu

F32 = jnp.float32
BF16 = jnp.bfloat16

D_MODEL = 1024
BATCH = 8
SEQ = 2048
DEPTH = 2
DEC_BATCH = 128
DEC_SEQ = 8
N_MEM = 256
GM_WIDTH = D_MODEL
GM_CHUNK = 128
GM_GROUP = 128
GM_GROUPS = GM_WIDTH // GM_GROUP
SSD_INNER = 2 * D_MODEL
SSD_HEADDIM = 64
SSD_HEADS = SSD_INNER // SSD_HEADDIM
SSD_STATE = 128
SSD_GROUPS = 4
SSD_HPG = SSD_HEADS // SSD_GROUPS
SSD_CONV = 4
SSD_CHUNK = 128
CONV_DIM = SSD_INNER + 2 * SSD_GROUPS * SSD_STATE
XA_HEADS = 4
XA_HEADDIM = D_MODEL // XA_HEADS
XA_WIDTH = XA_HEADS * XA_HEADDIM
N_BRANCH = 3
D_FF = ((8 * D_MODEL // 3 + 127) // 128) * 128
N_EXPERTS = 8
E_FF = 7 * D_MODEL // 2
ALPHA = (2 * DEPTH) ** 0.25
LN_EPS = 1e-5

N_PROMPT = BATCH * SEQ
N_SAMPLE = DEC_BATCH * DEC_SEQ
N_TOK = N_PROMPT + N_SAMPLE

OFF_Z = 2 * GM_WIDTH
OFF_XBC = OFF_Z + SSD_INNER
OFF_DT = OFF_XBC + CONV_DIM
OFF_Q = OFF_DT + SSD_HEADS
OFF_GATE = OFF_Q + XA_WIDTH

LANES = 128
ROWS = 128
GROUP_W = SSD_HPG * SSD_HEADDIM
VMEM_LIMIT = 56 * 1024 * 1024


def _params(sem):
    return pltpu.CompilerParams(dimension_semantics=sem, vmem_limit_bytes=VMEM_LIMIT)


def _dot(a, b):
    return jnp.dot(a, b, preferred_element_type=F32)


def _dot_nt(a, b):
    return lax.dot_general(a, b, (((1,), (1,)), ((), ())), preferred_element_type=F32)


def _split2(x):
    hi = x.astype(BF16)
    lo = (x - hi.astype(F32)).astype(BF16)
    return hi, lo


def _split3(x):
    hi = x.astype(BF16)
    r = x - hi.astype(F32)
    mid = r.astype(BF16)
    lo = (r - mid.astype(F32)).astype(BF16)
    return hi, mid, lo


def _dot_exact_lhs(m_bf16, x, pieces=3):
    parts = _split3(x) if pieces == 3 else _split2(x)
    n = x.shape[1]
    out = _dot(m_bf16, jnp.concatenate(parts, axis=1))
    return sum(out[:, i * n:(i + 1) * n] for i in range(1, pieces)) + out[:, :n]


def _expand_rhs2(xs, m_bf16):
    rows = xs[0].shape[0]
    out = _dot(jnp.concatenate([p for x in xs for p in _split2(x)], axis=0), m_bf16)
    return [out[2 * i * rows:(2 * i + 1) * rows] + out[(2 * i + 1) * rows:(2 * i + 2) * rows] for i in range(len(xs))]


def _layer_norm(r, g, b):
    mu = jnp.mean(r, axis=-1, keepdims=True)
    c = r - mu
    var = jnp.mean(c * c, axis=-1, keepdims=True)
    return c * lax.rsqrt(var + LN_EPS) * g + b


def _gelu_tanh(x):
    return x * (0.5 * (1.0 + jnp.tanh(0.7978845608028654 * (x + 0.044715 * (x * x * x)))))


def _sigmoid(x):
    return 1.0 / (1.0 + jnp.exp(-x))


def _silu(x):
    return x * _sigmoid(x)


def _softplus(x):
    return jnp.maximum(x, 0.0) + jnp.log(1.0 + jnp.exp(-jnp.abs(x)))


def _mm_kernel(x_ref, wt_ref, *rest, epilogue, n_extra, n_out):
    extras = [r[...] for r in rest[:n_extra]]
    outs = rest[n_extra:n_extra + n_out]
    w_bf16 = rest[n_extra + n_out]

    @pl.when(pl.program_id(1) == 0)
    def _():
        w_bf16[...] = wt_ref[0].T.astype(BF16)

    acc = _dot(x_ref[...], w_bf16[...])
    res = epilogue(acc, *extras)
    for o, r in zip(outs, res):
        o[...] = r.astype(o.dtype)


MM_TILE = 2176


def _mm(x, wt, layer, row_block0, n_cols, tn, extras, epilogue, out_dtypes, name, tm=MM_TILE):
    t, k = x.shape
    grid = (n_cols // tn, t // tm)
    in_specs = [pl.BlockSpec((tm, k), lambda j, i: (i, 0)),
                pl.BlockSpec((1, tn, k), lambda j, i: (layer, j + row_block0, 0))]
    in_specs += [pl.BlockSpec((1, tn), lambda j, i: (0, j)) for _ in extras]
    out_specs = [pl.BlockSpec((tm, tn), lambda j, i: (i, j)) for _ in out_dtypes]
    out_shape = [jax.ShapeDtypeStruct((t, n_cols), d) for d in out_dtypes]
    return pl.pallas_call(
        functools.partial(_mm_kernel, epilogue=epilogue, n_extra=len(extras), n_out=len(out_dtypes)),
        grid=grid, in_specs=in_specs, out_specs=out_specs, out_shape=out_shape,
        scratch_shapes=[pltpu.VMEM((k, tn), BF16)],
        compiler_params=_params(("parallel", "arbitrary")), name=name,
    )(x, wt, *extras)


MEM_TILE_SEQS = 4


def _mem_kv_kernel(m_ref, wk_ref, wv_ref, k5_ref, v5_ref, kb_ref, vb_ref):
    m = m_ref[...]
    for w_ref, o5_ref, ob_ref in ((wk_ref, k5_ref, kb_ref), (wv_ref, v5_ref, vb_ref)):
        acc = _dot(m, w_ref[0].astype(BF16))
        ob_ref[0] = acc.astype(BF16)
        for b in range(MEM_TILE_SEQS):
            for h in range(XA_HEADS):
                o5_ref[0, b, :, h, :] = acc[b * N_MEM:(b + 1) * N_MEM, h * XA_HEADDIM:(h + 1) * XA_HEADDIM]


def _mem_kv(mem_b, w_mem_k, w_mem_v):
    tm = MEM_TILE_SEQS * N_MEM
    n5 = (DEPTH, BATCH, N_MEM, XA_HEADS, XA_HEADDIM)
    w_spec = pl.BlockSpec((1, D_MODEL, XA_WIDTH), lambda l, r: (l, 0, 0))
    o5_spec = pl.BlockSpec((1, MEM_TILE_SEQS, N_MEM, XA_HEADS, XA_HEADDIM), lambda l, r: (l, r, 0, 0, 0))
    ob_spec = pl.BlockSpec((1, tm, XA_WIDTH), lambda l, r: (l, r, 0))
    return pl.pallas_call(
        _mem_kv_kernel,
        grid=(DEPTH, BATCH // MEM_TILE_SEQS),
        in_specs=[pl.BlockSpec((tm, D_MODEL), lambda l, r: (r, 0)), w_spec, w_spec],
        out_specs=[o5_spec, o5_spec, ob_spec, ob_spec],
        out_shape=[jax.ShapeDtypeStruct(n5, F32), jax.ShapeDtypeStruct(n5, F32),
                   jax.ShapeDtypeStruct((DEPTH, BATCH * N_MEM, XA_WIDTH), BF16),
                   jax.ShapeDtypeStruct((DEPTH, BATCH * N_MEM, XA_WIDTH), BF16)],
        compiler_params=_params(("parallel", "parallel")), name="mem_kv",
    )(mem_b, w_mem_k, w_mem_v)


def _epi_gelu(acc):
    return (_gelu_tanh(acc),)


def _epi_gelu_ln(acc, g, b):
    return (_layer_norm(_gelu_tanh(acc), g, b),)


def _epi_silu(acc):
    return (_silu(acc),)


def _epi_id(acc):
    return (acc,)


def _epi_softplus(acc, bias):
    return (_softplus(acc + bias),)


def _epi_qscale(acc):
    return (acc * (XA_HEADDIM ** -0.5),)


def _epi_sigmoid(acc):
    return (_sigmoid(acc),)


GM_TILE = 512


def _gmlp_kernel(v_ref, u_ref, w_ref, b_ref, o_ref):
    for c in range(GM_TILE // ROWS):
        rs = slice(c * ROWS, (c + 1) * ROWS)
        for g in range(GM_GROUPS):
            cs = slice(g * GM_GROUP, (g + 1) * GM_GROUP)
            z = _dot(w_ref[0, g], v_ref[rs, cs].astype(BF16)) + b_ref[0, :, cs]
            o_ref[rs, cs] = (u_ref[rs, cs].astype(F32) * z).astype(o_ref.dtype)


def _gmlp(v, u, w2, b2):
    n_prompt_tiles = N_PROMPT // GM_TILE

    def sel(i):
        return jnp.where(i >= n_prompt_tiles, 1, 0)

    return pl.pallas_call(
        _gmlp_kernel,
        grid=(N_TOK // GM_TILE,),
        in_specs=[pl.BlockSpec((GM_TILE, GM_WIDTH), lambda i: (i, 0)),
                  pl.BlockSpec((GM_TILE, GM_WIDTH), lambda i: (i, 0)),
                  pl.BlockSpec((1, GM_GROUPS, ROWS, ROWS), lambda i: (sel(i), 0, 0, 0)),
                  pl.BlockSpec((1, ROWS, GM_WIDTH), lambda i: (sel(i), 0, 0))],
        out_specs=pl.BlockSpec((GM_TILE, GM_WIDTH), lambda i: (i, 0)),
        out_shape=jax.ShapeDtypeStruct((N_TOK, GM_WIDTH), BF16),
        compiler_params=_params(("parallel",)), name="gmlp_spatial",
    )(v, u, w2, b2)


def _conv_silu(x_back, conv_w, conv_b):
    conv = conv_b + x_back(0) * conv_w[3:4, :]
    for s in range(1, SSD_CONV):
        conv = conv + x_back(s) * conv_w[3 - s:4 - s, :]
    return _silu(conv)


def _ssd_block(xc, dt, alog, lmat, bones, emat, dskip):
    xs = xc[:, :SSD_INNER].astype(F32)
    bm = xc[:, SSD_INNER:SSD_INNER + SSD_GROUPS * SSD_STATE].astype(BF16)
    cm = xc[:, SSD_INNER + SSD_GROUPS * SSD_STATE:].astype(BF16)

    lane = lax.broadcasted_iota(jnp.int32, (1, LANES), 1)
    a_neg = jnp.where(lane < SSD_HEADS, -jnp.exp(alog), 0.0)
    adt = dt * a_neg
    sums = _dot_exact_lhs(jnp.concatenate([lmat, bones], axis=0), adt)
    a_cs, a_tot = sums[:ROWS], sums[ROWS:]
    a_cs_t = a_cs.T
    mask = lmat.astype(F32) > 0.5

    dt_x, eacs_x, te_x = _expand_rhs2([dt, jnp.exp(a_cs), jnp.exp(a_tot - a_cs)], emat)
    xdt = xs * dt_x
    xw_t = (xdt * te_x).T.astype(BF16)

    lane_r = lax.broadcasted_iota(jnp.int32, (ROWS, LANES), 1)
    lo_half = lane_r < SSD_HEADDIM
    yd = []
    for g in range(SSD_GROUPS):
        ns = slice(g * SSD_STATE, (g + 1) * SSD_STATE)
        cb = _dot_nt(cm[:, ns], bm[:, ns])
        for hp in range(SSD_HPG // 2):
            h0 = g * SSD_HPG + 2 * hp
            ms = []
            for h in (h0, h0 + 1):
                seg = a_cs[:, h:h + 1] - a_cs_t[h:h + 1, :]
                ms.append(cb * jnp.exp(jnp.where(mask, seg, -1e30)))
            lhs = jnp.concatenate(ms, axis=1).astype(BF16)
            xp = xdt[:, h0 * SSD_HEADDIM:(h0 + 2) * SSD_HEADDIM]
            rhs = jnp.concatenate([jnp.where(lo_half, xp, 0.0), jnp.where(lo_half, 0.0, xp)],
                                  axis=0).astype(BF16)
            yd.append(_dot(lhs, rhs))
    y_pre = jnp.concatenate(yd, axis=1) + xs * dskip
    return dict(y_pre=y_pre, eacs_x=eacs_x, cm=cm, bm=bm, xw_t=xw_t, a_tot=a_tot)


def _ssd_finish(y, zs, norm_g):
    y = y * zs
    outs = []
    for g in range(SSD_GROUPS):
        yg = y[:, g * GROUP_W:(g + 1) * GROUP_W]
        ms = jnp.mean(yg * yg, axis=-1, keepdims=True)
        outs.append(yg * lax.rsqrt(ms + LN_EPS))
    return jnp.concatenate(outs, axis=1) * norm_g


def _ssd_prompt_kernel(*refs):
    y_ref = refs[12]

    @pl.when(pl.program_id(0) < BATCH)
    def _():
        _ssd_prompt_body(*refs)

    @pl.when(pl.program_id(0) == BATCH)
    def _():
        y_ref[...] = jnp.zeros_like(y_ref)


def _ssd_prompt_body(xbc_ref, dt_ref, zs_ref, alog_ref, cw_ref, cb_ref, l_ref, ones_ref, e_ref, et_ref, dskip_ref,
                     ng_ref, y_ref, st_ref, xpad, state):
    @pl.when(pl.program_id(1) == 0)
    def _():
        xpad[0:8, :] = jnp.zeros((8, CONV_DIM), F32)
        state[...] = jnp.zeros_like(state)

    xpad[8:8 + ROWS, :] = xbc_ref[...].astype(F32)
    xc = _conv_silu(lambda s: xpad[8 - s:8 - s + ROWS, :], cw_ref[...], cb_ref[...])
    xpad[0:8, :] = xpad[ROWS:ROWS + 8, :]
    r = _ssd_block(xc, dt_ref[...], alog_ref[...], l_ref[...], ones_ref[...], e_ref[...], dskip_ref[...])

    st = state[...]
    st_b = st.astype(BF16)
    y_off, s_new = [], []
    for g in range(SSD_GROUPS):
        ns = slice(g * SSD_STATE, (g + 1) * SSD_STATE)
        gs = slice(g * GROUP_W, (g + 1) * GROUP_W)
        y_off.append(_dot_nt(r["cm"][:, ns], st_b[gs, :]))
        s_new.append(_dot(r["xw_t"][gs, :], r["bm"][:, ns]))
    y = r["y_pre"] + jnp.concatenate(y_off, axis=1) * r["eacs_x"]
    y_ref[...] = _ssd_finish(y, zs_ref[...].astype(F32), ng_ref[...]).astype(y_ref.dtype)

    decay = _dot_exact_lhs(et_ref[...], jnp.exp(r["a_tot"].T), pieces=2)
    new_state = decay * st + jnp.concatenate(s_new, axis=0)
    state[...] = new_state
    st_ref[0] = new_state


def _ssd_consts(kind):
    tril = jnp.tril(jnp.ones((ROWS, ROWS), F32))
    if kind == "prompt":
        lmat, bones = tril, jnp.ones((ROWS, ROWS), F32)
    else:
        eye = jnp.eye(ROWS // DEC_SEQ, dtype=F32)
        blk = jnp.kron(eye, jnp.ones((DEC_SEQ, DEC_SEQ), F32))
        lmat, bones = tril * blk, blk
    head = jnp.arange(SSD_INNER) // SSD_HEADDIM
    emat = (jnp.arange(LANES)[:, None] == head[None, :]).astype(BF16)
    if kind == "prompt":
        return lmat.astype(BF16), bones.astype(BF16), emat, emat.T, None
    t = jnp.arange(ROWS)[:, None]
    col = jnp.arange(3 * ROWS)[None, :]
    shifts = []
    for s in range(1, SSD_CONV):
        inside = t % DEC_SEQ >= s
        earlier = (col % ROWS == t + DEC_SEQ - s) & (col >= ROWS)
        shifts.append(jnp.where(inside, col == t - s, earlier))
    shift = jnp.concatenate(shifts, axis=0).astype(BF16)
    return lmat.astype(BF16), bones.astype(BF16), emat, emat.T, shift


def _const_spec(shape):
    nd = len(shape)
    return pl.BlockSpec(shape, lambda *_: (0,) * nd)


def _ssd_prompt(xbc, dt, zs, alog, conv_w, conv_b, dskip, norm_g):
    lmat, bones, emat, emat_t, _ = _ssd_consts("prompt")
    nc = SEQ // ROWS
    n_blk = N_PROMPT // ROWS
    row = lambda b, c: (jnp.minimum(b * nc + c, n_blk - 1), 0)
    row_out = lambda b, c: (jnp.where(b < BATCH, b * nc + c, n_blk + jnp.minimum(c, N_SAMPLE // ROWS - 1)), 0)
    consts = [alog, conv_w, conv_b, lmat, bones, emat, emat_t, dskip, norm_g]
    return pl.pallas_call(
        _ssd_prompt_kernel,
        grid=(BATCH + 1, nc),
        in_specs=[pl.BlockSpec((ROWS, CONV_DIM), row), pl.BlockSpec((ROWS, LANES), row),
                  pl.BlockSpec((ROWS, SSD_INNER), row)] + [_const_spec(a.shape) for a in consts],
        out_specs=[pl.BlockSpec((ROWS, SSD_INNER), row_out),
                   pl.BlockSpec((1, SSD_INNER, SSD_STATE), lambda b, c: (jnp.minimum(b, BATCH - 1), 0, 0))],
        out_shape=[jax.ShapeDtypeStruct((N_TOK, SSD_INNER), BF16),
                   jax.ShapeDtypeStruct((BATCH, SSD_INNER, SSD_STATE), F32)],
        scratch_shapes=[pltpu.VMEM((ROWS + 8, CONV_DIM), F32), pltpu.VMEM((SSD_INNER, SSD_STATE), F32)],
        compiler_params=_params(("arbitrary", "arbitrary")), name="ssd_prompt",
    )(xbc, dt, zs, *consts)


SEQ_PER_STEP = 4
Q_ROWS = SEQ_PER_STEP * DEC_SEQ
N_QUARTER = ROWS // Q_ROWS


def _ssd_sample_kernel(*refs, n_alias):
    (xbc_ref, prev_ref, dt_ref, zs_ref, st_in, alog_ref, cw_ref, cb_ref, sh_ref, l_ref, ones_ref, e_ref, et_ref,
     dskip_ref, ng_ref) = refs[:15]
    (y_ref, st_out, ypre_s, eacs_s, cm_s, bm_s, xwt_s, eat_s) = refs[15 + n_alias:]
    q = pl.program_id(1)

    @pl.when(q == 0)
    def _():
        x_cur = xbc_ref[...]
        p_hi, p_lo = _split2(prev_ref[...])
        back = _dot(sh_ref[...], jnp.concatenate([x_cur, p_hi, p_lo], axis=0))
        xc = _conv_silu(lambda s: x_cur.astype(F32) if s == 0 else back[(s - 1) * ROWS:s * ROWS, :],
                        cw_ref[...], cb_ref[...])
        r = _ssd_block(xc, dt_ref[...], alog_ref[...], l_ref[...], ones_ref[...], e_ref[...], dskip_ref[...])
        ypre_s[...] = r["y_pre"]
        eacs_s[...] = r["eacs_x"]
        cm_s[...] = r["cm"].astype(F32)
        bm_s[...] = r["bm"].astype(F32)
        xwt_s[...] = r["xw_t"]
        eat_s[...] = jnp.exp(r["a_tot"].T)

    q0 = pl.multiple_of(q * Q_ROWS, Q_ROWS)
    cq = cm_s[pl.ds(q0, Q_ROWS), :].astype(BF16)
    row_q = lax.broadcasted_iota(jnp.int32, (Q_ROWS, 1), 0) // DEC_SEQ
    row_b = lax.broadcasted_iota(jnp.int32, (ROWS, 1), 0) // DEC_SEQ
    lane_b = lax.broadcasted_iota(jnp.int32, (1, LANES), 1) // DEC_SEQ
    e_atot_t = eat_s[...]
    et = et_ref[...]
    y_off = jnp.zeros((Q_ROWS, SSD_INNER), F32)
    for s in range(SEQ_PER_STEP):
        seq = q * SEQ_PER_STEP + s
        st = st_in[0, s]
        st_b = st.astype(BF16)
        bsel = row_b == seq
        yo, s_new = [], []
        for g in range(SSD_GROUPS):
            ns = slice(g * SSD_STATE, (g + 1) * SSD_STATE)
            gs = slice(g * GROUP_W, (g + 1) * GROUP_W)
            yo.append(_dot_nt(cq[:, ns], st_b[gs, :]))
            bm_g = jnp.where(bsel, bm_s[:, ns], 0.0).astype(BF16)
            s_new.append(_dot(xwt_s[gs, :], bm_g))
        y_off = y_off + jnp.where(row_q == s, jnp.concatenate(yo, axis=1), 0.0)
        dec_col = jnp.sum(jnp.where(lane_b == seq, e_atot_t, 0.0), axis=1, keepdims=True) * (1.0 / DEC_SEQ)
        decay = _dot_exact_lhs(et, jnp.broadcast_to(dec_col, (LANES, SSD_STATE)), pieces=2)
        new_state = decay * st + jnp.concatenate(s_new, axis=0)
        for d in range(st_out.shape[0]):
            st_out[d, s] = new_state

    y = ypre_s[pl.ds(q0, Q_ROWS), :] + y_off * eacs_s[pl.ds(q0, Q_ROWS), :]
    y_ref[...] = _ssd_finish(y, zs_ref[...].astype(F32), ng_ref[...]).astype(y_ref.dtype)


def _ssd_sample(layer, xbc, prev_rows, dt, zs, st_all, y_all, st_out_prev, alog, conv_w, conv_b, dskip, norm_g):
    lmat, bones, emat, emat_t, shift = _ssd_consts("sample")
    blk0 = N_PROMPT // ROWS
    qblk0 = N_PROMPT // Q_ROWS
    consts = [alog, conv_w, conv_b, shift, lmat, bones, emat, emat_t, dskip, norm_g]
    st_spec = pl.BlockSpec((1, SEQ_PER_STEP, SSD_INNER, SSD_STATE), lambda b, q: (layer, b * N_QUARTER + q, 0, 0))
    if st_out_prev is None:
        assert layer == 0
        st_out_spec = pl.BlockSpec((DEPTH, SEQ_PER_STEP, SSD_INNER, SSD_STATE),
                                   lambda b, q: (0, b * N_QUARTER + q, 0, 0))
    else:
        st_out_spec = st_spec
    aliased = [y_all] + ([] if st_out_prev is None else [st_out_prev])
    n_in = 5 + len(consts)
    aliases = {n_in: 0} if st_out_prev is None else {n_in: 0, n_in + 1: 1}
    return pl.pallas_call(
        functools.partial(_ssd_sample_kernel, n_alias=len(aliased)),
        grid=(N_SAMPLE // ROWS, N_QUARTER),
        in_specs=[pl.BlockSpec((ROWS, CONV_DIM), lambda b, q: (blk0 + b, 0)),
                  pl.BlockSpec((ROWS, CONV_DIM), lambda b, q: (b, 0)),
                  pl.BlockSpec((ROWS, LANES), lambda b, q: (blk0 + b, 0)),
                  pl.BlockSpec((Q_ROWS, SSD_INNER), lambda b, q: (qblk0 + b * N_QUARTER + q, 0)),
                  st_spec] + [_const_spec(a.shape) for a in consts]
                 + [pl.BlockSpec(memory_space=pl.ANY) for _ in aliased],
        out_specs=[pl.BlockSpec((Q_ROWS, SSD_INNER), lambda b, q: (qblk0 + b * N_QUARTER + q, 0)), st_out_spec],
        out_shape=[jax.ShapeDtypeStruct((N_TOK, SSD_INNER), BF16),
                   jax.ShapeDtypeStruct((DEPTH, DEC_BATCH, SSD_INNER, SSD_STATE), F32)],
        scratch_shapes=[pltpu.VMEM((ROWS, SSD_INNER), F32), pltpu.VMEM((ROWS, SSD_INNER), F32),
                        pltpu.VMEM((ROWS, SSD_GROUPS * SSD_STATE), F32),
                        pltpu.VMEM((ROWS, SSD_GROUPS * SSD_STATE), F32),
                        pltpu.VMEM((SSD_INNER, ROWS), BF16), pltpu.VMEM((LANES, ROWS), F32)],
        input_output_aliases=aliases,
        compiler_params=_params(("arbitrary", "arbitrary")), name="ssd_sample",
    )(xbc, prev_rows, dt, zs, st_all, *consts, *aliased)


def _xattn_prompt_kernel(q_ref, k_ref, v_ref, o_ref):
    @pl.when(pl.program_id(0) < BATCH)
    def _():
        q = q_ref[...]
        for h in range(XA_HEADS):
            hs = slice(h * XA_HEADDIM, (h + 1) * XA_HEADDIM)
            sc = _dot_nt(q[:, hs], k_ref[0, :, hs])
            p = jnp.exp(sc - jnp.max(sc, axis=-1, keepdims=True))
            p = (p / jnp.sum(p, axis=-1, keepdims=True)).astype(BF16)
            o_ref[:, hs] = _dot(p, v_ref[0, :, hs]).astype(o_ref.dtype)

    @pl.when(pl.program_id(0) == BATCH)
    def _():
        o_ref[...] = jnp.zeros_like(o_ref)


XA_Q_TILE = 512
XA_SAMPLE_SEQS = 4
XA_SAMPLE_ROWS = XA_SAMPLE_SEQS * DEC_SEQ


def _xattn_prompt(layer, q, mem_k, mem_v):
    nq = SEQ // XA_Q_TILE
    n_blk = N_PROMPT // XA_Q_TILE
    kv_spec = pl.BlockSpec((1, N_MEM, XA_WIDTH), lambda b, j: (layer * BATCH + jnp.minimum(b, BATCH - 1), 0, 0))
    row_out = lambda b, j: (jnp.where(b < BATCH, b * nq + j, n_blk + jnp.minimum(j, N_SAMPLE // XA_Q_TILE - 1)), 0)
    return pl.pallas_call(
        _xattn_prompt_kernel,
        grid=(BATCH + 1, nq),
        in_specs=[pl.BlockSpec((XA_Q_TILE, XA_WIDTH), lambda b, j: (jnp.minimum(b * nq + j, n_blk - 1), 0)),
                  kv_spec, kv_spec],
        out_specs=pl.BlockSpec((XA_Q_TILE, XA_WIDTH), row_out),
        out_shape=jax.ShapeDtypeStruct((N_TOK, XA_WIDTH), BF16),
        compiler_params=_params(("arbitrary", "arbitrary")), name="xattn_prompt",
    )(q, mem_k, mem_v)


def _xattn_sample_kernel(q_ref, k_ref, v_ref, y_all, o_ref):
    del y_all
    rows = XA_SAMPLE_ROWS
    q = q_ref[...]
    qblk = jnp.concatenate([q[:, h * XA_HEADDIM:(h + 1) * XA_HEADDIM] for h in range(XA_HEADS)], axis=0)
    mem_head = lax.broadcasted_iota(jnp.int32, (N_MEM * XA_HEADS, 1), 0) % XA_HEADS
    col_head = lax.broadcasted_iota(jnp.int32, (1, XA_HEADS * rows), 1) // rows
    same_head = mem_head == col_head
    row_seq = (lax.broadcasted_iota(jnp.int32, (XA_HEADS * rows, 1), 0) % rows) // DEC_SEQ
    out = jnp.zeros((XA_HEADS * rows, XA_HEADDIM), F32)
    for s in range(XA_SAMPLE_SEQS):
        k2 = k_ref[0, s].reshape(N_MEM * XA_HEADS, XA_HEADDIM).astype(BF16)
        v2 = v_ref[0, s].reshape(N_MEM * XA_HEADS, XA_HEADDIM).astype(BF16)
        sc = jnp.where(same_head, _dot_nt(k2, qblk), -1e30)
        p = jnp.exp(sc - jnp.max(sc, axis=0, keepdims=True))
        p = (p / jnp.sum(p, axis=0, keepdims=True)).astype(BF16)
        y = lax.dot_general(p, v2, (((0,), (0,)), ((), ())), preferred_element_type=F32)
        out = jnp.where(row_seq == s, y, out)
    for h in range(XA_HEADS):
        o_ref[:, h * XA_HEADDIM:(h + 1) * XA_HEADDIM] = out[h * rows:(h + 1) * rows].astype(o_ref.dtype)


def _xattn_sample(layer, q, cache_k, cache_v, y_all):
    rows = XA_SAMPLE_ROWS
    blk0 = N_PROMPT // rows
    kv_spec = pl.BlockSpec((1, XA_SAMPLE_SEQS, N_MEM, XA_HEADS, XA_HEADDIM), lambda j: (layer, j, 0, 0, 0))
    return pl.pallas_call(
        _xattn_sample_kernel,
        grid=(DEC_BATCH // XA_SAMPLE_SEQS,),
        in_specs=[pl.BlockSpec((rows, XA_WIDTH), lambda j: (blk0 + j, 0)), kv_spec, kv_spec,
                  pl.BlockSpec(memory_space=pl.ANY)],
        out_specs=pl.BlockSpec((rows, XA_WIDTH), lambda j: (blk0 + j, 0)),
        out_shape=jax.ShapeDtypeStruct((N_TOK, XA_WIDTH), BF16),
        input_output_aliases={3: 0},
        compiler_params=_params(("parallel",)), name="xattn_sample",
    )(q, cache_k, cache_v, y_all)


MERGE_TILE = 256


def _merge_kernel(ygm, yssd, yxa, gates, x_p, x_s, pgm, pssd, pxa, wout, lng, lnb, o_f, o_b):
    g = gates[...].astype(F32)
    m = _dot(ygm[...], pgm[...]) * g[:, :D_MODEL]
    m = m + _dot(yssd[...], pssd[...]) * g[:, D_MODEL:2 * D_MODEL]
    m = m + _dot(yxa[...], pxa[...]) * g[:, 2 * D_MODEL:]
    h = _dot(m.astype(BF16), wout[...])
    x = jnp.where(pl.program_id(0) < N_PROMPT // MERGE_TILE, x_p[...], x_s[...])
    y = _layer_norm(ALPHA * x + h, lng[...], lnb[...])
    o_f[...] = y
    o_b[...] = y.astype(BF16)


def _merge(ygm, yssd, yxa, gates, x_p, x_s, x_s_row0, pgm, pssd, pxa, wout, lng, lnb):
    tm = MERGE_TILE
    npt = N_PROMPT // tm
    row = lambda i: (i, 0)
    acts = [ygm, yssd, yxa, gates]
    consts = [pgm, pssd, pxa, wout, lng, lnb]
    return pl.pallas_call(
        _merge_kernel,
        grid=(N_TOK // tm,),
        in_specs=[pl.BlockSpec((tm, a.shape[1]), row) for a in acts]
                 + [pl.BlockSpec((tm, D_MODEL), lambda i: (jnp.minimum(i, npt - 1), 0)),
                    pl.BlockSpec((tm, D_MODEL), lambda i: (x_s_row0 // tm + jnp.maximum(i - npt, 0), 0))]
                 + [_const_spec(a.shape) for a in consts],
        out_specs=[pl.BlockSpec((tm, D_MODEL), row), pl.BlockSpec((tm, D_MODEL), row)],
        out_shape=[jax.ShapeDtypeStruct((N_TOK, D_MODEL), F32), jax.ShapeDtypeStruct((N_TOK, D_MODEL), BF16)],
        compiler_params=_params(("parallel",)), name="merge_out_ln",
    )(*acts, x_p, x_s, *consts)


FFN_TILE = 512
FFN_SPLIT = 2


def _ffn_kernel(xb, xf, wg, wu, wd, lng, lnb, o_f, o_b, acc):
    k = pl.program_id(1)

    @pl.when(k == 0)
    def _():
        acc[...] = jnp.zeros_like(acc)

    h = (_silu(_dot(xb[...], wg[...])) * _dot(xb[...], wu[...])).astype(BF16)
    acc[...] += _dot(h, wd[...])

    @pl.when(k == pl.num_programs(1) - 1)
    def _():
        y = _layer_norm(ALPHA * xf[...] + acc[...], lng[...], lnb[...])
        o_f[...] = y
        o_b[...] = y.astype(BF16)


def _ffn(xb, xf, wg, wu, wd, lng, lnb):
    tm, tf = FFN_TILE, D_FF // FFN_SPLIT
    row = lambda i, k: (i, 0)
    return pl.pallas_call(
        _ffn_kernel,
        grid=(N_TOK // tm, FFN_SPLIT),
        in_specs=[pl.BlockSpec((tm, D_MODEL), row), pl.BlockSpec((tm, D_MODEL), row),
                  pl.BlockSpec((D_MODEL, tf), lambda i, k: (0, k)),
                  pl.BlockSpec((D_MODEL, tf), lambda i, k: (0, k)),
                  pl.BlockSpec((tf, D_MODEL), lambda i, k: (k, 0)),
                  _const_spec(lng.shape), _const_spec(lnb.shape)],
        out_specs=[pl.BlockSpec((tm, D_MODEL), row), pl.BlockSpec((tm, D_MODEL), row)],
        out_shape=[jax.ShapeDtypeStruct((N_TOK, D_MODEL), F32), jax.ShapeDtypeStruct((N_TOK, D_MODEL), BF16)],
        scratch_shapes=[pltpu.VMEM((tm, D_MODEL), F32)],
        compiler_params=_params(("parallel", "arbitrary")), name="ffn_swiglu_ln",
    )(xb, xf, wg, wu, wd, lng, lnb)


ROUTER_TILE = 1024


def _router_kernel(x_ref, w_ref, b_ref, l_ref, ri_ref, rg_ref, cnt_ref, count):
    @pl.when(pl.program_id(0) == 0)
    def _():
        count[...] = jnp.zeros_like(count)

    xs = _split2(x_ref[...])
    ws = _split2(w_ref[...])
    logits = b_ref[...] + _dot(xs[0], ws[0]) + _dot(xs[0], ws[1]) + _dot(xs[1], ws[0])
    lane = lax.broadcasted_iota(jnp.int32, logits.shape, 1)
    logits = jnp.where(lane < N_EXPERTS, logits, -1e30)
    m1 = jnp.max(logits, axis=-1, keepdims=True)
    i1 = jnp.min(jnp.where(logits == m1, lane, LANES), axis=-1, keepdims=True)
    rest = jnp.where(lane == i1, -1e30, logits)
    m2 = jnp.max(rest, axis=-1, keepdims=True)
    i2 = jnp.min(jnp.where(rest == m2, lane, LANES), axis=-1, keepdims=True)
    e2 = jnp.exp(m2 - m1)
    den = 1.0 + e2

    hit1, hit2 = lane == i1, lane == i2
    assigned = jnp.where(hit1 | hit2, 1.0, 0.0)
    rank = _dot(l_ref[...], assigned.astype(BF16)) + count[...]
    r1 = jnp.sum(jnp.where(hit1, rank, 0.0), axis=-1, keepdims=True).astype(jnp.int32)
    r2 = jnp.sum(jnp.where(hit2, rank, 0.0), axis=-1, keepdims=True).astype(jnp.int32)
    ri_ref[...] = jnp.where(lane == 0, i1, jnp.where(lane == 1, i2, jnp.where(lane == 2, r1,
                            jnp.where(lane == 3, r2, 0))))
    rg_ref[...] = jnp.where(lane == 0, 1.0 / den, jnp.where(lane == 1, e2 / den, 0.0))
    count[...] = count[...] + jnp.sum(assigned, axis=0, keepdims=True)
    cnt_ref[...] = count[...]


def _router(x, w_pad, b_pad):
    tm = ROUTER_TILE
    strict_lower = jnp.tril(jnp.ones((tm, tm), F32), k=-1).astype(BF16)
    return pl.pallas_call(
        _router_kernel,
        grid=(N_TOK // tm,),
        in_specs=[pl.BlockSpec((tm, D_MODEL), lambda i: (i, 0)), _const_spec(w_pad.shape),
                  _const_spec(b_pad.shape), _const_spec(strict_lower.shape)],
        out_specs=[pl.BlockSpec((tm, LANES), lambda i: (i, 0)), pl.BlockSpec((tm, LANES), lambda i: (i, 0)),
                   _const_spec((1, LANES))],
        out_shape=[jax.ShapeDtypeStruct((N_TOK, LANES), jnp.int32), jax.ShapeDtypeStruct((N_TOK, LANES), F32),
                   jax.ShapeDtypeStruct((1, LANES), F32)],
        scratch_shapes=[pltpu.VMEM((1, LANES), F32)],
        compiler_params=_params(("arbitrary",)), name="moe_router",
    )(x, w_pad, b_pad, strict_lower)


EXPERT_TILE = 1024
N_SLOTS = N_TOK * 2 + N_EXPERTS * EXPERT_TILE
N_SLOT_TILES = N_SLOTS // EXPERT_TILE
MOE_TOK_TILE = 1024
MOE_FF_TILE = 512


def _row_copy(src, src_row, dst, dst_row, sem):
    return pltpu.make_async_copy(src.at[pl.ds(src_row, 1)], dst.at[pl.ds(dst_row, 1)], sem)


N_ZERO_TILES = 2 * N_EXPERTS


def _dispatch_kernel(zt_ref, zf_ref, s1_ref, s2_ref, x_ref, xs_out, zeros, sem, zsem):
    def zero_copy(k):
        dst = xs_out.at[pl.ds(pl.multiple_of(zt_ref[k] * EXPERT_TILE, EXPERT_TILE), EXPERT_TILE)]
        return pltpu.make_async_copy(zeros, dst, zsem)

    @pl.when(pl.program_id(0) == 0)
    def _():
        zeros[...] = jnp.zeros_like(zeros)
        for k in range(N_ZERO_TILES):
            @pl.when(zf_ref[k] == 1)
            def _():
                zero_copy(k).start()
        for k in range(N_ZERO_TILES):
            @pl.when(zf_ref[k] == 1)
            def _():
                zero_copy(k).wait()

    def body(r, carry):
        _row_copy(x_ref, r, xs_out, s1_ref[r], sem).start(priority=0)
        _row_copy(x_ref, r, xs_out, s2_ref[r], sem).start(priority=1)
        return carry

    lax.fori_loop(0, MOE_TOK_TILE, body, 0, unroll=8)
    for _ in range(2):
        pltpu.make_async_copy(x_ref, xs_out.at[pl.ds(0, MOE_TOK_TILE)], sem).wait()


def _dispatch(zero_tiles, zero_flags, slot1, slot2, x):
    tm = MOE_TOK_TILE
    smem = lambda: pl.BlockSpec((tm,), lambda i, zt, zf: (i,), memory_space=pltpu.SMEM)
    grid_spec = pltpu.PrefetchScalarGridSpec(
        num_scalar_prefetch=2,
        grid=(N_TOK // tm,),
        in_specs=[smem(), smem(), pl.BlockSpec((tm, D_MODEL), lambda i, zt, zf: (i, 0))],
        out_specs=pl.BlockSpec(memory_space=pl.ANY),
        scratch_shapes=[pltpu.VMEM((EXPERT_TILE, D_MODEL), F32), pltpu.SemaphoreType.DMA(()),
                        pltpu.SemaphoreType.DMA(())],
    )
    return pl.pallas_call(
        _dispatch_kernel, grid_spec=grid_spec,
        out_shape=jax.ShapeDtypeStruct((N_SLOTS, D_MODEL), F32),
        compiler_params=_params(("arbitrary",)), name="moe_dispatch",
    )(zero_tiles, zero_flags, slot1, slot2, x)


def _expert_ffn_kernel(te_ref, nv_ref, xs_ref, wg, wu, wd, ys_ref, xb_s):
    del te_ref
    t = pl.program_id(0)
    k = pl.program_id(1)

    @pl.when(t < nv_ref[0])
    def _():
        @pl.when(k == 0)
        def _():
            xb_s[...] = xs_ref[...].astype(BF16)
            ys_ref[...] = jnp.zeros_like(ys_ref)

        xb = xb_s[...]
        h = _silu(_dot(xb, wg[0, 0].astype(BF16))) * _dot(xb, wu[0, 0].astype(BF16))
        ys_ref[...] += _dot(h.astype(BF16), wd[0, 0].astype(BF16))

    @pl.when((t >= nv_ref[0]) & (k == 0))
    def _():
        ys_ref[...] = jnp.zeros_like(ys_ref)


def _expert_ffn(layer, tile_expert, n_valid, xs, wg, wu, wd):
    tm, tf = EXPERT_TILE, MOE_FF_TILE
    nk = E_FF // tf

    def tile(t, nv):
        return jnp.minimum(t, nv[0] - 1)

    def chunk(t, k, nv):
        return jnp.where(t < nv[0], k, nk - 1)

    grid_spec = pltpu.PrefetchScalarGridSpec(
        num_scalar_prefetch=2,
        grid=(N_SLOT_TILES, nk),
        in_specs=[pl.BlockSpec((tm, D_MODEL), lambda t, k, te, nv: (tile(t, nv), 0)),
                  pl.BlockSpec((1, 1, D_MODEL, tf), lambda t, k, te, nv: (layer, te[tile(t, nv)], 0, chunk(t, k, nv))),
                  pl.BlockSpec((1, 1, D_MODEL, tf), lambda t, k, te, nv: (layer, te[tile(t, nv)], 0, chunk(t, k, nv))),
                  pl.BlockSpec((1, 1, tf, D_MODEL), lambda t, k, te, nv: (layer, te[tile(t, nv)], chunk(t, k, nv), 0))],
        out_specs=pl.BlockSpec((tm, D_MODEL), lambda t, k, te, nv: (t, 0)),
        scratch_shapes=[pltpu.VMEM((tm, D_MODEL), BF16)],
    )
    return pl.pallas_call(
        _expert_ffn_kernel, grid_spec=grid_spec,
        out_shape=jax.ShapeDtypeStruct((N_SLOTS, D_MODEL), F32),
        compiler_params=_params(("arbitrary", "arbitrary")), name="moe_expert_ffn",
    )(tile_expert, n_valid, xs, wg, wu, wd)


def _combine_kernel(s1_ref, s2_ref, x_ref, rg_ref, ys_hbm, lng, lnb, o_p, o_s, buf1, buf2, sem):
    tm = MOE_TOK_TILE

    def body(r, carry):
        _row_copy(ys_hbm, s1_ref[r], buf1, r, sem).start(priority=0)
        _row_copy(ys_hbm, s2_ref[r], buf2, r, sem).start(priority=1)
        return carry

    lax.fori_loop(0, tm, body, 0, unroll=8)
    pltpu.make_async_copy(ys_hbm.at[pl.ds(0, tm)], buf1, sem).wait()
    pltpu.make_async_copy(ys_hbm.at[pl.ds(0, tm)], buf2, sem).wait()

    g = rg_ref[...]
    f = g[:, 0:1] * buf1[...] + g[:, 1:2] * buf2[...]
    y = _layer_norm(ALPHA * x_ref[...] + f, lng[...], lnb[...])
    i = pl.program_id(0)

    @pl.when(i < N_PROMPT // tm)
    def _():
        o_p[...] = y

    @pl.when(i >= N_PROMPT // tm)
    def _():
        o_s[...] = y


def _combine(slot1, slot2, x, rg, ys, lng, lnb):
    tm = MOE_TOK_TILE
    npt = N_PROMPT // tm
    smem = lambda: pl.BlockSpec((tm,), lambda i: (i,), memory_space=pltpu.SMEM)
    return pl.pallas_call(
        _combine_kernel,
        grid=(N_TOK // tm,),
        in_specs=[smem(), smem(), pl.BlockSpec((tm, D_MODEL), lambda i: (i, 0)),
                  pl.BlockSpec((tm, LANES), lambda i: (i, 0)), pl.BlockSpec(memory_space=pl.ANY),
                  _const_spec(lng.shape), _const_spec(lnb.shape)],
        out_specs=[pl.BlockSpec((tm, D_MODEL), lambda i: (jnp.minimum(i, npt - 1), 0)),
                   pl.BlockSpec((tm, D_MODEL), lambda i: (jnp.maximum(i - npt, 0), 0))],
        out_shape=[jax.ShapeDtypeStruct((N_PROMPT, D_MODEL), F32), jax.ShapeDtypeStruct((N_SAMPLE, D_MODEL), F32)],
        scratch_shapes=[pltpu.VMEM((tm, D_MODEL), F32), pltpu.VMEM((tm, D_MODEL), F32),
                        pltpu.SemaphoreType.DMA(())],
        compiler_params=_params(("arbitrary",)), name="moe_combine_ln",
    )(slot1, slot2, x, rg, ys, lng, lnb)


def _moe(layer, x, router_w, router_b, wg, wu, wd, lng, lnb):
    ri, rg, cnt = _router(x, _pad_lanes(router_w), _pad_lanes(_row(router_b)))
    counts = cnt[0, :N_EXPERTS].astype(jnp.int32)
    padded = (counts + EXPERT_TILE - 1) // EXPERT_TILE * EXPERT_TILE
    ends = jnp.cumsum(padded)
    starts = ends - padded
    slot1 = starts[ri[:, 0]] + ri[:, 2]
    slot2 = starts[ri[:, 1]] + ri[:, 3]
    n_valid = (ends[-1:] // EXPERT_TILE).astype(jnp.int32)
    tile_start = jnp.arange(N_SLOT_TILES, dtype=jnp.int32) * EXPERT_TILE
    tile_expert = jnp.minimum(jnp.sum(tile_start[:, None] >= ends[None, :], axis=1), N_EXPERTS - 1).astype(jnp.int32)

    tail = n_valid[0] + jnp.arange(N_EXPERTS, dtype=jnp.int32)
    zero_tiles = jnp.concatenate([(ends // EXPERT_TILE - 1).astype(jnp.int32), tail])
    zero_flags = jnp.concatenate([padded > 0, tail < N_SLOT_TILES]).astype(jnp.int32)
    zero_tiles = jnp.where(zero_flags == 1, zero_tiles, 0)

    xs = _dispatch(zero_tiles, zero_flags, slot1, slot2, x)
    ys = _expert_ffn(layer, tile_expert, n_valid, xs, wg, wu, wd)
    return _combine(slot1, slot2, x, rg, ys, lng, lnb)


def _row(a):
    return a.reshape(1, -1).astype(F32)


def _pad_lanes(a):
    return jnp.pad(a, ((0, 0), (0, LANES - a.shape[1])))


def _gmlp_weights(w_s, b_s):
    tril = jnp.tril(jnp.ones((GM_CHUNK, GM_CHUNK), dtype=bool))
    w_p = jnp.where(tril, w_s, 0.0)
    n_seq = ROWS // DEC_SEQ
    w_8 = w_p[:, :DEC_SEQ, :DEC_SEQ]
    w_d = jnp.einsum("ab,gij->gaibj", jnp.eye(n_seq, dtype=F32), w_8).reshape(GM_GROUPS, ROWS, ROWS)
    bias_p = jnp.repeat(b_s.T, GM_GROUP, axis=1)
    bias_d = jnp.tile(bias_p[:DEC_SEQ], (n_seq, 1))
    return jnp.stack([w_p, w_d]).astype(BF16), jnp.stack([bias_p, bias_d])


def kernel(x_prompt, x_sample, mem_prompt, cache_mem_k, cache_mem_v, state_conv, state_ssm, w_in, conv_w, conv_b, dt_bias, a_log, d_skip, ssd_norm_g, v_ln_g, v_ln_b, w_s, b_s, p_gm, p_ssd, p_xa, w_out, w_mem_k, w_mem_v, ln1_g, ln1_b, ln2_g, ln2_b, ffn_wg, ffn_wu, ffn_wd, router_w, router_b, moe_wg, moe_wu, moe_wd):
    assert DEPTH % 2 == 0
    x_p = x_prompt.reshape(N_PROMPT, D_MODEL)
    x_s = x_sample.reshape(N_SAMPLE, D_MODEL)
    x_s_row0 = 0
    xb = jnp.concatenate([x_p.astype(BF16), x_s.astype(BF16)], axis=0)
    mem_b = mem_prompt.reshape(BATCH * N_MEM, D_MODEL).astype(BF16)
    w_in_t = jnp.swapaxes(w_in, 1, 2)
    w_tail_t = w_in_t[:, OFF_Q:]
    st_all = state_ssm.reshape(DEPTH, DEC_BATCH, SSD_INNER, SSD_STATE)

    mem_k_out, mem_v_out, mem_kb, mem_vb = _mem_kv(mem_b, w_mem_k, w_mem_v)
    mem_kb = mem_kb.reshape(DEPTH * BATCH, N_MEM, XA_WIDTH)
    mem_vb = mem_vb.reshape(DEPTH * BATCH, N_MEM, XA_WIDTH)

    ssm_p_out, conv_p_out, conv_s_out, v_out = [], [], [], []
    ssm_s = None
    for i in range(DEPTH):
        (u,) = _mm(xb, w_in_t, i, 0, GM_WIDTH, 1024, [], _epi_gelu, [BF16], "in_u")
        (v,) = _mm(xb, w_in_t, i, 1, GM_WIDTH, 1024, [_row(v_ln_g[i]), _row(v_ln_b[i])], _epi_gelu_ln, [F32], "in_v",
                   tm=1024)
        (zs,) = _mm(xb, w_in_t, i, OFF_Z // 1024, SSD_INNER, 1024, [], _epi_silu, [BF16], "in_z")
        (xbc,) = _mm(xb, w_in_t, i, OFF_XBC // 1024, CONV_DIM, 1024, [], _epi_id, [BF16], "in_xbc")
        (dt,) = _mm(xb, w_in_t, i, OFF_DT // LANES, LANES, LANES, [_pad_lanes(_row(dt_bias[i]))], _epi_softplus,
                    [F32], "in_dt")
        (q,) = _mm(xb, w_tail_t, i, 0, XA_WIDTH, 1024, [], _epi_qscale, [BF16], "in_q")
        (gates,) = _mm(xb, w_tail_t, i, 1, N_BRANCH * D_MODEL, 1024, [], _epi_sigmoid, [BF16], "in_gates")

        gm_w, gm_b = _gmlp_weights(w_s[i], b_s[i])
        y_gm = _gmlp(v, u, gm_w, gm_b)

        alog = _pad_lanes(_row(a_log[i]))
        dskip = _row(jnp.repeat(d_skip[i], SSD_HEADDIM))
        ssd_args = (alog, conv_w[i], _row(conv_b[i]), dskip, _row(ssd_norm_g[i]))
        y_ssd, ssm_p = _ssd_prompt(xbc, dt, zs, *ssd_args)
        prev_rows = jnp.pad(state_conv[i], ((0, 0), (DEC_SEQ - (SSD_CONV - 1), 0), (0, 0))).reshape(N_SAMPLE, CONV_DIM)
        y_ssd, ssm_s = _ssd_sample(i, xbc, prev_rows, dt, zs, st_all, y_ssd, ssm_s, *ssd_args)
        ssm_p_out.append(ssm_p.reshape(BATCH, SSD_HEADS, SSD_HEADDIM, SSD_STATE))
        slots = xbc.reshape(N_TOK // DEC_SEQ, DEC_SEQ, CONV_DIM)
        n_p, per_seq, keep = N_PROMPT // DEC_SEQ, SEQ // DEC_SEQ, SSD_CONV - 1
        conv_p_out.append(lax.slice(slots, (per_seq - 1, DEC_SEQ - keep, 0), (n_p, DEC_SEQ, CONV_DIM),
                                    (per_seq, 1, 1)).astype(F32))
        conv_s_out.append(lax.slice(slots, (n_p, DEC_SEQ - keep, 0), slots.shape).astype(F32))
        v_out.append(v[N_PROMPT:].reshape(DEC_BATCH, DEC_SEQ, GM_WIDTH))

        y_xa = _xattn_prompt(i, q, mem_kb, mem_vb)
        y_xa = _xattn_sample(i, q, cache_mem_k, cache_mem_v, y_xa)

        x, xb = _merge(y_gm, y_ssd, y_xa, gates, x_p, x_s, x_s_row0, p_gm[i].astype(BF16), p_ssd[i].astype(BF16),
                       p_xa[i].astype(BF16), w_out[i].astype(BF16), _row(ln1_g[i]), _row(ln1_b[i]))

        j = i // 2
        if i % 2 == 0:
            x, xb = _ffn(xb, x, ffn_wg[j].astype(BF16), ffn_wu[j].astype(BF16), ffn_wd[j].astype(BF16),
                         _row(ln2_g[i]), _row(ln2_b[i]))
            x_p, x_s, x_s_row0 = x, x, N_PROMPT
        else:
            x_p, x_s = _moe(j, x, router_w[j], router_b[j], moe_wg, moe_wu, moe_wd, _row(ln2_g[i]), _row(ln2_b[i]))
            x_s_row0 = 0
            if i + 1 < DEPTH:
                xb = jnp.concatenate([x_p.astype(BF16), x_s.astype(BF16)], axis=0)

    y_prompt = x_p.reshape(BATCH, SEQ, D_MODEL)
    y_sample = x_s.reshape(DEC_BATCH, DEC_SEQ, D_MODEL)
    ssm_s_out = ssm_s.reshape(DEPTH, DEC_BATCH, SSD_HEADS, SSD_HEADDIM, SSD_STATE)
    return (y_prompt, y_sample, mem_k_out, mem_v_out, jnp.stack(conv_p_out),
            jnp.stack(ssm_p_out), jnp.stack(conv_s_out), ssm_s_out, jnp.stack(v_out))
```

```python
import functools

import jax
import jax.numpy as jnp
from jax import lax
from jax.experimental import pallas as pl
from jax.experimental.pallas import tpu as pltpu

F32 = jnp.float32
BF16 = jnp.bfloat16

D_MODEL = 1024
BATCH = 8
SEQ = 2048
DEPTH = 2
DEC_BATCH = 128
DEC_SEQ = 8
N_MEM = 256
GM_WIDTH = D_MODEL
GM_CHUNK = 128
GM_GROUP = 128
GM_GROUPS = GM_WIDTH // GM_GROUP
SSD_INNER = 2 * D_MODEL
SSD_HEADDIM = 64
SSD_HEADS = SSD_INNER // SSD_HEADDIM
SSD_STATE = 128
SSD_GROUPS = 4
SSD_HPG = SSD_HEADS // SSD_GROUPS
SSD_CONV = 4
SSD_CHUNK = 128
CONV_DIM = SSD_INNER + 2 * SSD_GROUPS * SSD_STATE
XA_HEADS = 4
XA_HEADDIM = D_MODEL // XA_HEADS
XA_WIDTH = XA_HEADS * XA_HEADDIM
N_BRANCH = 3
D_FF = ((8 * D_MODEL // 3 + 127) // 128) * 128
N_EXPERTS = 8
E_FF = 7 * D_MODEL // 2
ALPHA = (2 * DEPTH) ** 0.25
LN_EPS = 1e-5

N_PROMPT = BATCH * SEQ
N_SAMPLE = DEC_BATCH * DEC_SEQ
N_TOK = N_PROMPT + N_SAMPLE

OFF_Z = 2 * GM_WIDTH
OFF_XBC = OFF_Z + SSD_INNER
OFF_DT = OFF_XBC + CONV_DIM
OFF_Q = OFF_DT + SSD_HEADS
OFF_GATE = OFF_Q + XA_WIDTH

LANES = 128
ROWS = 128
GROUP_W = SSD_HPG * SSD_HEADDIM
VMEM_LIMIT = 56 * 1024 * 1024


def _params(sem):
    return pltpu.CompilerParams(dimension_semantics=sem, vmem_limit_bytes=VMEM_LIMIT)


def _dot(a, b):
    return jnp.dot(a, b, preferred_element_type=F32)


def _dot_nt(a, b):
    return lax.dot_general(a, b, (((1,), (1,)), ((), ())), preferred_element_type=F32)


def _split2(x):
    hi = x.astype(BF16)
    lo = (x - hi.astype(F32)).astype(BF16)
    return hi, lo


def _split3(x):
    hi = x.astype(BF16)
    r = x - hi.astype(F32)
    mid = r.astype(BF16)
    lo = (r - mid.astype(F32)).astype(BF16)
    return hi, mid, lo


def _dot_exact_lhs(m_bf16, x, pieces=3):
    parts = _split3(x) if pieces == 3 else _split2(x)
    n = x.shape[1]
    out = _dot(m_bf16, jnp.concatenate(parts, axis=1))
    return sum(out[:, i * n:(i + 1) * n] for i in range(1, pieces)) + out[:, :n]


def _expand_rhs2(xs, m_bf16):
    rows = xs[0].shape[0]
    out = _dot(jnp.concatenate([p for x in xs for p in _split2(x)], axis=0), m_bf16)
    return [out[2 * i * rows:(2 * i + 1) * rows] + out[(2 * i + 1) * rows:(2 * i + 2) * rows] for i in range(len(xs))]


def _layer_norm(r, g, b):
    mu = jnp.mean(r, axis=-1, keepdims=True)
    c = r - mu
    var = jnp.mean(c * c, axis=-1, keepdims=True)
    return c * lax.rsqrt(var + LN_EPS) * g + b


def _gelu_tanh(x):
    return x * (0.5 * (1.0 + jnp.tanh(0.7978845608028654 * (x + 0.044715 * (x * x * x)))))


def _sigmoid(x):
    return 1.0 / (1.0 + jnp.exp(-x))


def _silu(x):
    return x * _sigmoid(x)


def _softplus(x):
    return jnp.maximum(x, 0.0) + jnp.log(1.0 + jnp.exp(-jnp.abs(x)))


def _mm_kernel(x_ref, wt_ref, *rest, epilogue, n_extra, n_out):
    extras = [r[...] for r in rest[:n_extra]]
    outs = rest[n_extra:n_extra + n_out]
    w_bf16 = rest[n_extra + n_out]

    @pl.when(pl.program_id(1) == 0)
    def _():
        w_bf16[...] = wt_ref[0].T.astype(BF16)

    acc = _dot(x_ref[...], w_bf16[...])
    res = epilogue(acc, *extras)
    for o, r in zip(outs, res):
        o[...] = r.astype(o.dtype)


MM_TILE = 2176


def _mm(x, wt, layer, row0, n_cols, tn, extras, epilogue, out_dtypes, name, tm=MM_TILE):
    t, k = x.shape
    assert row0 % 8 == 0
    grid = (n_cols // tn, t // tm)
    in_specs = [pl.BlockSpec((tm, k), lambda j, i: (i, 0)),
                pl.BlockSpec((pl.Element(1), pl.Element(tn), pl.Element(k)),
                             lambda j, i: (layer, pl.multiple_of(row0 + j * tn, 8), 0))]
    in_specs += [pl.BlockSpec((1, tn), lambda j, i: (0, j)) for _ in extras]
    out_specs = [pl.BlockSpec((tm, tn), lambda j, i: (i, j)) for _ in out_dtypes]
    out_shape = [jax.ShapeDtypeStruct((t, n_cols), d) for d in out_dtypes]
    return pl.pallas_call(
        functools.partial(_mm_kernel, epilogue=epilogue, n_extra=len(extras), n_out=len(out_dtypes)),
        grid=grid, in_specs=in_specs, out_specs=out_specs, out_shape=out_shape,
        scratch_shapes=[pltpu.VMEM((k, tn), BF16)],
        compiler_params=_params(("parallel", "arbitrary")), name=name,
    )(x, wt, *extras)


MEM_TILE_SEQS = 4


def _mem_kv_kernel(m_ref, wk_ref, wv_ref, k5_ref, v5_ref, kb_ref, vb_ref):
    m = m_ref[...]
    for w_ref, o5_ref, ob_ref in ((wk_ref, k5_ref, kb_ref), (wv_ref, v5_ref, vb_ref)):
        acc = _dot(m, w_ref[0].astype(BF16))
        ob_ref[0] = acc.astype(BF16)
        for b in range(MEM_TILE_SEQS):
            for h in range(XA_HEADS):
                o5_ref[0, b, :, h, :] = acc[b * N_MEM:(b + 1) * N_MEM, h * XA_HEADDIM:(h + 1) * XA_HEADDIM]


def _mem_kv(mem_b, w_mem_k, w_mem_v):
    tm = MEM_TILE_SEQS * N_MEM
    n5 = (DEPTH, BATCH, N_MEM, XA_HEADS, XA_HEADDIM)
    w_spec = pl.BlockSpec((1, D_MODEL, XA_WIDTH), lambda l, r: (l, 0, 0))
    o5_spec = pl.BlockSpec((1, MEM_TILE_SEQS, N_MEM, XA_HEADS, XA_HEADDIM), lambda l, r: (l, r, 0, 0, 0))
    ob_spec = pl.BlockSpec((1, tm, XA_WIDTH), lambda l, r: (l, r, 0))
    return pl.pallas_call(
        _mem_kv_kernel,
        grid=(DEPTH, BATCH // MEM_TILE_SEQS),
        in_specs=[pl.BlockSpec((tm, D_MODEL), lambda l, r: (r, 0)), w_spec, w_spec],
        out_specs=[o5_spec, o5_spec, ob_spec, ob_spec],
        out_shape=[jax.ShapeDtypeStruct(n5, F32), jax.ShapeDtypeStruct(n5, F32),
                   jax.ShapeDtypeStruct((DEPTH, BATCH * N_MEM, XA_WIDTH), BF16),
                   jax.ShapeDtypeStruct((DEPTH, BATCH * N_MEM, XA_WIDTH), BF16)],
        compiler_params=_params(("parallel", "parallel")), name="mem_kv",
    )(mem_b, w_mem_k, w_mem_v)


def _epi_gelu(acc):
    return (_gelu_tanh(acc),)


def _epi_gelu_ln(acc, g, b):
    return (_layer_norm(_gelu_tanh(acc), g, b),)


def _epi_silu(acc):
    return (_silu(acc),)


def _epi_id(acc):
    return (acc,)


def _epi_softplus(acc, bias):
    return (_softplus(acc + bias),)


def _epi_qscale(acc):
    return (acc * (XA_HEADDIM ** -0.5),)


def _epi_sigmoid(acc):
    return (_sigmoid(acc),)


GM_TILE = 512


def _gmlp_kernel(v_ref, u_ref, w_ref, b_ref, o_ref):
    for c in range(GM_TILE // ROWS):
        rs = slice(c * ROWS, (c + 1) * ROWS)
        for g in range(GM_GROUPS):
            cs = slice(g * GM_GROUP, (g + 1) * GM_GROUP)
            z = _dot(w_ref[0, g], v_ref[rs, cs].astype(BF16)) + b_ref[0, :, cs]
            o_ref[rs, cs] = (u_ref[rs, cs].astype(F32) * z).astype(o_ref.dtype)


def _gmlp(v, u, w2, b2):
    n_prompt_tiles = N_PROMPT // GM_TILE

    def sel(i):
        return jnp.where(i >= n_prompt_tiles, 1, 0)

    return pl.pallas_call(
        _gmlp_kernel,
        grid=(N_TOK // GM_TILE,),
        in_specs=[pl.BlockSpec((GM_TILE, GM_WIDTH), lambda i: (i, 0)),
                  pl.BlockSpec((GM_TILE, GM_WIDTH), lambda i: (i, 0)),
                  pl.BlockSpec((1, GM_GROUPS, ROWS, ROWS), lambda i: (sel(i), 0, 0, 0)),
                  pl.BlockSpec((1, ROWS, GM_WIDTH), lambda i: (sel(i), 0, 0))],
        out_specs=pl.BlockSpec((GM_TILE, GM_WIDTH), lambda i: (i, 0)),
        out_shape=jax.ShapeDtypeStruct((N_TOK, GM_WIDTH), BF16),
        compiler_params=_params(("parallel",)), name="gmlp_spatial",
    )(v, u, w2, b2)


def _conv_silu(x_back, conv_w, conv_b):
    conv = conv_b + x_back(0) * conv_w[3:4, :]
    for s in range(1, SSD_CONV):
        conv = conv + x_back(s) * conv_w[3 - s:4 - s, :]
    return _silu(conv)


def _ssd_block(xc, dt, alog, lmat, bones, emat, dskip):
    xs = xc[:, :SSD_INNER].astype(F32)
    bm = xc[:, SSD_INNER:SSD_INNER + SSD_GROUPS * SSD_STATE].astype(BF16)
    cm = xc[:, SSD_INNER + SSD_GROUPS * SSD_STATE:].astype(BF16)

    lane = lax.broadcasted_iota(jnp.int32, (1, LANES), 1)
    a_neg = jnp.where(lane < SSD_HEADS, -jnp.exp(alog), 0.0)
    adt = dt * a_neg
    sums = _dot_exact_lhs(jnp.concatenate([lmat, bones], axis=0), adt)
    a_cs, a_tot = sums[:ROWS], sums[ROWS:]
    a_cs_t = a_cs.T
    mask = lmat.astype(F32) > 0.5

    dt_x, eacs_x, te_x = _expand_rhs2([dt, jnp.exp(a_cs), jnp.exp(a_tot - a_cs)], emat)
    xdt = xs * dt_x
    xw_t = (xdt * te_x).T.astype(BF16)

    lane_r = lax.broadcasted_iota(jnp.int32, (ROWS, LANES), 1)
    lo_half = lane_r < SSD_HEADDIM
    yd = []
    for g in range(SSD_GROUPS):
        ns = slice(g * SSD_STATE, (g + 1) * SSD_STATE)
        cb = _dot_nt(cm[:, ns], bm[:, ns])
        for hp in range(SSD_HPG // 2):
            h0 = g * SSD_HPG + 2 * hp
            ms = []
            for h in (h0, h0 + 1):
                seg = a_cs[:, h:h + 1] - a_cs_t[h:h + 1, :]
                ms.append(cb * jnp.exp(jnp.where(mask, seg, -1e30)))
            lhs = jnp.concatenate(ms, axis=1).astype(BF16)
            xp = xdt[:, h0 * SSD_HEADDIM:(h0 + 2) * SSD_HEADDIM]
            rhs = jnp.concatenate([jnp.where(lo_half, xp, 0.0), jnp.where(lo_half, 0.0, xp)],
                                  axis=0).astype(BF16)
            yd.append(_dot(lhs, rhs))
    y_pre = jnp.concatenate(yd, axis=1) + xs * dskip
    return dict(y_pre=y_pre, eacs_x=eacs_x, cm=cm, bm=bm, xw_t=xw_t, a_tot=a_tot)


def _ssd_finish(y, zs, norm_g):
    y = y * zs
    outs = []
    for g in range(SSD_GROUPS):
        yg = y[:, g * GROUP_W:(g + 1) * GROUP_W]
        ms = jnp.mean(yg * yg, axis=-1, keepdims=True)
        outs.append(yg * lax.rsqrt(ms + LN_EPS))
    return jnp.concatenate(outs, axis=1) * norm_g


def _ssd_prompt_kernel(*refs):
    y_ref = refs[12]

    @pl.when(pl.program_id(0) < BATCH)
    def _():
        _ssd_prompt_body(*refs)

    @pl.when(pl.program_id(0) == BATCH)
    def _():
        y_ref[...] = jnp.zeros_like(y_ref)


def _ssd_prompt_body(xbc_ref, dt_ref, zs_ref, alog_ref, cw_ref, cb_ref, l_ref, ones_ref, e_ref, et_ref, dskip_ref,
                     ng_ref, y_ref, st_ref, xpad, state):
    @pl.when(pl.program_id(1) == 0)
    def _():
        xpad[0:8, :] = jnp.zeros((8, CONV_DIM), F32)
        state[...] = jnp.zeros_like(state)

    xpad[8:8 + ROWS, :] = xbc_ref[...].astype(F32)
    xc = _conv_silu(lambda s: xpad[8 - s:8 - s + ROWS, :], cw_ref[...], cb_ref[...])
    xpad[0:8, :] = xpad[ROWS:ROWS + 8, :]
    r = _ssd_block(xc, dt_ref[...], alog_ref[...], l_ref[...], ones_ref[...], e_ref[...], dskip_ref[...])

    st = state[...]
    st_b = st.astype(BF16)
    y_off, s_new = [], []
    for g in range(SSD_GROUPS):
        ns = slice(g * SSD_STATE, (g + 1) * SSD_STATE)
        gs = slice(g * GROUP_W, (g + 1) * GROUP_W)
        y_off.append(_dot_nt(r["cm"][:, ns], st_b[gs, :]))
        s_new.append(_dot(r["xw_t"][gs, :], r["bm"][:, ns]))
    y = r["y_pre"] + jnp.concatenate(y_off, axis=1) * r["eacs_x"]
    y_ref[...] = _ssd_finish(y, zs_ref[...].astype(F32), ng_ref[...]).astype(y_ref.dtype)

    decay = _dot_exact_lhs(et_ref[...], jnp.exp(r["a_tot"].T), pieces=2)
    new_state = decay * st + jnp.concatenate(s_new, axis=0)
    state[...] = new_state
    st_ref[0] = new_state


def _ssd_consts(kind):
    tril = jnp.tril(jnp.ones((ROWS, ROWS), F32))
    if kind == "prompt":
        lmat, bones = tril, jnp.ones((ROWS, ROWS), F32)
    else:
        eye = jnp.eye(ROWS // DEC_SEQ, dtype=F32)
        blk = jnp.kron(eye, jnp.ones((DEC_SEQ, DEC_SEQ), F32))
        lmat, bones = tril * blk, blk
    head = jnp.arange(SSD_INNER) // SSD_HEADDIM
    emat = (jnp.arange(LANES)[:, None] == head[None, :]).astype(BF16)
    if kind == "prompt":
        return lmat.astype(BF16), bones.astype(BF16), emat, emat.T, None
    t = jnp.arange(ROWS)[:, None]
    col = jnp.arange(3 * ROWS)[None, :]
    shifts = []
    for s in range(1, SSD_CONV):
        inside = t % DEC_SEQ >= s
        earlier = (col % ROWS == t + DEC_SEQ - s) & (col >= ROWS)
        shifts.append(jnp.where(inside, col == t - s, earlier))
    shift = jnp.concatenate(shifts, axis=0).astype(BF16)
    return lmat.astype(BF16), bones.astype(BF16), emat, emat.T, shift


def _const_spec(shape):
    nd = len(shape)
    return pl.BlockSpec(shape, lambda *_: (0,) * nd)


def _ssd_prompt(xbc, dt, zs, alog, conv_w, conv_b, dskip, norm_g):
    lmat, bones, emat, emat_t, _ = _ssd_consts("prompt")
    nc = SEQ // ROWS
    n_blk = N_PROMPT // ROWS
    row = lambda b, c: (jnp.minimum(b * nc + c, n_blk - 1), 0)
    row_out = lambda b, c: (jnp.where(b < BATCH, b * nc + c, n_blk + jnp.minimum(c, N_SAMPLE // ROWS - 1)), 0)
    consts = [alog, conv_w, conv_b, lmat, bones, emat, emat_t, dskip, norm_g]
    return pl.pallas_call(
        _ssd_prompt_kernel,
        grid=(BATCH + 1, nc),
        in_specs=[pl.BlockSpec((ROWS, CONV_DIM), row), pl.BlockSpec((ROWS, LANES), row),
                  pl.BlockSpec((ROWS, SSD_INNER), row)] + [_const_spec(a.shape) for a in consts],
        out_specs=[pl.BlockSpec((ROWS, SSD_INNER), row_out),
                   pl.BlockSpec((1, SSD_INNER, SSD_STATE), lambda b, c: (jnp.minimum(b, BATCH - 1), 0, 0))],
        out_shape=[jax.ShapeDtypeStruct((N_TOK, SSD_INNER), BF16),
                   jax.ShapeDtypeStruct((BATCH, SSD_INNER, SSD_STATE), F32)],
        scratch_shapes=[pltpu.VMEM((ROWS + 8, CONV_DIM), F32), pltpu.VMEM((SSD_INNER, SSD_STATE), F32)],
        compiler_params=_params(("arbitrary", "arbitrary")), name="ssd_prompt",
    )(xbc, dt, zs, *consts)


SEQ_PER_STEP = 4
Q_ROWS = SEQ_PER_STEP * DEC_SEQ
N_QUARTER = ROWS // Q_ROWS


def _ssd_sample_kernel(*refs, n_alias):
    (xbc_ref, prev_ref, dt_ref, zs_ref, st_in, alog_ref, cw_ref, cb_ref, sh_ref, l_ref, ones_ref, e_ref, et_ref,
     dskip_ref, ng_ref) = refs[:15]
    (y_ref, st_out, ypre_s, eacs_s, cm_s, bm_s, xwt_s, eat_s) = refs[15 + n_alias:]
    q = pl.program_id(1)

    @pl.when(q == 0)
    def _():
        x_cur = xbc_ref[...]
        p_hi, p_lo = _split2(prev_ref[...])
        back = _dot(sh_ref[...], jnp.concatenate([x_cur, p_hi, p_lo], axis=0))
        xc = _conv_silu(lambda s: x_cur.astype(F32) if s == 0 else back[(s - 1) * ROWS:s * ROWS, :],
                        cw_ref[...], cb_ref[...])
        r = _ssd_block(xc, dt_ref[...], alog_ref[...], l_ref[...], ones_ref[...], e_ref[...], dskip_ref[...])
        ypre_s[...] = r["y_pre"]
        eacs_s[...] = r["eacs_x"]
        cm_s[...] = r["cm"].astype(F32)
        bm_s[...] = r["bm"].astype(F32)
        xwt_s[...] = r["xw_t"]
        eat_s[...] = jnp.exp(r["a_tot"].T)

    q0 = pl.multiple_of(q * Q_ROWS, Q_ROWS)
    cq = cm_s[pl.ds(q0, Q_ROWS), :].astype(BF16)
    row_q = lax.broadcasted_iota(jnp.int32, (Q_ROWS, 1), 0) // DEC_SEQ
    row_b = lax.broadcasted_iota(jnp.int32, (ROWS, 1), 0) // DEC_SEQ
    lane_b = lax.broadcasted_iota(jnp.int32, (1, LANES), 1) // DEC_SEQ
    e_atot_t = eat_s[...]
    et = et_ref[...]
    y_off = jnp.zeros((Q_ROWS, SSD_INNER), F32)
    for s in range(SEQ_PER_STEP):
        seq = q * SEQ_PER_STEP + s
        st = st_in[0, s]
        st_b = st.astype(BF16)
        bsel = row_b == seq
        yo, s_new = [], []
        for g in range(SSD_GROUPS):
            ns = slice(g * SSD_STATE, (g + 1) * SSD_STATE)
            gs = slice(g * GROUP_W, (g + 1) * GROUP_W)
            yo.append(_dot_nt(cq[:, ns], st_b[gs, :]))
            bm_g = jnp.where(bsel, bm_s[:, ns], 0.0).astype(BF16)
            s_new.append(_dot(xwt_s[gs, :], bm_g))
        y_off = y_off + jnp.where(row_q == s, jnp.concatenate(yo, axis=1), 0.0)
        dec_col = jnp.sum(jnp.where(lane_b == seq, e_atot_t, 0.0), axis=1, keepdims=True) * (1.0 / DEC_SEQ)
        decay = _dot_exact_lhs(et, jnp.broadcast_to(dec_col, (LANES, SSD_STATE)), pieces=2)
        new_state = decay * st + jnp.concatenate(s_new, axis=0)
        for d in range(st_out.shape[0]):
            st_out[d, s] = new_state

    y = ypre_s[pl.ds(q0, Q_ROWS), :] + y_off * eacs_s[pl.ds(q0, Q_ROWS), :]
    y_ref[...] = _ssd_finish(y, zs_ref[...].astype(F32), ng_ref[...]).astype(y_ref.dtype)


def _ssd_sample(layer, xbc, prev_rows, dt, zs, st_all, y_all, st_out_prev, alog, conv_w, conv_b, dskip, norm_g):
    lmat, bones, emat, emat_t, shift = _ssd_consts("sample")
    blk0 = N_PROMPT // ROWS
    qblk0 = N_PROMPT // Q_ROWS
    consts = [alog, conv_w, conv_b, shift, lmat, bones, emat, emat_t, dskip, norm_g]
    st_spec = pl.BlockSpec((1, SEQ_PER_STEP, SSD_INNER, SSD_STATE), lambda b, q: (layer, b * N_QUARTER + q, 0, 0))
    if st_out_prev is None:
        assert layer == 0
        st_out_spec = pl.BlockSpec((DEPTH, SEQ_PER_STEP, SSD_INNER, SSD_STATE),
                                   lambda b, q: (0, b * N_QUARTER + q, 0, 0))
    else:
        st_out_spec = st_spec
    aliased = [y_all] + ([] if st_out_prev is None else [st_out_prev])
    n_in = 5 + len(consts)
    aliases = {n_in: 0} if st_out_prev is None else {n_in: 0, n_in + 1: 1}
    return pl.pallas_call(
        functools.partial(_ssd_sample_kernel, n_alias=len(aliased)),
        grid=(N_SAMPLE // ROWS, N_QUARTER),
        in_specs=[pl.BlockSpec((ROWS, CONV_DIM), lambda b, q: (blk0 + b, 0)),
                  pl.BlockSpec((ROWS, CONV_DIM), lambda b, q: (b, 0)),
                  pl.BlockSpec((ROWS, LANES), lambda b, q: (blk0 + b, 0)),
                  pl.BlockSpec((Q_ROWS, SSD_INNER), lambda b, q: (qblk0 + b * N_QUARTER + q, 0)),
                  st_spec] + [_const_spec(a.shape) for a in consts]
                 + [pl.BlockSpec(memory_space=pl.ANY) for _ in aliased],
        out_specs=[pl.BlockSpec((Q_ROWS, SSD_INNER), lambda b, q: (qblk0 + b * N_QUARTER + q, 0)), st_out_spec],
        out_shape=[jax.ShapeDtypeStruct((N_TOK, SSD_INNER), BF16),
                   jax.ShapeDtypeStruct((DEPTH, DEC_BATCH, SSD_INNER, SSD_STATE), F32)],
        scratch_shapes=[pltpu.VMEM((ROWS, SSD_INNER), F32), pltpu.VMEM((ROWS, SSD_INNER), F32),
                        pltpu.VMEM((ROWS, SSD_GROUPS * SSD_STATE), F32),
                        pltpu.VMEM((ROWS, SSD_GROUPS * SSD_STATE), F32),
                        pltpu.VMEM((SSD_INNER, ROWS), BF16), pltpu.VMEM((LANES, ROWS), F32)],
        input_output_aliases=aliases,
        compiler_params=_params(("arbitrary", "arbitrary")), name="ssd_sample",
    )(xbc, prev_rows, dt, zs, st_all, *consts, *aliased)


def _xattn_prompt_kernel(q_ref, k_ref, v_ref, o_ref):
    @pl.when(pl.program_id(0) < BATCH)
    def _():
        q = q_ref[...]
        for h in range(XA_HEADS):
            hs = slice(h * XA_HEADDIM, (h + 1) * XA_HEADDIM)
            sc = _dot_nt(q[:, hs], k_ref[0, :, hs])
            p = jnp.exp(sc - jnp.max(sc, axis=-1, keepdims=True))
            p = (p / jnp.sum(p, axis=-1, keepdims=True)).astype(BF16)
            o_ref[:, hs] = _dot(p, v_ref[0, :, hs]).astype(o_ref.dtype)

    @pl.when(pl.program_id(0) == BATCH)
    def _():
        o_ref[...] = jnp.zeros_like(o_ref)


XA_Q_TILE = 512
XA_SAMPLE_SEQS = 4
XA_SAMPLE_ROWS = XA_SAMPLE_SEQS * DEC_SEQ


def _xattn_prompt(layer, q, mem_k, mem_v):
    nq = SEQ // XA_Q_TILE
    n_blk = N_PROMPT // XA_Q_TILE
    kv_spec = pl.BlockSpec((1, N_MEM, XA_WIDTH), lambda b, j: (layer * BATCH + jnp.minimum(b, BATCH - 1), 0, 0))
    row_out = lambda b, j: (jnp.where(b < BATCH, b * nq + j, n_blk + jnp.minimum(j, N_SAMPLE // XA_Q_TILE - 1)), 0)
    return pl.pallas_call(
        _xattn_prompt_kernel,
        grid=(BATCH + 1, nq),
        in_specs=[pl.BlockSpec((XA_Q_TILE, XA_WIDTH), lambda b, j: (jnp.minimum(b * nq + j, n_blk - 1), 0)),
                  kv_spec, kv_spec],
        out_specs=pl.BlockSpec((XA_Q_TILE, XA_WIDTH), row_out),
        out_shape=jax.ShapeDtypeStruct((N_TOK, XA_WIDTH), BF16),
        compiler_params=_params(("arbitrary", "arbitrary")), name="xattn_prompt",
    )(q, mem_k, mem_v)


def _xattn_sample_kernel(q_ref, k_ref, v_ref, y_all, o_ref):
    del y_all
    rows = XA_SAMPLE_ROWS
    q = q_ref[...]
    qblk = jnp.concatenate([q[:, h * XA_HEADDIM:(h + 1) * XA_HEADDIM] for h in range(XA_HEADS)], axis=0)
    mem_head = lax.broadcasted_iota(jnp.int32, (N_MEM * XA_HEADS, 1), 0) % XA_HEADS
    col_head = lax.broadcasted_iota(jnp.int32, (1, XA_HEADS * rows), 1) // rows
    same_head = mem_head == col_head
    row_seq = (lax.broadcasted_iota(jnp.int32, (XA_HEADS * rows, 1), 0) % rows) // DEC_SEQ
    out = jnp.zeros((XA_HEADS * rows, XA_HEADDIM), F32)
    for s in range(XA_SAMPLE_SEQS):
        k2 = k_ref[0, s].reshape(N_MEM * XA_HEADS, XA_HEADDIM).astype(BF16)
        v2 = v_ref[0, s].reshape(N_MEM * XA_HEADS, XA_HEADDIM).astype(BF16)
        sc = jnp.where(same_head, _dot_nt(k2, qblk), -1e30)
        p = jnp.exp(sc - jnp.max(sc, axis=0, keepdims=True))
        p = (p / jnp.sum(p, axis=0, keepdims=True)).astype(BF16)
        y = lax.dot_general(p, v2, (((0,), (0,)), ((), ())), preferred_element_type=F32)
        out = jnp.where(row_seq == s, y, out)
    for h in range(XA_HEADS):
        o_ref[:, h * XA_HEADDIM:(h + 1) * XA_HEADDIM] = out[h * rows:(h + 1) * rows].astype(o_ref.dtype)


def _xattn_sample(layer, q, cache_k, cache_v, y_all):
    rows = XA_SAMPLE_ROWS
    blk0 = N_PROMPT // rows
    kv_spec = pl.BlockSpec((1, XA_SAMPLE_SEQS, N_MEM, XA_HEADS, XA_HEADDIM), lambda j: (layer, j, 0, 0, 0))
    return pl.pallas_call(
        _xattn_sample_kernel,
        grid=(DEC_BATCH // XA_SAMPLE_SEQS,),
        in_specs=[pl.BlockSpec((rows, XA_WIDTH), lambda j: (blk0 + j, 0)), kv_spec, kv_spec,
                  pl.BlockSpec(memory_space=pl.ANY)],
        out_specs=pl.BlockSpec((rows, XA_WIDTH), lambda j: (blk0 + j, 0)),
        out_shape=jax.ShapeDtypeStruct((N_TOK, XA_WIDTH), BF16),
        input_output_aliases={3: 0},
        compiler_params=_params(("parallel",)), name="xattn_sample",
    )(q, cache_k, cache_v, y_all)


MERGE_TILE = 256


def _merge_kernel(ygm, yssd, yxa, gates, x_p, x_s, pgm, pssd, pxa, wout, lng, lnb, o_f, o_b):
    g = gates[...].astype(F32)
    m = _dot(ygm[...], pgm[...]) * g[:, :D_MODEL]
    m = m + _dot(yssd[...], pssd[...]) * g[:, D_MODEL:2 * D_MODEL]
    m = m + _dot(yxa[...], pxa[...]) * g[:, 2 * D_MODEL:]
    h = _dot(m.astype(BF16), wout[...])
    x = jnp.where(pl.program_id(0) < N_PROMPT // MERGE_TILE, x_p[...], x_s[...])
    y = _layer_norm(ALPHA * x + h, lng[...], lnb[...])
    o_f[...] = y
    o_b[...] = y.astype(BF16)


def _merge(ygm, yssd, yxa, gates, x_p, x_s, x_s_row0, pgm, pssd, pxa, wout, lng, lnb):
    tm = MERGE_TILE
    npt = N_PROMPT // tm
    row = lambda i: (i, 0)
    acts = [ygm, yssd, yxa, gates]
    consts = [pgm, pssd, pxa, wout, lng, lnb]
    return pl.pallas_call(
        _merge_kernel,
        grid=(N_TOK // tm,),
        in_specs=[pl.BlockSpec((tm, a.shape[1]), row) for a in acts]
                 + [pl.BlockSpec((tm, D_MODEL), lambda i: (jnp.minimum(i, npt - 1), 0)),
                    pl.BlockSpec((tm, D_MODEL), lambda i: (x_s_row0 // tm + jnp.maximum(i - npt, 0), 0))]
                 + [_const_spec(a.shape) for a in consts],
        out_specs=[pl.BlockSpec((tm, D_MODEL), row), pl.BlockSpec((tm, D_MODEL), row)],
        out_shape=[jax.ShapeDtypeStruct((N_TOK, D_MODEL), F32), jax.ShapeDtypeStruct((N_TOK, D_MODEL), BF16)],
        compiler_params=_params(("parallel",)), name="merge_out_ln",
    )(*acts, x_p, x_s, *consts)


FFN_TILE = 512
FFN_SUB = 256


def _ffn_kernel(xb, xf, wg, wu, wd, lng, lnb, o_f, o_b):
    for r in range(FFN_TILE // FFN_SUB):
        rows = slice(r * FFN_SUB, (r + 1) * FFN_SUB)
        x = xb[rows, :]
        h = (_silu(_dot(x, wg[...])) * _dot(x, wu[...])).astype(BF16)
        y = _layer_norm(ALPHA * xf[rows, :] + _dot(h, wd[...]), lng[...], lnb[...])
        o_f[rows, :] = y
        o_b[rows, :] = y.astype(BF16)


def _resident_spec(shape):
    nd = len(shape)
    return pl.BlockSpec(shape, lambda *_: (0,) * nd, pipeline_mode=pl.Buffered(1))


def _ffn(xb, xf, wg, wu, wd, lng, lnb):
    tm = FFN_TILE
    row = lambda i: (i, 0)
    return pl.pallas_call(
        _ffn_kernel,
        grid=(N_TOK // tm,),
        in_specs=[pl.BlockSpec((tm, D_MODEL), row), pl.BlockSpec((tm, D_MODEL), row),
                  _resident_spec(wg.shape), _resident_spec(wu.shape), _resident_spec(wd.shape),
                  _const_spec(lng.shape), _const_spec(lnb.shape)],
        out_specs=[pl.BlockSpec((tm, D_MODEL), row), pl.BlockSpec((tm, D_MODEL), row)],
        out_shape=[jax.ShapeDtypeStruct((N_TOK, D_MODEL), F32), jax.ShapeDtypeStruct((N_TOK, D_MODEL), BF16)],
        compiler_params=_params(("parallel",)), name="ffn_swiglu_ln",
    )(xb, xf, wg, wu, wd, lng, lnb)


ROUTER_TILE = 1024


def _router_kernel(x_ref, w_ref, b_ref, l_ref, ri_ref, rg_ref, cnt_ref, count):
    @pl.when(pl.program_id(0) == 0)
    def _():
        count[...] = jnp.zeros_like(count)

    xs = _split2(x_ref[...])
    ws = _split2(w_ref[...])
    logits = b_ref[...] + _dot(xs[0], ws[0]) + _dot(xs[0], ws[1]) + _dot(xs[1], ws[0])
    lane = lax.broadcasted_iota(jnp.int32, logits.shape, 1)
    logits = jnp.where(lane < N_EXPERTS, logits, -1e30)
    m1 = jnp.max(logits, axis=-1, keepdims=True)
    i1 = jnp.min(jnp.where(logits == m1, lane, LANES), axis=-1, keepdims=True)
    rest = jnp.where(lane == i1, -1e30, logits)
    m2 = jnp.max(rest, axis=-1, keepdims=True)
    i2 = jnp.min(jnp.where(rest == m2, lane, LANES), axis=-1, keepdims=True)
    e2 = jnp.exp(m2 - m1)
    den = 1.0 + e2

    hit1, hit2 = lane == i1, lane == i2
    assigned = jnp.where(hit1 | hit2, 1.0, 0.0)
    rank = _dot(l_ref[...], assigned.astype(BF16)) + count[...]
    r1 = jnp.sum(jnp.where(hit1, rank, 0.0), axis=-1, keepdims=True).astype(jnp.int32)
    r2 = jnp.sum(jnp.where(hit2, rank, 0.0), axis=-1, keepdims=True).astype(jnp.int32)
    ri_ref[...] = jnp.where(lane == 0, i1, jnp.where(lane == 1, i2, jnp.where(lane == 2, r1,
                            jnp.where(lane == 3, r2, 0))))
    rg_ref[...] = jnp.where(lane == 0, 1.0 / den, jnp.where(lane == 1, e2 / den, 0.0))
    count[...] = count[...] + jnp.sum(assigned, axis=0, keepdims=True)
    cnt_ref[...] = count[...]


def _router(x, w_pad, b_pad):
    tm = ROUTER_TILE
    strict_lower = jnp.tril(jnp.ones((tm, tm), F32), k=-1).astype(BF16)
    return pl.pallas_call(
        _router_kernel,
        grid=(N_TOK // tm,),
        in_specs=[pl.BlockSpec((tm, D_MODEL), lambda i: (i, 0)), _const_spec(w_pad.shape),
                  _const_spec(b_pad.shape), _const_spec(strict_lower.shape)],
        out_specs=[pl.BlockSpec((tm, LANES), lambda i: (i, 0)), pl.BlockSpec((tm, LANES), lambda i: (i, 0)),
                   _const_spec((1, LANES))],
        out_shape=[jax.ShapeDtypeStruct((N_TOK, LANES), jnp.int32), jax.ShapeDtypeStruct((N_TOK, LANES), F32),
                   jax.ShapeDtypeStruct((1, LANES), F32)],
        scratch_shapes=[pltpu.VMEM((1, LANES), F32)],
        compiler_params=_params(("arbitrary",)), name="moe_router",
    )(x, w_pad, b_pad, strict_lower)


EXPERT_TILE = 1024
N_SLOTS = N_TOK * 2 + N_EXPERTS * EXPERT_TILE
N_SLOT_TILES = N_SLOTS // EXPERT_TILE
MOE_TOK_TILE = 1024
MOE_FF_TILE = 512


def _row_copy(src, src_row, dst, dst_row, sem):
    return pltpu.make_async_copy(src.at[pl.ds(src_row, 1)], dst.at[pl.ds(dst_row, 1)], sem)


N_ZERO_TILES = 2 * N_EXPERTS


def _dispatch_kernel(zt_ref, zf_ref, s1_ref, s2_ref, x_ref, xs_out, zeros, sem, zsem):
    def zero_copy(k):
        dst = xs_out.at[pl.ds(pl.multiple_of(zt_ref[k] * EXPERT_TILE, EXPERT_TILE), EXPERT_TILE)]
        return pltpu.make_async_copy(zeros, dst, zsem)

    @pl.when(pl.program_id(0) == 0)
    def _():
        zeros[...] = jnp.zeros_like(zeros)
        for k in range(N_ZERO_TILES):
            @pl.when(zf_ref[k] == 1)
            def _():
                zero_copy(k).start()
        for k in range(N_ZERO_TILES):
            @pl.when(zf_ref[k] == 1)
            def _():
                zero_copy(k).wait()

    def body(r, carry):
        _row_copy(x_ref, r, xs_out, s1_ref[r], sem).start(priority=0)
        _row_copy(x_ref, r, xs_out, s2_ref[r], sem).start(priority=1)
        return carry

    lax.fori_loop(0, MOE_TOK_TILE, body, 0, unroll=8)
    for _ in range(2):
        pltpu.make_async_copy(x_ref, xs_out.at[pl.ds(0, MOE_TOK_TILE)], sem).wait()


def _dispatch(zero_tiles, zero_flags, slot1, slot2, x):
    tm = MOE_TOK_TILE
    smem = lambda: pl.BlockSpec((tm,), lambda i, zt, zf: (i,), memory_space=pltpu.SMEM)
    grid_spec = pltpu.PrefetchScalarGridSpec(
        num_scalar_prefetch=2,
        grid=(N_TOK // tm,),
        in_specs=[smem(), smem(), pl.BlockSpec((tm, D_MODEL), lambda i, zt, zf: (i, 0))],
        out_specs=pl.BlockSpec(memory_space=pl.ANY),
        scratch_shapes=[pltpu.VMEM((EXPERT_TILE, D_MODEL), F32), pltpu.SemaphoreType.DMA(()),
                        pltpu.SemaphoreType.DMA(())],
    )
    return pl.pallas_call(
        _dispatch_kernel, grid_spec=grid_spec,
        out_shape=jax.ShapeDtypeStruct((N_SLOTS, D_MODEL), F32),
        compiler_params=_params(("arbitrary",)), name="moe_dispatch",
    )(zero_tiles, zero_flags, slot1, slot2, x)


EXPERT_ROW_STEP = 256


def _expert_ffn_kernel(te_ref, nv_ref, tr_ref, xs_ref, wg, wu, wd, ys_ref, xb_s):
    del te_ref
    t = pl.program_id(0)
    k = pl.program_id(1)

    @pl.when(t < nv_ref[0])
    def _():
        @pl.when(k == 0)
        def _():
            xb_s[...] = xs_ref[...].astype(BF16)
            ys_ref[...] = jnp.zeros_like(ys_ref)

        wg_b, wu_b, wd_b = wg[0, 0].astype(BF16), wu[0, 0].astype(BF16), wd[0, 0].astype(BF16)
        for m in range(1, EXPERT_TILE // EXPERT_ROW_STEP + 1):
            @pl.when(tr_ref[t] == m)
            def _():
                rows = slice(0, m * EXPERT_ROW_STEP)
                xb = xb_s[rows, :]
                h = _silu(_dot(xb, wg_b)) * _dot(xb, wu_b)
                ys_ref[rows, :] += _dot(h.astype(BF16), wd_b)

    @pl.when((t >= nv_ref[0]) & (k == 0))
    def _():
        ys_ref[...] = jnp.zeros_like(ys_ref)


def _expert_ffn(layer, tile_expert, n_valid, tile_rows, xs, wg, wu, wd):
    tm, tf = EXPERT_TILE, MOE_FF_TILE
    nk = E_FF // tf

    def tile(t, nv):
        return jnp.minimum(t, nv[0] - 1)

    def chunk(t, k, nv):
        return jnp.where(t < nv[0], k, nk - 1)

    grid_spec = pltpu.PrefetchScalarGridSpec(
        num_scalar_prefetch=3,
        grid=(N_SLOT_TILES, nk),
        in_specs=[pl.BlockSpec((tm, D_MODEL), lambda t, k, te, nv, tr: (tile(t, nv), 0)),
                  pl.BlockSpec((1, 1, D_MODEL, tf),
                               lambda t, k, te, nv, tr: (layer, te[tile(t, nv)], 0, chunk(t, k, nv))),
                  pl.BlockSpec((1, 1, D_MODEL, tf),
                               lambda t, k, te, nv, tr: (layer, te[tile(t, nv)], 0, chunk(t, k, nv))),
                  pl.BlockSpec((1, 1, tf, D_MODEL),
                               lambda t, k, te, nv, tr: (layer, te[tile(t, nv)], chunk(t, k, nv), 0))],
        out_specs=pl.BlockSpec((tm, D_MODEL), lambda t, k, te, nv, tr: (t, 0)),
        scratch_shapes=[pltpu.VMEM((tm, D_MODEL), BF16)],
    )
    return pl.pallas_call(
        _expert_ffn_kernel, grid_spec=grid_spec,
        out_shape=jax.ShapeDtypeStruct((N_SLOTS, D_MODEL), F32),
        compiler_params=_params(("arbitrary", "arbitrary")), name="moe_expert_ffn",
    )(tile_expert, n_valid, tile_rows, xs, wg, wu, wd)


def _combine_kernel(s1_ref, s2_ref, s1n_ref, s2n_ref, x_ref, rg_ref, ys_hbm, lng, lnb, o_p, o_s, buf, sem):
    tm = MOE_TOK_TILE
    i = pl.program_id(0)

    def gather(sa_ref, sb_ref, slot):
        def body(r, carry):
            _row_copy(ys_hbm, sa_ref[r], buf.at[slot, 0], r, sem.at[slot]).start(priority=0)
            _row_copy(ys_hbm, sb_ref[r], buf.at[slot, 1], r, sem.at[slot]).start(priority=1)
            return carry

        lax.fori_loop(0, tm, body, 0, unroll=8)

    @pl.when(i == 0)
    def _():
        gather(s1_ref, s2_ref, 0)

    @pl.when(i + 1 < pl.num_programs(0))
    def _():
        gather(s1n_ref, s2n_ref, (i + 1) % 2)

    slot = i % 2
    for j in range(2):
        pltpu.make_async_copy(ys_hbm.at[pl.ds(0, tm)], buf.at[slot, j], sem.at[slot]).wait()

    g = rg_ref[...]
    f = g[:, 0:1] * buf[slot, 0] + g[:, 1:2] * buf[slot, 1]
    y = _layer_norm(ALPHA * x_ref[...] + f, lng[...], lnb[...])

    @pl.when(i < N_PROMPT // tm)
    def _():
        o_p[...] = y

    @pl.when(i >= N_PROMPT // tm)
    def _():
        o_s[...] = y


def _combine(slot1, slot2, x, rg, ys, lng, lnb):
    tm = MOE_TOK_TILE
    npt = N_PROMPT // tm
    n_tiles = N_TOK // tm
    smem = lambda: pl.BlockSpec((tm,), lambda i: (i,), memory_space=pltpu.SMEM)
    smem_next = lambda: pl.BlockSpec((tm,), lambda i: (jnp.minimum(i + 1, n_tiles - 1),), memory_space=pltpu.SMEM)
    return pl.pallas_call(
        _combine_kernel,
        grid=(n_tiles,),
        in_specs=[smem(), smem(), smem_next(), smem_next(), pl.BlockSpec((tm, D_MODEL), lambda i: (i, 0)),
                  pl.BlockSpec((tm, LANES), lambda i: (i, 0)), pl.BlockSpec(memory_space=pl.ANY),
                  _const_spec(lng.shape), _const_spec(lnb.shape)],
        out_specs=[pl.BlockSpec((tm, D_MODEL), lambda i: (jnp.minimum(i, npt - 1), 0)),
                   pl.BlockSpec((tm, D_MODEL), lambda i: (jnp.maximum(i - npt, 0), 0))],
        out_shape=[jax.ShapeDtypeStruct((N_PROMPT, D_MODEL), F32), jax.ShapeDtypeStruct((N_SAMPLE, D_MODEL), F32)],
        scratch_shapes=[pltpu.VMEM((2, 2, tm, D_MODEL), F32), pltpu.SemaphoreType.DMA((2,))],
        compiler_params=_params(("arbitrary",)), name="moe_combine_ln",
    )(slot1, slot2, slot1, slot2, x, rg, ys, lng, lnb)


def _moe(layer, x, router_w, router_b, wg, wu, wd, lng, lnb):
    ri, rg, cnt = _router(x, _pad_lanes(router_w), _pad_lanes(_row(router_b)))
    counts = cnt[0, :N_EXPERTS].astype(jnp.int32)
    padded = (counts + EXPERT_TILE - 1) // EXPERT_TILE * EXPERT_TILE
    ends = jnp.cumsum(padded)
    starts = ends - padded
    slot1 = starts[ri[:, 0]] + ri[:, 2]
    slot2 = starts[ri[:, 1]] + ri[:, 3]
    n_valid = (ends[-1:] // EXPERT_TILE).astype(jnp.int32)
    tile_start = jnp.arange(N_SLOT_TILES, dtype=jnp.int32) * EXPERT_TILE
    tile_expert = jnp.minimum(jnp.sum(tile_start[:, None] >= ends[None, :], axis=1), N_EXPERTS - 1).astype(jnp.int32)

    tail = n_valid[0] + jnp.arange(N_EXPERTS, dtype=jnp.int32)
    zero_tiles = jnp.concatenate([(ends // EXPERT_TILE - 1).astype(jnp.int32), tail])
    zero_flags = jnp.concatenate([padded > 0, tail < N_SLOT_TILES]).astype(jnp.int32)
    zero_tiles = jnp.where(zero_flags == 1, zero_tiles, 0)

    xs = _dispatch(zero_tiles, zero_flags, slot1, slot2, x)
    rows_left = counts[tile_expert] - (tile_start - starts[tile_expert])
    tile_rows = (jnp.clip(rows_left, 1, EXPERT_TILE) + EXPERT_ROW_STEP - 1) // EXPERT_ROW_STEP
    ys = _expert_ffn(layer, tile_expert, n_valid, tile_rows.astype(jnp.int32), xs, wg, wu, wd)
    return _combine(slot1, slot2, x, rg, ys, lng, lnb)


def _row(a):
    return a.reshape(1, -1).astype(F32)


def _pad_lanes(a):
    return jnp.pad(a, ((0, 0), (0, LANES - a.shape[1])))


def _gmlp_weights(w_s, b_s):
    tril = jnp.tril(jnp.ones((GM_CHUNK, GM_CHUNK), dtype=bool))
    w_p = jnp.where(tril, w_s, 0.0)
    n_seq = ROWS // DEC_SEQ
    w_8 = w_p[:, :DEC_SEQ, :DEC_SEQ]
    w_d = jnp.einsum("ab,gij->gaibj", jnp.eye(n_seq, dtype=F32), w_8).reshape(GM_GROUPS, ROWS, ROWS)
    bias_p = jnp.repeat(b_s.T, GM_GROUP, axis=1)
    bias_d = jnp.tile(bias_p[:DEC_SEQ], (n_seq, 1))
    return jnp.stack([w_p, w_d]).astype(BF16), jnp.stack([bias_p, bias_d])


def kernel(x_prompt, x_sample, mem_prompt, cache_mem_k, cache_mem_v, state_conv, state_ssm, w_in, conv_w, conv_b, dt_bias, a_log, d_skip, ssd_norm_g, v_ln_g, v_ln_b, w_s, b_s, p_gm, p_ssd, p_xa, w_out, w_mem_k, w_mem_v, ln1_g, ln1_b, ln2_g, ln2_b, ffn_wg, ffn_wu, ffn_wd, router_w, router_b, moe_wg, moe_wu, moe_wd):
    assert DEPTH % 2 == 0
    x_p = x_prompt.reshape(N_PROMPT, D_MODEL)
    x_s = x_sample.reshape(N_SAMPLE, D_MODEL)
    x_s_row0 = 0
    xb = jnp.concatenate([x_p.astype(BF16), x_s.astype(BF16)], axis=0)
    mem_b = mem_prompt.reshape(BATCH * N_MEM, D_MODEL).astype(BF16)
    w_in_t = jnp.swapaxes(w_in, 1, 2)
    st_all = state_ssm.reshape(DEPTH, DEC_BATCH, SSD_INNER, SSD_STATE)

    mem_k_out, mem_v_out, mem_kb, mem_vb = _mem_kv(mem_b, w_mem_k, w_mem_v)
    mem_kb = mem_kb.reshape(DEPTH * BATCH, N_MEM, XA_WIDTH)
    mem_vb = mem_vb.reshape(DEPTH * BATCH, N_MEM, XA_WIDTH)

    ssm_p_out, conv_p_out, conv_s_out, v_out = [], [], [], []
    ssm_s = None
    for i in range(DEPTH):
        (u,) = _mm(xb, w_in_t, i, 0, GM_WIDTH, 1024, [], _epi_gelu, [BF16], "in_u")
        (v,) = _mm(xb, w_in_t, i, GM_WIDTH, GM_WIDTH, 1024, [_row(v_ln_g[i]), _row(v_ln_b[i])], _epi_gelu_ln, [F32],
                   "in_v", tm=1024)
        (zs,) = _mm(xb, w_in_t, i, OFF_Z, SSD_INNER, 1024, [], _epi_silu, [BF16], "in_z")
        (xbc,) = _mm(xb, w_in_t, i, OFF_XBC, CONV_DIM, 1024, [], _epi_id, [BF16], "in_xbc")
        (dt,) = _mm(xb, w_in_t, i, OFF_DT, LANES, LANES, [_pad_lanes(_row(dt_bias[i]))], _epi_softplus, [F32], "in_dt")
        (q,) = _mm(xb, w_in_t, i, OFF_Q, XA_WIDTH, 1024, [], _epi_qscale, [BF16], "in_q")
        (gates,) = _mm(xb, w_in_t, i, OFF_GATE, N_BRANCH * D_MODEL, 1024, [], _epi_sigmoid, [BF16], "in_gates")

        gm_w, gm_b = _gmlp_weights(w_s[i], b_s[i])
        y_gm = _gmlp(v, u, gm_w, gm_b)

        alog = _pad_lanes(_row(a_log[i]))
        dskip = _row(jnp.repeat(d_skip[i], SSD_HEADDIM))
        ssd_args = (alog, conv_w[i], _row(conv_b[i]), dskip, _row(ssd_norm_g[i]))
        y_ssd, ssm_p = _ssd_prompt(xbc, dt, zs, *ssd_args)
        prev_rows = jnp.pad(state_conv[i], ((0, 0), (DEC_SEQ - (SSD_CONV - 1), 0), (0, 0))).reshape(N_SAMPLE, CONV_DIM)
        y_ssd, ssm_s = _ssd_sample(i, xbc, prev_rows, dt, zs, st_all, y_ssd, ssm_s, *ssd_args)
        ssm_p_out.append(ssm_p.reshape(BATCH, SSD_HEADS, SSD_HEADDIM, SSD_STATE))
        slots = xbc.reshape(N_TOK // DEC_SEQ, DEC_SEQ, CONV_DIM)
        n_p, per_seq, keep = N_PROMPT // DEC_SEQ, SEQ // DEC_SEQ, SSD_CONV - 1
        conv_p_out.append(lax.slice(slots, (per_seq - 1, DEC_SEQ - keep, 0), (n_p, DEC_SEQ, CONV_DIM),
                                    (per_seq, 1, 1)).astype(F32))
        conv_s_out.append(lax.slice(slots, (n_p, DEC_SEQ - keep, 0), slots.shape).astype(F32))
        v_out.append(v[N_PROMPT:].reshape(DEC_BATCH, DEC_SEQ, GM_WIDTH))

        y_xa = _xattn_prompt(i, q, mem_kb, mem_vb)
        y_xa = _xattn_sample(i, q, cache_mem_k, cache_mem_v, y_xa)

        x, xb = _merge(y_gm, y_ssd, y_xa, gates, x_p, x_s, x_s_row0, p_gm[i].astype(BF16), p_ssd[i].astype(BF16),
                       p_xa[i].astype(BF16), w_out[i].astype(BF16), _row(ln1_g[i]), _row(ln1_b[i]))

        j = i // 2
        if i % 2 == 0:
            x, xb = _ffn(xb, x, ffn_wg[j].astype(BF16), ffn_wu[j].astype(BF16), ffn_wd[j].astype(BF16),
                         _row(ln2_g[i]), _row(ln2_b[i]))
            x_p, x_s, x_s_row0 = x, x, N_PROMPT
        else:
            x_p, x_s = _moe(j, x, router_w[j], router_b[j], moe_wg, moe_wu, moe_wd, _row(ln2_g[i]), _row(ln2_b[i]))
            x_s_row0 = 0
            if i + 1 < DEPTH:
                xb = jnp.concatenate([x_p.astype(BF16), x_s.astype(BF16)], axis=0)

    y_prompt = x_p.reshape(BATCH, SEQ, D_MODEL)
    y_sample = x_s.reshape(DEC_BATCH, DEC_SEQ, D_MODEL)
    ssm_s_out = ssm_s.reshape(DEPTH, DEC_BATCH, SSD_HEADS, SSD_HEADDIM, SSD_STATE)
    return (y_prompt, y_sample, mem_k_out, mem_v_out, jnp.stack(conv_p_out),
            jnp.stack(ssm_p_out), jnp.stack(conv_s_out), ssm_s_out, jnp.stack(v_out))
```

```python
import functools

import jax
import jax.numpy as jnp
from jax import lax
from jax.experimental import pallas as pl
from jax.experimental.pallas import tpu as pltpu

F32 = jnp.float32
BF16 = jnp.bfloat16

D_MODEL = 1024
BATCH = 8
SEQ = 2048
DEPTH = 2
DEC_BATCH = 128
DEC_SEQ = 8
N_MEM = 256
GM_WIDTH = D_MODEL
GM_CHUNK = 128
GM_GROUP = 128
GM_GROUPS = GM_WIDTH // GM_GROUP
SSD_INNER = 2 * D_MODEL
SSD_HEADDIM = 64
SSD_HEADS = SSD_INNER // SSD_HEADDIM
SSD_STATE = 128
SSD_GROUPS = 4
SSD_HPG = SSD_HEADS // SSD_GROUPS
SSD_CONV = 4
SSD_CHUNK = 128
CONV_DIM = SSD_INNER + 2 * SSD_GROUPS * SSD_STATE
XA_HEADS = 4
XA_HEADDIM = D_MODEL // XA_HEADS
XA_WIDTH = XA_HEADS * XA_HEADDIM
N_BRANCH = 3
D_FF = ((8 * D_MODEL // 3 + 127) // 128) * 128
N_EXPERTS = 8
E_FF = 7 * D_MODEL // 2
ALPHA = (2 * DEPTH) ** 0.25
LN_EPS = 1e-5

N_PROMPT = BATCH * SEQ
N_SAMPLE = DEC_BATCH * DEC_SEQ
N_TOK = N_PROMPT + N_SAMPLE

OFF_Z = 2 * GM_WIDTH
OFF_XBC = OFF_Z + SSD_INNER
OFF_DT = OFF_XBC + CONV_DIM
OFF_Q = OFF_DT + SSD_HEADS
OFF_GATE = OFF_Q + XA_WIDTH

LANES = 128
ROWS = 128
GROUP_W = SSD_HPG * SSD_HEADDIM
VMEM_LIMIT = 56 * 1024 * 1024


def _params(sem):
    return pltpu.CompilerParams(dimension_semantics=sem, vmem_limit_bytes=VMEM_LIMIT)


def _dot(a, b):
    return jnp.dot(a, b, preferred_element_type=F32)


def _dot_nt(a, b):
    return lax.dot_general(a, b, (((1,), (1,)), ((), ())), preferred_element_type=F32)


def _split2(x):
    hi = x.astype(BF16)
    lo = (x - hi.astype(F32)).astype(BF16)
    return hi, lo


def _split3(x):
    hi = x.astype(BF16)
    r = x - hi.astype(F32)
    mid = r.astype(BF16)
    lo = (r - mid.astype(F32)).astype(BF16)
    return hi, mid, lo


def _dot_exact_lhs(m_bf16, x, pieces=3):
    parts = _split3(x) if pieces == 3 else _split2(x)
    n = x.shape[1]
    out = _dot(m_bf16, jnp.concatenate(parts, axis=1))
    return sum(out[:, i * n:(i + 1) * n] for i in range(1, pieces)) + out[:, :n]


def _expand_rhs2(xs, m_bf16):
    rows = xs[0].shape[0]
    out = _dot(jnp.concatenate([p for x in xs for p in _split2(x)], axis=0), m_bf16)
    return [out[2 * i * rows:(2 * i + 1) * rows] + out[(2 * i + 1) * rows:(2 * i + 2) * rows] for i in range(len(xs))]


def _layer_norm(r, g, b):
    mu = jnp.mean(r, axis=-1, keepdims=True)
    c = r - mu
    var = jnp.mean(c * c, axis=-1, keepdims=True)
    return c * lax.rsqrt(var + LN_EPS) * g + b


def _gelu_tanh(x):
    return x * (0.5 * (1.0 + jnp.tanh(0.7978845608028654 * (x + 0.044715 * (x * x * x)))))


def _sigmoid(x):
    return 1.0 / (1.0 + jnp.exp(-x))


def _silu(x):
    return x * _sigmoid(x)


def _softplus(x):
    return jnp.maximum(x, 0.0) + jnp.log(1.0 + jnp.exp(-jnp.abs(x)))


def _mm_kernel(x_ref, wt_ref, *rest, epilogue, n_extra, n_out):
    extras = [r[...] for r in rest[:n_extra]]
    outs = rest[n_extra:n_extra + n_out]
    w_bf16 = rest[n_extra + n_out]

    @pl.when(pl.program_id(1) == 0)
    def _():
        w_bf16[...] = wt_ref[0].T.astype(BF16)

    acc = _dot(x_ref[...], w_bf16[...])
    res = epilogue(acc, *extras)
    for o, r in zip(outs, res):
        o[...] = r.astype(o.dtype)


MM_TILE = 2176


def _mm(x, wt, layer, row0, n_cols, tn, extras, epilogue, out_dtypes, name, tm=MM_TILE):
    t, k = x.shape
    assert row0 % 8 == 0
    grid = (n_cols // tn, t // tm)
    in_specs = [pl.BlockSpec((tm, k), lambda j, i: (i, 0)),
                pl.BlockSpec((pl.Element(1), pl.Element(tn), pl.Element(k)),
                             lambda j, i: (layer, pl.multiple_of(row0 + j * tn, 8), 0))]
    in_specs += [pl.BlockSpec((1, tn), lambda j, i: (0, j)) for _ in extras]
    out_specs = [pl.BlockSpec((tm, tn), lambda j, i: (i, j)) for _ in out_dtypes]
    out_shape = [jax.ShapeDtypeStruct((t, n_cols), d) for d in out_dtypes]
    return pl.pallas_call(
        functools.partial(_mm_kernel, epilogue=epilogue, n_extra=len(extras), n_out=len(out_dtypes)),
        grid=grid, in_specs=in_specs, out_specs=out_specs, out_shape=out_shape,
        scratch_shapes=[pltpu.VMEM((k, tn), BF16)],
        compiler_params=_params(("parallel", "arbitrary")), name=name,
    )(x, wt, *extras)


MEM_TILE_SEQS = 4


def _mem_kv_kernel(m_ref, wk_ref, wv_ref, k5_ref, v5_ref, kb_ref, vb_ref):
    m = m_ref[...]
    for w_ref, o5_ref, ob_ref in ((wk_ref, k5_ref, kb_ref), (wv_ref, v5_ref, vb_ref)):
        acc = _dot(m, w_ref[0].astype(BF16))
        ob_ref[0] = acc.astype(BF16)
        for b in range(MEM_TILE_SEQS):
            for h in range(XA_HEADS):
                o5_ref[0, b, :, h, :] = acc[b * N_MEM:(b + 1) * N_MEM, h * XA_HEADDIM:(h + 1) * XA_HEADDIM]


def _mem_kv(mem_b, w_mem_k, w_mem_v):
    tm = MEM_TILE_SEQS * N_MEM
    n5 = (DEPTH, BATCH, N_MEM, XA_HEADS, XA_HEADDIM)
    w_spec = pl.BlockSpec((1, D_MODEL, XA_WIDTH), lambda l, r: (l, 0, 0))
    o5_spec = pl.BlockSpec((1, MEM_TILE_SEQS, N_MEM, XA_HEADS, XA_HEADDIM), lambda l, r: (l, r, 0, 0, 0))
    ob_spec = pl.BlockSpec((1, tm, XA_WIDTH), lambda l, r: (l, r, 0))
    return pl.pallas_call(
        _mem_kv_kernel,
        grid=(DEPTH, BATCH // MEM_TILE_SEQS),
        in_specs=[pl.BlockSpec((tm, D_MODEL), lambda l, r: (r, 0)), w_spec, w_spec],
        out_specs=[o5_spec, o5_spec, ob_spec, ob_spec],
        out_shape=[jax.ShapeDtypeStruct(n5, F32), jax.ShapeDtypeStruct(n5, F32),
                   jax.ShapeDtypeStruct((DEPTH, BATCH * N_MEM, XA_WIDTH), BF16),
                   jax.ShapeDtypeStruct((DEPTH, BATCH * N_MEM, XA_WIDTH), BF16)],
        compiler_params=_params(("parallel", "parallel")), name="mem_kv",
    )(mem_b, w_mem_k, w_mem_v)


def _epi_gelu(acc):
    return (_gelu_tanh(acc),)


def _epi_gelu_ln(acc, g, b):
    return (_layer_norm(_gelu_tanh(acc), g, b),)


def _epi_silu(acc):
    return (_silu(acc),)


def _epi_id(acc):
    return (acc,)


def _epi_softplus(acc, bias):
    return (_softplus(acc + bias),)


def _epi_qscale(acc):
    return (acc * (XA_HEADDIM ** -0.5),)


def _epi_sigmoid(acc):
    return (_sigmoid(acc),)


GM_TILE = 512


def _gmlp_kernel(v_ref, u_ref, w_ref, b_ref, o_ref):
    for c in range(GM_TILE // ROWS):
        rs = slice(c * ROWS, (c + 1) * ROWS)
        for g in range(GM_GROUPS):
            cs = slice(g * GM_GROUP, (g + 1) * GM_GROUP)
            z = _dot(w_ref[0, g], v_ref[rs, cs].astype(BF16)) + b_ref[0, :, cs]
            o_ref[rs, cs] = (u_ref[rs, cs].astype(F32) * z).astype(o_ref.dtype)


def _gmlp(v, u, w2, b2):
    n_prompt_tiles = N_PROMPT // GM_TILE

    def sel(i):
        return jnp.where(i >= n_prompt_tiles, 1, 0)

    return pl.pallas_call(
        _gmlp_kernel,
        grid=(N_TOK // GM_TILE,),
        in_specs=[pl.BlockSpec((GM_TILE, GM_WIDTH), lambda i: (i, 0)),
                  pl.BlockSpec((GM_TILE, GM_WIDTH), lambda i: (i, 0)),
                  pl.BlockSpec((1, GM_GROUPS, ROWS, ROWS), lambda i: (sel(i), 0, 0, 0)),
                  pl.BlockSpec((1, ROWS, GM_WIDTH), lambda i: (sel(i), 0, 0))],
        out_specs=pl.BlockSpec((GM_TILE, GM_WIDTH), lambda i: (i, 0)),
        out_shape=jax.ShapeDtypeStruct((N_TOK, GM_WIDTH), BF16),
        compiler_params=_params(("parallel",)), name="gmlp_spatial",
    )(v, u, w2, b2)


def _conv_silu(x_back, conv_w, conv_b):
    conv = conv_b + x_back(0) * conv_w[3:4, :]
    for s in range(1, SSD_CONV):
        conv = conv + x_back(s) * conv_w[3 - s:4 - s, :]
    return _silu(conv)


def _ssd_block(xc, dt, alog, lmat, bones, emat, dskip):
    xs = xc[:, :SSD_INNER].astype(F32)
    bm = xc[:, SSD_INNER:SSD_INNER + SSD_GROUPS * SSD_STATE].astype(BF16)
    cm = xc[:, SSD_INNER + SSD_GROUPS * SSD_STATE:].astype(BF16)

    lane = lax.broadcasted_iota(jnp.int32, (1, LANES), 1)
    a_neg = jnp.where(lane < SSD_HEADS, -jnp.exp(alog), 0.0)
    adt = dt * a_neg
    sums = _dot_exact_lhs(jnp.concatenate([lmat, bones], axis=0), adt)
    a_cs, a_tot = sums[:ROWS], sums[ROWS:]
    a_cs_t = a_cs.T
    mask = lmat.astype(F32) > 0.5

    dt_x, eacs_x, te_x = _expand_rhs2([dt, jnp.exp(a_cs), jnp.exp(a_tot - a_cs)], emat)
    xdt = xs * dt_x
    xw_t = (xdt * te_x).T.astype(BF16)

    lane_r = lax.broadcasted_iota(jnp.int32, (ROWS, LANES), 1)
    lo_half = lane_r < SSD_HEADDIM
    yd = []
    for g in range(SSD_GROUPS):
        ns = slice(g * SSD_STATE, (g + 1) * SSD_STATE)
        cb = _dot_nt(cm[:, ns], bm[:, ns])
        for hp in range(SSD_HPG // 2):
            h0 = g * SSD_HPG + 2 * hp
            ms = []
            for h in (h0, h0 + 1):
                seg = a_cs[:, h:h + 1] - a_cs_t[h:h + 1, :]
                ms.append(cb * jnp.exp(jnp.where(mask, seg, -1e30)))
            lhs = jnp.concatenate(ms, axis=1).astype(BF16)
            xp = xdt[:, h0 * SSD_HEADDIM:(h0 + 2) * SSD_HEADDIM]
            rhs = jnp.concatenate([jnp.where(lo_half, xp, 0.0), jnp.where(lo_half, 0.0, xp)],
                                  axis=0).astype(BF16)
            yd.append(_dot(lhs, rhs))
    y_pre = jnp.concatenate(yd, axis=1) + xs * dskip
    return dict(y_pre=y_pre, eacs_x=eacs_x, cm=cm, bm=bm, xw_t=xw_t, a_tot=a_tot)


def _ssd_finish(y, zs, norm_g):
    y = y * zs
    outs = []
    for g in range(SSD_GROUPS):
        yg = y[:, g * GROUP_W:(g + 1) * GROUP_W]
        ms = jnp.mean(yg * yg, axis=-1, keepdims=True)
        outs.append(yg * lax.rsqrt(ms + LN_EPS))
    return jnp.concatenate(outs, axis=1) * norm_g


PROMPT_STEP_ROWS = 2 * ROWS


def _ssd_prompt_kernel(*refs):
    y_ref = refs[12]

    @pl.when(pl.program_id(0) < BATCH)
    def _():
        _ssd_prompt_body(*refs)

    @pl.when(pl.program_id(0) == BATCH)
    def _():
        y_ref[...] = jnp.zeros_like(y_ref)


def _ssd_prompt_body(xbc_ref, dt_ref, zs_ref, alog_ref, cw_ref, cb_ref, l_ref, ones_ref, e_ref, et_ref, dskip_ref,
                     ng_ref, y_ref, st_ref, xpad, state):
    @pl.when(pl.program_id(1) == 0)
    def _():
        xpad[0:8, :] = jnp.zeros((8, CONV_DIM), F32)
        state[...] = jnp.zeros_like(state)

    xpad[8:8 + PROMPT_STEP_ROWS, :] = xbc_ref[...].astype(F32)
    for j in range(PROMPT_STEP_ROWS // ROWS):
        rows = slice(j * ROWS, (j + 1) * ROWS)
        base = 8 + j * ROWS
        xc = _conv_silu(lambda s: xpad[base - s:base - s + ROWS, :], cw_ref[...], cb_ref[...])
        r = _ssd_block(xc, dt_ref[rows, :], alog_ref[...], l_ref[...], ones_ref[...], e_ref[...], dskip_ref[...])

        st = state[...]
        st_b = st.astype(BF16)
        y_off, s_new = [], []
        for g in range(SSD_GROUPS):
            ns = slice(g * SSD_STATE, (g + 1) * SSD_STATE)
            gs = slice(g * GROUP_W, (g + 1) * GROUP_W)
            y_off.append(_dot_nt(r["cm"][:, ns], st_b[gs, :]))
            s_new.append(_dot(r["xw_t"][gs, :], r["bm"][:, ns]))
        y = r["y_pre"] + jnp.concatenate(y_off, axis=1) * r["eacs_x"]
        y_ref[rows, :] = _ssd_finish(y, zs_ref[rows, :].astype(F32), ng_ref[...]).astype(y_ref.dtype)

        decay = _dot_exact_lhs(et_ref[...], jnp.exp(r["a_tot"].T), pieces=2)
        state[...] = decay * st + jnp.concatenate(s_new, axis=0)
    xpad[0:8, :] = xpad[PROMPT_STEP_ROWS:PROMPT_STEP_ROWS + 8, :]
    st_ref[0] = state[...]


def _ssd_consts(kind):
    tril = jnp.tril(jnp.ones((ROWS, ROWS), F32))
    if kind == "prompt":
        lmat, bones = tril, jnp.ones((ROWS, ROWS), F32)
    else:
        eye = jnp.eye(ROWS // DEC_SEQ, dtype=F32)
        blk = jnp.kron(eye, jnp.ones((DEC_SEQ, DEC_SEQ), F32))
        lmat, bones = tril * blk, blk
    head = jnp.arange(SSD_INNER) // SSD_HEADDIM
    emat = (jnp.arange(LANES)[:, None] == head[None, :]).astype(BF16)
    if kind == "prompt":
        return lmat.astype(BF16), bones.astype(BF16), emat, emat.T, None
    t = jnp.arange(ROWS)[:, None]
    col = jnp.arange(3 * ROWS)[None, :]
    shifts = []
    for s in range(1, SSD_CONV):
        inside = t % DEC_SEQ >= s
        earlier = (col % ROWS == t + DEC_SEQ - s) & (col >= ROWS)
        shifts.append(jnp.where(inside, col == t - s, earlier))
    shift = jnp.concatenate(shifts, axis=0).astype(BF16)
    return lmat.astype(BF16), bones.astype(BF16), emat, emat.T, shift


def _const_spec(shape):
    nd = len(shape)
    return pl.BlockSpec(shape, lambda *_: (0,) * nd)


def _ssd_prompt(xbc, dt, zs, alog, conv_w, conv_b, dskip, norm_g):
    lmat, bones, emat, emat_t, _ = _ssd_consts("prompt")
    tr = PROMPT_STEP_ROWS
    nc = SEQ // tr
    n_blk = N_PROMPT // tr
    row = lambda b, c: (jnp.minimum(b * nc + c, n_blk - 1), 0)
    row_out = lambda b, c: (jnp.where(b < BATCH, b * nc + c, n_blk + jnp.minimum(c, N_SAMPLE // tr - 1)), 0)
    consts = [alog, conv_w, conv_b, lmat, bones, emat, emat_t, dskip, norm_g]
    return pl.pallas_call(
        _ssd_prompt_kernel,
        grid=(BATCH + 1, nc),
        in_specs=[pl.BlockSpec((tr, CONV_DIM), row), pl.BlockSpec((tr, LANES), row),
                  pl.BlockSpec((tr, SSD_INNER), row)] + [_const_spec(a.shape) for a in consts],
        out_specs=[pl.BlockSpec((tr, SSD_INNER), row_out),
                   pl.BlockSpec((1, SSD_INNER, SSD_STATE), lambda b, c: (jnp.minimum(b, BATCH - 1), 0, 0))],
        out_shape=[jax.ShapeDtypeStruct((N_TOK, SSD_INNER), BF16),
                   jax.ShapeDtypeStruct((BATCH, SSD_INNER, SSD_STATE), F32)],
        scratch_shapes=[pltpu.VMEM((tr + 8, CONV_DIM), F32), pltpu.VMEM((SSD_INNER, SSD_STATE), F32)],
        compiler_params=_params(("arbitrary", "arbitrary")), name="ssd_prompt",
    )(xbc, dt, zs, *consts)


SEQ_PER_STEP = 4
Q_ROWS = SEQ_PER_STEP * DEC_SEQ
N_QUARTER = ROWS // Q_ROWS


def _ssd_sample_kernel(*refs, n_alias):
    (xbc_ref, prev_ref, dt_ref, zs_ref, st_in, alog_ref, cw_ref, cb_ref, sh_ref, l_ref, ones_ref, e_ref, et_ref,
     dskip_ref, ng_ref) = refs[:15]
    (y_ref, st_out, ypre_s, eacs_s, cm_s, bm_s, xwt_s, eat_s) = refs[15 + n_alias:]
    q = pl.program_id(1)

    @pl.when(q == 0)
    def _():
        x_cur = xbc_ref[...]
        p_hi, p_lo = _split2(prev_ref[...])
        back = _dot(sh_ref[...], jnp.concatenate([x_cur, p_hi, p_lo], axis=0))
        xc = _conv_silu(lambda s: x_cur.astype(F32) if s == 0 else back[(s - 1) * ROWS:s * ROWS, :],
                        cw_ref[...], cb_ref[...])
        r = _ssd_block(xc, dt_ref[...], alog_ref[...], l_ref[...], ones_ref[...], e_ref[...], dskip_ref[...])
        ypre_s[...] = r["y_pre"]
        eacs_s[...] = r["eacs_x"]
        cm_s[...] = r["cm"].astype(F32)
        bm_s[...] = r["bm"].astype(F32)
        xwt_s[...] = r["xw_t"]
        eat_s[...] = jnp.exp(r["a_tot"].T)

    q0 = pl.multiple_of(q * Q_ROWS, Q_ROWS)
    cq = cm_s[pl.ds(q0, Q_ROWS), :].astype(BF16)
    row_q = lax.broadcasted_iota(jnp.int32, (Q_ROWS, 1), 0) // DEC_SEQ
    row_b = lax.broadcasted_iota(jnp.int32, (ROWS, 1), 0) // DEC_SEQ
    lane_b = lax.broadcasted_iota(jnp.int32, (1, LANES), 1) // DEC_SEQ
    e_atot_t = eat_s[...]
    et = et_ref[...]
    y_off = jnp.zeros((Q_ROWS, SSD_INNER), F32)
    for s in range(SEQ_PER_STEP):
        seq = q * SEQ_PER_STEP + s
        st = st_in[0, s]
        st_b = st.astype(BF16)
        bsel = row_b == seq
        yo, s_new = [], []
        for g in range(SSD_GROUPS):
            ns = slice(g * SSD_STATE, (g + 1) * SSD_STATE)
            gs = slice(g * GROUP_W, (g + 1) * GROUP_W)
            yo.append(_dot_nt(cq[:, ns], st_b[gs, :]))
            bm_g = jnp.where(bsel, bm_s[:, ns], 0.0).astype(BF16)
            s_new.append(_dot(xwt_s[gs, :], bm_g))
        y_off = y_off + jnp.where(row_q == s, jnp.concatenate(yo, axis=1), 0.0)
        dec_col = jnp.sum(jnp.where(lane_b == seq, e_atot_t, 0.0), axis=1, keepdims=True) * (1.0 / DEC_SEQ)
        decay = _dot_exact_lhs(et, jnp.broadcast_to(dec_col, (LANES, SSD_STATE)), pieces=2)
        new_state = decay * st + jnp.concatenate(s_new, axis=0)
        for d in range(st_out.shape[0]):
            st_out[d, s] = new_state

    y = ypre_s[pl.ds(q0, Q_ROWS), :] + y_off * eacs_s[pl.ds(q0, Q_ROWS), :]
    y_ref[...] = _ssd_finish(y, zs_ref[...].astype(F32), ng_ref[...]).astype(y_ref.dtype)


def _ssd_sample(layer, xbc, prev_rows, dt, zs, st_all, y_all, st_out_prev, alog, conv_w, conv_b, dskip, norm_g):
    lmat, bones, emat, emat_t, shift = _ssd_consts("sample")
    blk0 = N_PROMPT // ROWS
    qblk0 = N_PROMPT // Q_ROWS
    consts = [alog, conv_w, conv_b, shift, lmat, bones, emat, emat_t, dskip, norm_g]
    st_spec = pl.BlockSpec((1, SEQ_PER_STEP, SSD_INNER, SSD_STATE), lambda b, q: (layer, b * N_QUARTER + q, 0, 0))
    if st_out_prev is None:
        assert layer == 0
        st_out_spec = pl.BlockSpec((DEPTH, SEQ_PER_STEP, SSD_INNER, SSD_STATE),
                                   lambda b, q: (0, b * N_QUARTER + q, 0, 0))
    else:
        st_out_spec = st_spec
    aliased = [y_all] + ([] if st_out_prev is None else [st_out_prev])
    n_in = 5 + len(consts)
    aliases = {n_in: 0} if st_out_prev is None else {n_in: 0, n_in + 1: 1}
    return pl.pallas_call(
        functools.partial(_ssd_sample_kernel, n_alias=len(aliased)),
        grid=(N_SAMPLE // ROWS, N_QUARTER),
        in_specs=[pl.BlockSpec((ROWS, CONV_DIM), lambda b, q: (blk0 + b, 0)),
                  pl.BlockSpec((ROWS, CONV_DIM), lambda b, q: (b, 0)),
                  pl.BlockSpec((ROWS, LANES), lambda b, q: (blk0 + b, 0)),
                  pl.BlockSpec((Q_ROWS, SSD_INNER), lambda b, q: (qblk0 + b * N_QUARTER + q, 0)),
                  st_spec] + [_const_spec(a.shape) for a in consts]
                 + [pl.BlockSpec(memory_space=pl.ANY) for _ in aliased],
        out_specs=[pl.BlockSpec((Q_ROWS, SSD_INNER), lambda b, q: (qblk0 + b * N_QUARTER + q, 0)), st_out_spec],
        out_shape=[jax.ShapeDtypeStruct((N_TOK, SSD_INNER), BF16),
                   jax.ShapeDtypeStruct((DEPTH, DEC_BATCH, SSD_INNER, SSD_STATE), F32)],
        scratch_shapes=[pltpu.VMEM((ROWS, SSD_INNER), F32), pltpu.VMEM((ROWS, SSD_INNER), F32),
                        pltpu.VMEM((ROWS, SSD_GROUPS * SSD_STATE), F32),
                        pltpu.VMEM((ROWS, SSD_GROUPS * SSD_STATE), F32),
                        pltpu.VMEM((SSD_INNER, ROWS), BF16), pltpu.VMEM((LANES, ROWS), F32)],
        input_output_aliases=aliases,
        compiler_params=_params(("arbitrary", "arbitrary")), name="ssd_sample",
    )(xbc, prev_rows, dt, zs, st_all, *consts, *aliased)


def _xattn_prompt_kernel(q_ref, k_ref, v_ref, o_ref):
    @pl.when(pl.program_id(0) < BATCH)
    def _():
        q = q_ref[...]
        for h in range(XA_HEADS):
            hs = slice(h * XA_HEADDIM, (h + 1) * XA_HEADDIM)
            sc = _dot_nt(q[:, hs], k_ref[0, :, hs])
            p = jnp.exp(sc - jnp.max(sc, axis=-1, keepdims=True))
            p = (p / jnp.sum(p, axis=-1, keepdims=True)).astype(BF16)
            o_ref[:, hs] = _dot(p, v_ref[0, :, hs]).astype(o_ref.dtype)

    @pl.when(pl.program_id(0) == BATCH)
    def _():
        o_ref[...] = jnp.zeros_like(o_ref)


XA_Q_TILE = 512
XA_SAMPLE_SEQS = 4
XA_SAMPLE_ROWS = XA_SAMPLE_SEQS * DEC_SEQ


def _xattn_prompt(layer, q, mem_k, mem_v):
    nq = SEQ // XA_Q_TILE
    n_blk = N_PROMPT // XA_Q_TILE
    kv_spec = pl.BlockSpec((1, N_MEM, XA_WIDTH), lambda b, j: (layer * BATCH + jnp.minimum(b, BATCH - 1), 0, 0))
    row_out = lambda b, j: (jnp.where(b < BATCH, b * nq + j, n_blk + jnp.minimum(j, N_SAMPLE // XA_Q_TILE - 1)), 0)
    return pl.pallas_call(
        _xattn_prompt_kernel,
        grid=(BATCH + 1, nq),
        in_specs=[pl.BlockSpec((XA_Q_TILE, XA_WIDTH), lambda b, j: (jnp.minimum(b * nq + j, n_blk - 1), 0)),
                  kv_spec, kv_spec],
        out_specs=pl.BlockSpec((XA_Q_TILE, XA_WIDTH), row_out),
        out_shape=jax.ShapeDtypeStruct((N_TOK, XA_WIDTH), BF16),
        compiler_params=_params(("arbitrary", "arbitrary")), name="xattn_prompt",
    )(q, mem_k, mem_v)


def _xattn_sample_kernel(q_ref, k_ref, v_ref, y_all, o_ref):
    del y_all
    rows = XA_SAMPLE_ROWS
    q = q_ref[...]
    qblk = jnp.concatenate([q[:, h * XA_HEADDIM:(h + 1) * XA_HEADDIM] for h in range(XA_HEADS)], axis=0)
    mem_head = lax.broadcasted_iota(jnp.int32, (N_MEM * XA_HEADS, 1), 0) % XA_HEADS
    col_head = lax.broadcasted_iota(jnp.int32, (1, XA_HEADS * rows), 1) // rows
    same_head = mem_head == col_head
    row_seq = (lax.broadcasted_iota(jnp.int32, (XA_HEADS * rows, 1), 0) % rows) // DEC_SEQ
    out = jnp.zeros((XA_HEADS * rows, XA_HEADDIM), F32)
    for s in range(XA_SAMPLE_SEQS):
        k2 = k_ref[0, s].reshape(N_MEM * XA_HEADS, XA_HEADDIM).astype(BF16)
        v2 = v_ref[0, s].reshape(N_MEM * XA_HEADS, XA_HEADDIM).astype(BF16)
        sc = jnp.where(same_head, _dot_nt(k2, qblk), -1e30)
        p = jnp.exp(sc - jnp.max(sc, axis=0, keepdims=True))
        p = (p / jnp.sum(p, axis=0, keepdims=True)).astype(BF16)
        y = lax.dot_general(p, v2, (((0,), (0,)), ((), ())), preferred_element_type=F32)
        out = jnp.where(row_seq == s, y, out)
    for h in range(XA_HEADS):
        o_ref[:, h * XA_HEADDIM:(h + 1) * XA_HEADDIM] = out[h * rows:(h + 1) * rows].astype(o_ref.dtype)


def _xattn_sample(layer, q, cache_k, cache_v, y_all):
    rows = XA_SAMPLE_ROWS
    blk0 = N_PROMPT // rows
    kv_spec = pl.BlockSpec((1, XA_SAMPLE_SEQS, N_MEM, XA_HEADS, XA_HEADDIM), lambda j: (layer, j, 0, 0, 0))
    return pl.pallas_call(
        _xattn_sample_kernel,
        grid=(DEC_BATCH // XA_SAMPLE_SEQS,),
        in_specs=[pl.BlockSpec((rows, XA_WIDTH), lambda j: (blk0 + j, 0)), kv_spec, kv_spec,
                  pl.BlockSpec(memory_space=pl.ANY)],
        out_specs=pl.BlockSpec((rows, XA_WIDTH), lambda j: (blk0 + j, 0)),
        out_shape=jax.ShapeDtypeStruct((N_TOK, XA_WIDTH), BF16),
        input_output_aliases={3: 0},
        compiler_params=_params(("parallel",)), name="xattn_sample",
    )(q, cache_k, cache_v, y_all)


MERGE_TILE = 512
MERGE_SUB = 256


def _merge_kernel(ygm, yssd, yxa, gates, x_p, x_s, pgm, pssd, pxa, wout, lng, lnb, o_f, o_b):
    is_prompt = pl.program_id(0) < N_PROMPT // MERGE_TILE
    for r in range(MERGE_TILE // MERGE_SUB):
        rows = slice(r * MERGE_SUB, (r + 1) * MERGE_SUB)
        g = gates[rows, :].astype(F32)
        m = _dot(ygm[rows, :], pgm[...]) * g[:, :D_MODEL]
        m = m + _dot(yssd[rows, :], pssd[...]) * g[:, D_MODEL:2 * D_MODEL]
        m = m + _dot(yxa[rows, :], pxa[...]) * g[:, 2 * D_MODEL:]
        h = _dot(m.astype(BF16), wout[...])
        x = jnp.where(is_prompt, x_p[rows, :], x_s[rows, :])
        y = _layer_norm(ALPHA * x + h, lng[...], lnb[...])
        o_f[rows, :] = y
        o_b[rows, :] = y.astype(BF16)


def _merge(ygm, yssd, yxa, gates, x_p, x_s, x_s_row0, pgm, pssd, pxa, wout, lng, lnb):
    tm = MERGE_TILE
    npt = N_PROMPT // tm
    row = lambda i: (i, 0)
    acts = [ygm, yssd, yxa, gates]
    weights = [pgm, pssd, pxa, wout]
    consts = weights + [lng, lnb]
    return pl.pallas_call(
        _merge_kernel,
        grid=(N_TOK // tm,),
        in_specs=[pl.BlockSpec((tm, a.shape[1]), row) for a in acts]
                 + [pl.BlockSpec((tm, D_MODEL), lambda i: (jnp.minimum(i, npt - 1), 0)),
                    pl.BlockSpec((tm, D_MODEL), lambda i: (x_s_row0 // tm + jnp.maximum(i - npt, 0), 0))]
                 + [_resident_spec(a.shape) for a in weights] + [_const_spec(lng.shape), _const_spec(lnb.shape)],
        out_specs=[pl.BlockSpec((tm, D_MODEL), row), pl.BlockSpec((tm, D_MODEL), row)],
        out_shape=[jax.ShapeDtypeStruct((N_TOK, D_MODEL), F32), jax.ShapeDtypeStruct((N_TOK, D_MODEL), BF16)],
        compiler_params=_params(("parallel",)), name="merge_out_ln",
    )(*acts, x_p, x_s, *consts)


FFN_TILE = 512
FFN_SUB = 256


def _ffn_kernel(xb, xf, wg, wu, wd, lng, lnb, o_f, o_b):
    for r in range(FFN_TILE // FFN_SUB):
        rows = slice(r * FFN_SUB, (r + 1) * FFN_SUB)
        x = xb[rows, :]
        h = (_silu(_dot(x, wg[...])) * _dot(x, wu[...])).astype(BF16)
        y = _layer_norm(ALPHA * xf[rows, :] + _dot(h, wd[...]), lng[...], lnb[...])
        o_f[rows, :] = y
        o_b[rows, :] = y.astype(BF16)


def _resident_spec(shape):
    nd = len(shape)
    return pl.BlockSpec(shape, lambda *_: (0,) * nd, pipeline_mode=pl.Buffered(1))


def _ffn(xb, xf, wg, wu, wd, lng, lnb):
    tm = FFN_TILE
    row = lambda i: (i, 0)
    return pl.pallas_call(
        _ffn_kernel,
        grid=(N_TOK // tm,),
        in_specs=[pl.BlockSpec((tm, D_MODEL), row), pl.BlockSpec((tm, D_MODEL), row),
                  _resident_spec(wg.shape), _resident_spec(wu.shape), _resident_spec(wd.shape),
                  _const_spec(lng.shape), _const_spec(lnb.shape)],
        out_specs=[pl.BlockSpec((tm, D_MODEL), row), pl.BlockSpec((tm, D_MODEL), row)],
        out_shape=[jax.ShapeDtypeStruct((N_TOK, D_MODEL), F32), jax.ShapeDtypeStruct((N_TOK, D_MODEL), BF16)],
        compiler_params=_params(("parallel",)), name="ffn_swiglu_ln",
    )(xb, xf, wg, wu, wd, lng, lnb)


ROUTER_TILE = 1024


def _router_kernel(x_ref, w_ref, b_ref, l_ref, ri_ref, rg_ref, cnt_ref, count):
    @pl.when(pl.program_id(0) == 0)
    def _():
        count[...] = jnp.zeros_like(count)

    xs = _split2(x_ref[...])
    ws = _split2(w_ref[...])
    logits = b_ref[...] + _dot(xs[0], ws[0]) + _dot(xs[0], ws[1]) + _dot(xs[1], ws[0])
    lane = lax.broadcasted_iota(jnp.int32, logits.shape, 1)
    logits = jnp.where(lane < N_EXPERTS, logits, -1e30)
    m1 = jnp.max(logits, axis=-1, keepdims=True)
    i1 = jnp.min(jnp.where(logits == m1, lane, LANES), axis=-1, keepdims=True)
    rest = jnp.where(lane == i1, -1e30, logits)
    m2 = jnp.max(rest, axis=-1, keepdims=True)
    i2 = jnp.min(jnp.where(rest == m2, lane, LANES), axis=-1, keepdims=True)
    e2 = jnp.exp(m2 - m1)
    den = 1.0 + e2

    hit1, hit2 = lane == i1, lane == i2
    assigned = jnp.where(hit1 | hit2, 1.0, 0.0)
    rank = _dot(l_ref[...], assigned.astype(BF16)) + count[...]
    r1 = jnp.sum(jnp.where(hit1, rank, 0.0), axis=-1, keepdims=True).astype(jnp.int32)
    r2 = jnp.sum(jnp.where(hit2, rank, 0.0), axis=-1, keepdims=True).astype(jnp.int32)
    ri_ref[...] = jnp.where(lane == 0, i1, jnp.where(lane == 1, i2, jnp.where(lane == 2, r1,
                            jnp.where(lane == 3, r2, 0))))
    rg_ref[...] = jnp.where(lane == 0, 1.0 / den, jnp.where(lane == 1, e2 / den, 0.0))
    count[...] = count[...] + jnp.sum(assigned, axis=0, keepdims=True)
    cnt_ref[...] = count[...]


def _router(x, w_pad, b_pad):
    tm = ROUTER_TILE
    strict_lower = jnp.tril(jnp.ones((tm, tm), F32), k=-1).astype(BF16)
    return pl.pallas_call(
        _router_kernel,
        grid=(N_TOK // tm,),
        in_specs=[pl.BlockSpec((tm, D_MODEL), lambda i: (i, 0)), _const_spec(w_pad.shape),
                  _const_spec(b_pad.shape), _const_spec(strict_lower.shape)],
        out_specs=[pl.BlockSpec((tm, LANES), lambda i: (i, 0)), pl.BlockSpec((tm, LANES), lambda i: (i, 0)),
                   _const_spec((1, LANES))],
        out_shape=[jax.ShapeDtypeStruct((N_TOK, LANES), jnp.int32), jax.ShapeDtypeStruct((N_TOK, LANES), F32),
                   jax.ShapeDtypeStruct((1, LANES), F32)],
        scratch_shapes=[pltpu.VMEM((1, LANES), F32)],
        compiler_params=_params(("arbitrary",)), name="moe_router",
    )(x, w_pad, b_pad, strict_lower)


EXPERT_TILE = 1024
N_SLOTS = N_TOK * 2 + N_EXPERTS * EXPERT_TILE
N_SLOT_TILES = N_SLOTS // EXPERT_TILE
MOE_TOK_TILE = 1024
MOE_FF_TILE = 512


def _row_copy(src, src_row, dst, dst_row, sem):
    return pltpu.make_async_copy(src.at[pl.ds(src_row, 1)], dst.at[pl.ds(dst_row, 1)], sem)


N_ZERO_TILES = 2 * N_EXPERTS


def _dispatch_kernel(zt_ref, zf_ref, s1_ref, s2_ref, x_ref, xs_out, zeros, sem, zsem):
    def zero_copy(k):
        dst = xs_out.at[pl.ds(pl.multiple_of(zt_ref[k] * EXPERT_TILE, EXPERT_TILE), EXPERT_TILE)]
        return pltpu.make_async_copy(zeros, dst, zsem)

    @pl.when(pl.program_id(0) == 0)
    def _():
        zeros[...] = jnp.zeros_like(zeros)
        for k in range(N_ZERO_TILES):
            @pl.when(zf_ref[k] == 1)
            def _():
                zero_copy(k).start()
        for k in range(N_ZERO_TILES):
            @pl.when(zf_ref[k] == 1)
            def _():
                zero_copy(k).wait()

    def body(r, carry):
        _row_copy(x_ref, r, xs_out, s1_ref[r], sem).start(priority=0)
        _row_copy(x_ref, r, xs_out, s2_ref[r], sem).start(priority=1)
        return carry

    lax.fori_loop(0, MOE_TOK_TILE, body, 0, unroll=8)
    for _ in range(2):
        pltpu.make_async_copy(x_ref, xs_out.at[pl.ds(0, MOE_TOK_TILE)], sem).wait()


def _dispatch(zero_tiles, zero_flags, slot1, slot2, x):
    tm = MOE_TOK_TILE
    smem = lambda: pl.BlockSpec((tm,), lambda i, zt, zf: (i,), memory_space=pltpu.SMEM)
    grid_spec = pltpu.PrefetchScalarGridSpec(
        num_scalar_prefetch=2,
        grid=(N_TOK // tm,),
        in_specs=[smem(), smem(), pl.BlockSpec((tm, D_MODEL), lambda i, zt, zf: (i, 0))],
        out_specs=pl.BlockSpec(memory_space=pl.ANY),
        scratch_shapes=[pltpu.VMEM((EXPERT_TILE, D_MODEL), F32), pltpu.SemaphoreType.DMA(()),
                        pltpu.SemaphoreType.DMA(())],
    )
    return pl.pallas_call(
        _dispatch_kernel, grid_spec=grid_spec,
        out_shape=jax.ShapeDtypeStruct((N_SLOTS, D_MODEL), F32),
        compiler_params=_params(("arbitrary",)), name="moe_dispatch",
    )(zero_tiles, zero_flags, slot1, slot2, x)


EXPERT_ROW_STEP = 256


def _expert_ffn_kernel(te_ref, nv_ref, tr_ref, xs_ref, wg, wu, wd, ys_ref, xb_s):
    del te_ref
    t = pl.program_id(0)
    k = pl.program_id(1)

    @pl.when(t < nv_ref[0])
    def _():
        @pl.when(k == 0)
        def _():
            xb_s[...] = xs_ref[...].astype(BF16)
            ys_ref[...] = jnp.zeros_like(ys_ref)

        wg_b, wu_b, wd_b = wg[0, 0].astype(BF16), wu[0, 0].astype(BF16), wd[0, 0].astype(BF16)
        for m in range(1, EXPERT_TILE // EXPERT_ROW_STEP + 1):
            @pl.when(tr_ref[t] == m)
            def _():
                rows = slice(0, m * EXPERT_ROW_STEP)
                xb = xb_s[rows, :]
                h = _silu(_dot(xb, wg_b)) * _dot(xb, wu_b)
                ys_ref[rows, :] += _dot(h.astype(BF16), wd_b)

    @pl.when((t >= nv_ref[0]) & (k == 0))
    def _():
        ys_ref[...] = jnp.zeros_like(ys_ref)


def _expert_ffn(layer, tile_expert, n_valid, tile_rows, xs, wg, wu, wd):
    tm, tf = EXPERT_TILE, MOE_FF_TILE
    nk = E_FF // tf

    def tile(t, nv):
        return jnp.minimum(t, nv[0] - 1)

    def chunk(t, k, nv):
        return jnp.where(t < nv[0], k, nk - 1)

    grid_spec = pltpu.PrefetchScalarGridSpec(
        num_scalar_prefetch=3,
        grid=(N_SLOT_TILES, nk),
        in_specs=[pl.BlockSpec((tm, D_MODEL), lambda t, k, te, nv, tr: (tile(t, nv), 0)),
                  pl.BlockSpec((1, 1, D_MODEL, tf),
                               lambda t, k, te, nv, tr: (layer, te[tile(t, nv)], 0, chunk(t, k, nv))),
                  pl.BlockSpec((1, 1, D_MODEL, tf),
                               lambda t, k, te, nv, tr: (layer, te[tile(t, nv)], 0, chunk(t, k, nv))),
                  pl.BlockSpec((1, 1, tf, D_MODEL),
                               lambda t, k, te, nv, tr: (layer, te[tile(t, nv)], chunk(t, k, nv), 0))],
        out_specs=pl.BlockSpec((tm, D_MODEL), lambda t, k, te, nv, tr: (t, 0)),
        scratch_shapes=[pltpu.VMEM((tm, D_MODEL), BF16)],
    )
    return pl.pallas_call(
        _expert_ffn_kernel, grid_spec=grid_spec,
        out_shape=jax.ShapeDtypeStruct((N_SLOTS, D_MODEL), F32),
        compiler_params=_params(("arbitrary", "arbitrary")), name="moe_expert_ffn",
    )(tile_expert, n_valid, tile_rows, xs, wg, wu, wd)


def _combine_kernel(s1_ref, s2_ref, s1n_ref, s2n_ref, x_ref, rg_ref, ys_hbm, lng, lnb, o_p, o_s, buf, sem):
    tm = MOE_TOK_TILE
    i = pl.program_id(0)

    def gather(sa_ref, sb_ref, slot):
        def body(r, carry):
            _row_copy(ys_hbm, sa_ref[r], buf.at[slot, 0], r, sem.at[slot]).start(priority=0)
            _row_copy(ys_hbm, sb_ref[r], buf.at[slot, 1], r, sem.at[slot]).start(priority=1)
            return carry

        lax.fori_loop(0, tm, body, 0, unroll=8)

    @pl.when(i == 0)
    def _():
        gather(s1_ref, s2_ref, 0)

    @pl.when(i + 1 < pl.num_programs(0))
    def _():
        gather(s1n_ref, s2n_ref, (i + 1) % 2)

    slot = i % 2
    for j in range(2):
        pltpu.make_async_copy(ys_hbm.at[pl.ds(0, tm)], buf.at[slot, j], sem.at[slot]).wait()

    g = rg_ref[...]
    f = g[:, 0:1] * buf[slot, 0] + g[:, 1:2] * buf[slot, 1]
    y = _layer_norm(ALPHA * x_ref[...] + f, lng[...], lnb[...])

    @pl.when(i < N_PROMPT // tm)
    def _():
        o_p[...] = y

    @pl.when(i >= N_PROMPT // tm)
    def _():
        o_s[...] = y


def _combine(slot1, slot2, x, rg, ys, lng, lnb):
    tm = MOE_TOK_TILE
    npt = N_PROMPT // tm
    n_tiles = N_TOK // tm
    smem = lambda: pl.BlockSpec((tm,), lambda i: (i,), memory_space=pltpu.SMEM)
    smem_next = lambda: pl.BlockSpec((tm,), lambda i: (jnp.minimum(i + 1, n_tiles - 1),), memory_space=pltpu.SMEM)
    return pl.pallas_call(
        _combine_kernel,
        grid=(n_tiles,),
        in_specs=[smem(), smem(), smem_next(), smem_next(), pl.BlockSpec((tm, D_MODEL), lambda i: (i, 0)),
                  pl.BlockSpec((tm, LANES), lambda i: (i, 0)), pl.BlockSpec(memory_space=pl.ANY),
                  _const_spec(lng.shape), _const_spec(lnb.shape)],
        out_specs=[pl.BlockSpec((tm, D_MODEL), lambda i: (jnp.minimum(i, npt - 1), 0)),
                   pl.BlockSpec((tm, D_MODEL), lambda i: (jnp.maximum(i - npt, 0), 0))],
        out_shape=[jax.ShapeDtypeStruct((N_PROMPT, D_MODEL), F32), jax.ShapeDtypeStruct((N_SAMPLE, D_MODEL), F32)],
        scratch_shapes=[pltpu.VMEM((2, 2, tm, D_MODEL), F32), pltpu.SemaphoreType.DMA((2,))],
        compiler_params=_params(("arbitrary",)), name="moe_combine_ln",
    )(slot1, slot2, slot1, slot2, x, rg, ys, lng, lnb)


def _moe(layer, x, router_w, router_b, wg, wu, wd, lng, lnb):
    ri, rg, cnt = _router(x, _pad_lanes(router_w), _pad_lanes(_row(router_b)))
    counts = cnt[0, :N_EXPERTS].astype(jnp.int32)
    padded = (counts + EXPERT_TILE - 1) // EXPERT_TILE * EXPERT_TILE
    ends = jnp.cumsum(padded)
    starts = ends - padded
    slot1 = starts[ri[:, 0]] + ri[:, 2]
    slot2 = starts[ri[:, 1]] + ri[:, 3]
    n_valid = (ends[-1:] // EXPERT_TILE).astype(jnp.int32)
    tile_start = jnp.arange(N_SLOT_TILES, dtype=jnp.int32) * EXPERT_TILE
    tile_expert = jnp.minimum(jnp.sum(tile_start[:, None] >= ends[None, :], axis=1), N_EXPERTS - 1).astype(jnp.int32)

    tail = n_valid[0] + jnp.arange(N_EXPERTS, dtype=jnp.int32)
    zero_tiles = jnp.concatenate([(ends // EXPERT_TILE - 1).astype(jnp.int32), tail])
    zero_flags = jnp.concatenate([padded > 0, tail < N_SLOT_TILES]).astype(jnp.int32)
    zero_tiles = jnp.where(zero_flags == 1, zero_tiles, 0)

    xs = _dispatch(zero_tiles, zero_flags, slot1, slot2, x)
    rows_left = counts[tile_expert] - (tile_start - starts[tile_expert])
    tile_rows = (jnp.clip(rows_left, 1, EXPERT_TILE) + EXPERT_ROW_STEP - 1) // EXPERT_ROW_STEP
    ys = _expert_ffn(layer, tile_expert, n_valid, tile_rows.astype(jnp.int32), xs, wg, wu, wd)
    return _combine(slot1, slot2, x, rg, ys, lng, lnb)


def _row(a):
    return a.reshape(1, -1).astype(F32)


def _pad_lanes(a):
    return jnp.pad(a, ((0, 0), (0, LANES - a.shape[1])))


def _gmlp_weights(w_s, b_s):
    tril = jnp.tril(jnp.ones((GM_CHUNK, GM_CHUNK), dtype=bool))
    w_p = jnp.where(tril, w_s, 0.0)
    n_seq = ROWS // DEC_SEQ
    w_8 = w_p[:, :DEC_SEQ, :DEC_SEQ]
    w_d = jnp.einsum("ab,gij->gaibj", jnp.eye(n_seq, dtype=F32), w_8).reshape(GM_GROUPS, ROWS, ROWS)
    bias_p = jnp.repeat(b_s.T, GM_GROUP, axis=1)
    bias_d = jnp.tile(bias_p[:DEC_SEQ], (n_seq, 1))
    return jnp.stack([w_p, w_d]).astype(BF16), jnp.stack([bias_p, bias_d])


def kernel(x_prompt, x_sample, mem_prompt, cache_mem_k, cache_mem_v, state_conv, state_ssm, w_in, conv_w, conv_b, dt_bias, a_log, d_skip, ssd_norm_g, v_ln_g, v_ln_b, w_s, b_s, p_gm, p_ssd, p_xa, w_out, w_mem_k, w_mem_v, ln1_g, ln1_b, ln2_g, ln2_b, ffn_wg, ffn_wu, ffn_wd, router_w, router_b, moe_wg, moe_wu, moe_wd):
    assert DEPTH % 2 == 0
    x_p = x_prompt.reshape(N_PROMPT, D_MODEL)
    x_s = x_sample.reshape(N_SAMPLE, D_MODEL)
    x_s_row0 = 0
    xb = jnp.concatenate([x_p.astype(BF16), x_s.astype(BF16)], axis=0)
    mem_b = mem_prompt.reshape(BATCH * N_MEM, D_MODEL).astype(BF16)
    w_in_t = jnp.swapaxes(w_in, 1, 2)
    st_all = state_ssm.reshape(DEPTH, DEC_BATCH, SSD_INNER, SSD_STATE)

    mem_k_out, mem_v_out, mem_kb, mem_vb = _mem_kv(mem_b, w_mem_k, w_mem_v)
    mem_kb = mem_kb.reshape(DEPTH * BATCH, N_MEM, XA_WIDTH)
    mem_vb = mem_vb.reshape(DEPTH * BATCH, N_MEM, XA_WIDTH)

    ssm_p_out, conv_p_out, conv_s_out, v_out = [], [], [], []
    ssm_s = None
    for i in range(DEPTH):
        (u,) = _mm(xb, w_in_t, i, 0, GM_WIDTH, 1024, [], _epi_gelu, [BF16], "in_u")
        (v,) = _mm(xb, w_in_t, i, GM_WIDTH, GM_WIDTH, 1024, [_row(v_ln_g[i]), _row(v_ln_b[i])], _epi_gelu_ln, [F32],
                   "in_v", tm=1024)
        (zs,) = _mm(xb, w_in_t, i, OFF_Z, SSD_INNER, 1024, [], _epi_silu, [BF16], "in_z")
        (xbc,) = _mm(xb, w_in_t, i, OFF_XBC, CONV_DIM, 1024, [], _epi_id, [BF16], "in_xbc")
        (dt,) = _mm(xb, w_in_t, i, OFF_DT, LANES, LANES, [_pad_lanes(_row(dt_bias[i]))], _epi_softplus, [F32], "in_dt")
        (q,) = _mm(xb, w_in_t, i, OFF_Q, XA_WIDTH, 1024, [], _epi_qscale, [BF16], "in_q")
        (gates,) = _mm(xb, w_in_t, i, OFF_GATE, N_BRANCH * D_MODEL, 1024, [], _epi_sigmoid, [BF16], "in_gates")

        gm_w, gm_b = _gmlp_weights(w_s[i], b_s[i])
        y_gm = _gmlp(v, u, gm_w, gm_b)

        alog = _pad_lanes(_row(a_log[i]))
        dskip = _row(jnp.repeat(d_skip[i], SSD_HEADDIM))
        ssd_args = (alog, conv_w[i], _row(conv_b[i]), dskip, _row(ssd_norm_g[i]))
        y_ssd, ssm_p = _ssd_prompt(xbc, dt, zs, *ssd_args)
        prev_rows = jnp.pad(state_conv[i], ((0, 0), (DEC_SEQ - (SSD_CONV - 1), 0), (0, 0))).reshape(N_SAMPLE, CONV_DIM)
        y_ssd, ssm_s = _ssd_sample(i, xbc, prev_rows, dt, zs, st_all, y_ssd, ssm_s, *ssd_args)
        ssm_p_out.append(ssm_p.reshape(BATCH, SSD_HEADS, SSD_HEADDIM, SSD_STATE))
        slots = xbc.reshape(N_TOK // DEC_SEQ, DEC_SEQ, CONV_DIM)
        n_p, per_seq, keep = N_PROMPT // DEC_SEQ, SEQ // DEC_SEQ, SSD_CONV - 1
        conv_p_out.append(lax.slice(slots, (per_seq - 1, DEC_SEQ - keep, 0), (n_p, DEC_SEQ, CONV_DIM),
                                    (per_seq, 1, 1)).astype(F32))
        conv_s_out.append(lax.slice(slots, (n_p, DEC_SEQ - keep, 0), slots.shape).astype(F32))
        v_out.append(v[N_PROMPT:].reshape(DEC_BATCH, DEC_SEQ, GM_WIDTH))

        y_xa = _xattn_prompt(i, q, mem_kb, mem_vb)
        y_xa = _xattn_sample(i, q, cache_mem_k, cache_mem_v, y_xa)

        x, xb = _merge(y_gm, y_ssd, y_xa, gates, x_p, x_s, x_s_row0, p_gm[i].astype(BF16), p_ssd[i].astype(BF16),
                       p_xa[i].astype(BF16), w_out[i].astype(BF16), _row(ln1_g[i]), _row(ln1_b[i]))

        j = i // 2
        if i % 2 == 0:
            x, xb = _ffn(xb, x, ffn_wg[j].astype(BF16), ffn_wu[j].astype(BF16), ffn_wd[j].astype(BF16),
                         _row(ln2_g[i]), _row(ln2_b[i]))
            x_p, x_s, x_s_row0 = x, x, N_PROMPT
        else:
            x_p, x_s = _moe(j, x, router_w[j], router_b[j], moe_wg, moe_wu, moe_wd, _row(ln2_g[i]), _row(ln2_b[i]))
            x_s_row0 = 0
            if i + 1 < DEPTH:
                xb = jnp.concatenate([x_p.astype(BF16), x_s.astype(BF16)], axis=0)

    y_prompt = x_p.reshape(BATCH, SEQ, D_MODEL)
    y_sample = x_s.reshape(DEC_BATCH, DEC_SEQ, D_MODEL)
    ssm_s_out = ssm_s.reshape(DEPTH, DEC_BATCH, SSD_HEADS, SSD_HEADDIM, SSD_STATE)
    return (y_prompt, y_sample, mem_k_out, mem_v_out, jnp.stack(conv_p_out),
            jnp.stack(ssm_p_out), jnp.stack(conv_s_out), ssm_s_out, jnp.stack(v_out))
```

```python
import functools

import jax
import jax.numpy as jnp
from jax import lax
from jax.experimental import pallas as pl
from jax.experimental.pallas import tpu as pltpu

F32 = jnp.float32
BF16 = jnp.bfloat16

D_MODEL = 1024
BATCH = 8
SEQ = 2048
DEPTH = 2
DEC_BATCH = 128
DEC_SEQ = 8
N_MEM = 256
GM_WIDTH = D_MODEL
GM_CHUNK = 128
GM_GROUP = 128
GM_GROUPS = GM_WIDTH // GM_GROUP
SSD_INNER = 2 * D_MODEL
SSD_HEADDIM = 64
SSD_HEADS = SSD_INNER // SSD_HEADDIM
SSD_STATE = 128
SSD_GROUPS = 4
SSD_HPG = SSD_HEADS // SSD_GROUPS
SSD_CONV = 4
SSD_CHUNK = 128
CONV_DIM = SSD_INNER + 2 * SSD_GROUPS * SSD_STATE
XA_HEADS = 4
XA_HEADDIM = D_MODEL // XA_HEADS
XA_WIDTH = XA_HEADS * XA_HEADDIM
N_BRANCH = 3
D_FF = ((8 * D_MODEL // 3 + 127) // 128) * 128
N_EXPERTS = 8
E_FF = 7 * D_MODEL // 2
ALPHA = (2 * DEPTH) ** 0.25
LN_EPS = 1e-5

N_PROMPT = BATCH * SEQ
N_SAMPLE = DEC_BATCH * DEC_SEQ
N_TOK = N_PROMPT + N_SAMPLE

OFF_Z = 2 * GM_WIDTH
OFF_XBC = OFF_Z + SSD_INNER
OFF_DT = OFF_XBC + CONV_DIM
OFF_Q = OFF_DT + SSD_HEADS
OFF_GATE = OFF_Q + XA_WIDTH

LANES = 128
ROWS = 128
GROUP_W = SSD_HPG * SSD_HEADDIM
VMEM_LIMIT = 56 * 1024 * 1024


def _params(sem):
    return pltpu.CompilerParams(dimension_semantics=sem, vmem_limit_bytes=VMEM_LIMIT)


def _dot(a, b):
    return jnp.dot(a, b, preferred_element_type=F32)


def _dot_nt(a, b):
    return lax.dot_general(a, b, (((1,), (1,)), ((), ())), preferred_element_type=F32)


def _split2(x):
    hi = x.astype(BF16)
    lo = (x - hi.astype(F32)).astype(BF16)
    return hi, lo


def _split3(x):
    hi = x.astype(BF16)
    r = x - hi.astype(F32)
    mid = r.astype(BF16)
    lo = (r - mid.astype(F32)).astype(BF16)
    return hi, mid, lo


def _dot_exact_lhs(m_bf16, x, pieces=3):
    parts = _split3(x) if pieces == 3 else _split2(x)
    n = x.shape[1]
    out = _dot(m_bf16, jnp.concatenate(parts, axis=1))
    return sum(out[:, i * n:(i + 1) * n] for i in range(1, pieces)) + out[:, :n]


def _expand_rhs2(xs, m_bf16):
    rows = xs[0].shape[0]
    out = _dot(jnp.concatenate([p for x in xs for p in _split2(x)], axis=0), m_bf16)
    return [out[2 * i * rows:(2 * i + 1) * rows] + out[(2 * i + 1) * rows:(2 * i + 2) * rows] for i in range(len(xs))]


def _layer_norm(r, g, b):
    mu = jnp.mean(r, axis=-1, keepdims=True)
    c = r - mu
    var = jnp.mean(c * c, axis=-1, keepdims=True)
    return c * lax.rsqrt(var + LN_EPS) * g + b


def _gelu_tanh(x):
    return x * (0.5 * (1.0 + jnp.tanh(0.7978845608028654 * (x + 0.044715 * (x * x * x)))))


def _sigmoid(x):
    return 1.0 / (1.0 + jnp.exp(-x))


def _silu(x):
    return x * _sigmoid(x)


def _softplus(x):
    return jnp.maximum(x, 0.0) + jnp.log(1.0 + jnp.exp(-jnp.abs(x)))


def _mm_kernel(x_ref, wt_ref, *rest, epilogue, n_extra, n_out):
    extras = [r[...] for r in rest[:n_extra]]
    outs = rest[n_extra:n_extra + n_out]
    w_bf16 = rest[n_extra + n_out]

    @pl.when(pl.program_id(1) == 0)
    def _():
        w_bf16[...] = wt_ref[0].T.astype(BF16)

    acc = _dot(x_ref[...], w_bf16[...])
    res = epilogue(acc, *extras)
    for o, r in zip(outs, res):
        o[...] = r.astype(o.dtype)


MM_TILE = 2176


def _mm(x, wt, layer, row0, n_cols, tn, extras, epilogue, out_dtypes, name, tm=MM_TILE):
    t, k = x.shape
    assert row0 % 8 == 0
    grid = (n_cols // tn, t // tm)
    in_specs = [pl.BlockSpec((tm, k), lambda j, i: (i, 0)),
                pl.BlockSpec((pl.Element(1), pl.Element(tn), pl.Element(k)),
                             lambda j, i: (layer, pl.multiple_of(row0 + j * tn, 8), 0))]
    in_specs += [pl.BlockSpec((1, tn), lambda j, i: (0, j)) for _ in extras]
    out_specs = [pl.BlockSpec((tm, tn), lambda j, i: (i, j)) for _ in out_dtypes]
    out_shape = [jax.ShapeDtypeStruct((t, n_cols), d) for d in out_dtypes]
    return pl.pallas_call(
        functools.partial(_mm_kernel, epilogue=epilogue, n_extra=len(extras), n_out=len(out_dtypes)),
        grid=grid, in_specs=in_specs, out_specs=out_specs, out_shape=out_shape,
        scratch_shapes=[pltpu.VMEM((k, tn), BF16)],
        compiler_params=_params(("parallel", "arbitrary")), name=name,
    )(x, wt, *extras)


MEM_TILE_SEQS = 4


def _mem_kv_kernel(m_ref, wk_ref, wv_ref, k5_ref, v5_ref, kb_ref, vb_ref):
    m = m_ref[...]
    for w_ref, o5_ref, ob_ref in ((wk_ref, k5_ref, kb_ref), (wv_ref, v5_ref, vb_ref)):
        acc = _dot(m, w_ref[0].astype(BF16))
        ob_ref[0] = acc.astype(BF16)
        for b in range(MEM_TILE_SEQS):
            for h in range(XA_HEADS):
                o5_ref[0, b, :, h, :] = acc[b * N_MEM:(b + 1) * N_MEM, h * XA_HEADDIM:(h + 1) * XA_HEADDIM]


def _mem_kv(mem_b, w_mem_k, w_mem_v):
    tm = MEM_TILE_SEQS * N_MEM
    n5 = (DEPTH, BATCH, N_MEM, XA_HEADS, XA_HEADDIM)
    w_spec = pl.BlockSpec((1, D_MODEL, XA_WIDTH), lambda l, r: (l, 0, 0))
    o5_spec = pl.BlockSpec((1, MEM_TILE_SEQS, N_MEM, XA_HEADS, XA_HEADDIM), lambda l, r: (l, r, 0, 0, 0))
    ob_spec = pl.BlockSpec((1, tm, XA_WIDTH), lambda l, r: (l, r, 0))
    return pl.pallas_call(
        _mem_kv_kernel,
        grid=(DEPTH, BATCH // MEM_TILE_SEQS),
        in_specs=[pl.BlockSpec((tm, D_MODEL), lambda l, r: (r, 0)), w_spec, w_spec],
        out_specs=[o5_spec, o5_spec, ob_spec, ob_spec],
        out_shape=[jax.ShapeDtypeStruct(n5, F32), jax.ShapeDtypeStruct(n5, F32),
                   jax.ShapeDtypeStruct((DEPTH, BATCH * N_MEM, XA_WIDTH), BF16),
                   jax.ShapeDtypeStruct((DEPTH, BATCH * N_MEM, XA_WIDTH), BF16)],
        compiler_params=_params(("parallel", "parallel")), name="mem_kv",
    )(mem_b, w_mem_k, w_mem_v)


def _epi_gelu(acc):
    return (_gelu_tanh(acc),)


def _epi_gelu_ln(acc, g, b):
    return (_layer_norm(_gelu_tanh(acc), g, b),)


def _epi_silu(acc):
    return (_silu(acc),)


def _epi_id(acc):
    return (acc,)


def _epi_softplus(acc, bias):
    return (_softplus(acc + bias),)


def _epi_qscale(acc):
    return (acc * (XA_HEADDIM ** -0.5),)


def _epi_sigmoid(acc):
    return (_sigmoid(acc),)


GM_TILE = 512


def _gmlp_kernel(v_ref, u_ref, w_ref, b_ref, o_ref):
    for c in range(GM_TILE // ROWS):
        rs = slice(c * ROWS, (c + 1) * ROWS)
        for g in range(GM_GROUPS):
            cs = slice(g * GM_GROUP, (g + 1) * GM_GROUP)
            z = _dot(w_ref[0, g], v_ref[rs, cs].astype(BF16)) + b_ref[0, :, cs]
            o_ref[rs, cs] = (u_ref[rs, cs].astype(F32) * z).astype(o_ref.dtype)


def _gmlp(v, u, w2, b2):
    n_prompt_tiles = N_PROMPT // GM_TILE

    def sel(i):
        return jnp.where(i >= n_prompt_tiles, 1, 0)

    return pl.pallas_call(
        _gmlp_kernel,
        grid=(N_TOK // GM_TILE,),
        in_specs=[pl.BlockSpec((GM_TILE, GM_WIDTH), lambda i: (i, 0)),
                  pl.BlockSpec((GM_TILE, GM_WIDTH), lambda i: (i, 0)),
                  pl.BlockSpec((1, GM_GROUPS, ROWS, ROWS), lambda i: (sel(i), 0, 0, 0)),
                  pl.BlockSpec((1, ROWS, GM_WIDTH), lambda i: (sel(i), 0, 0))],
        out_specs=pl.BlockSpec((GM_TILE, GM_WIDTH), lambda i: (i, 0)),
        out_shape=jax.ShapeDtypeStruct((N_TOK, GM_WIDTH), BF16),
        compiler_params=_params(("parallel",)), name="gmlp_spatial",
    )(v, u, w2, b2)


def _conv_silu(x_back, conv_w, conv_b):
    conv = conv_b + x_back(0) * conv_w[3:4, :]
    for s in range(1, SSD_CONV):
        conv = conv + x_back(s) * conv_w[3 - s:4 - s, :]
    return _silu(conv)


def _ssd_block(xc, dt, alog, lmat, bones, emat, dskip):
    xs = xc[:, :SSD_INNER].astype(F32)
    bm = xc[:, SSD_INNER:SSD_INNER + SSD_GROUPS * SSD_STATE].astype(BF16)
    cm = xc[:, SSD_INNER + SSD_GROUPS * SSD_STATE:].astype(BF16)

    lane = lax.broadcasted_iota(jnp.int32, (1, LANES), 1)
    a_neg = jnp.where(lane < SSD_HEADS, -jnp.exp(alog), 0.0)
    adt = dt * a_neg
    sums = _dot_exact_lhs(jnp.concatenate([lmat, bones], axis=0), adt)
    a_cs, a_tot = sums[:ROWS], sums[ROWS:]
    a_cs_t = a_cs.T
    mask = lmat.astype(F32) > 0.5

    dt_x, eacs_x, te_x = _expand_rhs2([dt, jnp.exp(a_cs), jnp.exp(a_tot - a_cs)], emat)
    xdt = xs * dt_x
    xw_t = (xdt * te_x).T.astype(BF16)

    lane_r = lax.broadcasted_iota(jnp.int32, (ROWS, LANES), 1)
    lo_half = lane_r < SSD_HEADDIM
    yd = []
    for g in range(SSD_GROUPS):
        ns = slice(g * SSD_STATE, (g + 1) * SSD_STATE)
        cb = _dot_nt(cm[:, ns], bm[:, ns])
        for hp in range(SSD_HPG // 2):
            h0 = g * SSD_HPG + 2 * hp
            ms = []
            for h in (h0, h0 + 1):
                seg = a_cs[:, h:h + 1] - a_cs_t[h:h + 1, :]
                ms.append(cb * jnp.exp(jnp.where(mask, seg, -1e30)))
            lhs = jnp.concatenate(ms, axis=1).astype(BF16)
            xp = xdt[:, h0 * SSD_HEADDIM:(h0 + 2) * SSD_HEADDIM]
            rhs = jnp.concatenate([jnp.where(lo_half, xp, 0.0), jnp.where(lo_half, 0.0, xp)],
                                  axis=0).astype(BF16)
            yd.append(_dot(lhs, rhs))
    y_pre = jnp.concatenate(yd, axis=1) + xs * dskip
    return dict(y_pre=y_pre, eacs_x=eacs_x, cm=cm, bm=bm, xw_t=xw_t, a_tot=a_tot)


def _ssd_finish(y, zs, norm_g):
    y = y * zs
    outs = []
    for g in range(SSD_GROUPS):
        yg = y[:, g * GROUP_W:(g + 1) * GROUP_W]
        ms = jnp.mean(yg * yg, axis=-1, keepdims=True)
        outs.append(yg * lax.rsqrt(ms + LN_EPS))
    return jnp.concatenate(outs, axis=1) * norm_g


PROMPT_STEP_ROWS = 2 * ROWS


def _ssd_prompt_kernel(*refs):
    y_ref = refs[12]

    @pl.when(pl.program_id(0) < BATCH)
    def _():
        _ssd_prompt_body(*refs)

    @pl.when(pl.program_id(0) == BATCH)
    def _():
        y_ref[...] = jnp.zeros_like(y_ref)


def _ssd_prompt_body(xbc_ref, dt_ref, zs_ref, alog_ref, cw_ref, cb_ref, l_ref, ones_ref, e_ref, et_ref, dskip_ref,
                     ng_ref, y_ref, st_ref, xpad, state):
    @pl.when(pl.program_id(1) == 0)
    def _():
        xpad[0:8, :] = jnp.zeros((8, CONV_DIM), F32)
        state[...] = jnp.zeros_like(state)

    xpad[8:8 + PROMPT_STEP_ROWS, :] = xbc_ref[...].astype(F32)
    for j in range(PROMPT_STEP_ROWS // ROWS):
        rows = slice(j * ROWS, (j + 1) * ROWS)
        base = 8 + j * ROWS
        xc = _conv_silu(lambda s: xpad[base - s:base - s + ROWS, :], cw_ref[...], cb_ref[...])
        r = _ssd_block(xc, dt_ref[rows, :], alog_ref[...], l_ref[...], ones_ref[...], e_ref[...], dskip_ref[...])

        st = state[...]
        st_b = st.astype(BF16)
        y_off, s_new = [], []
        for g in range(SSD_GROUPS):
            ns = slice(g * SSD_STATE, (g + 1) * SSD_STATE)
            gs = slice(g * GROUP_W, (g + 1) * GROUP_W)
            y_off.append(_dot_nt(r["cm"][:, ns], st_b[gs, :]))
            s_new.append(_dot(r["xw_t"][gs, :], r["bm"][:, ns]))
        y = r["y_pre"] + jnp.concatenate(y_off, axis=1) * r["eacs_x"]
        y_ref[rows, :] = _ssd_finish(y, zs_ref[rows, :].astype(F32), ng_ref[...]).astype(y_ref.dtype)

        decay = _dot_exact_lhs(et_ref[...], jnp.exp(r["a_tot"].T), pieces=2)
        state[...] = decay * st + jnp.concatenate(s_new, axis=0)
    xpad[0:8, :] = xpad[PROMPT_STEP_ROWS:PROMPT_STEP_ROWS + 8, :]
    st_ref[0] = state[...]


def _ssd_consts(kind):
    tril = jnp.tril(jnp.ones((ROWS, ROWS), F32))
    if kind == "prompt":
        lmat, bones = tril, jnp.ones((ROWS, ROWS), F32)
    else:
        eye = jnp.eye(ROWS // DEC_SEQ, dtype=F32)
        blk = jnp.kron(eye, jnp.ones((DEC_SEQ, DEC_SEQ), F32))
        lmat, bones = tril * blk, blk
    head = jnp.arange(SSD_INNER) // SSD_HEADDIM
    emat = (jnp.arange(LANES)[:, None] == head[None, :]).astype(BF16)
    if kind == "prompt":
        return lmat.astype(BF16), bones.astype(BF16), emat, emat.T, None
    t = jnp.arange(ROWS)[:, None]
    col = jnp.arange(3 * ROWS)[None, :]
    shifts = []
    for s in range(1, SSD_CONV):
        inside = t % DEC_SEQ >= s
        earlier = (col % ROWS == t + DEC_SEQ - s) & (col >= ROWS)
        shifts.append(jnp.where(inside, col == t - s, earlier))
    shift = jnp.concatenate(shifts, axis=0).astype(BF16)
    return lmat.astype(BF16), bones.astype(BF16), emat, emat.T, shift


def _const_spec(shape):
    nd = len(shape)
    return pl.BlockSpec(shape, lambda *_: (0,) * nd)


def _ssd_prompt(xbc, dt, zs, alog, conv_w, conv_b, dskip, norm_g):
    lmat, bones, emat, emat_t, _ = _ssd_consts("prompt")
    tr = PROMPT_STEP_ROWS
    nc = SEQ // tr
    n_blk = N_PROMPT // tr
    row = lambda b, c: (jnp.minimum(b * nc + c, n_blk - 1), 0)
    row_out = lambda b, c: (jnp.where(b < BATCH, b * nc + c, n_blk + jnp.minimum(c, N_SAMPLE // tr - 1)), 0)
    consts = [alog, conv_w, conv_b, lmat, bones, emat, emat_t, dskip, norm_g]
    return pl.pallas_call(
        _ssd_prompt_kernel,
        grid=(BATCH + 1, nc),
        in_specs=[pl.BlockSpec((tr, CONV_DIM), row), pl.BlockSpec((tr, LANES), row),
                  pl.BlockSpec((tr, SSD_INNER), row)] + [_const_spec(a.shape) for a in consts],
        out_specs=[pl.BlockSpec((tr, SSD_INNER), row_out),
                   pl.BlockSpec((1, SSD_INNER, SSD_STATE), lambda b, c: (jnp.minimum(b, BATCH - 1), 0, 0))],
        out_shape=[jax.ShapeDtypeStruct((N_TOK, SSD_INNER), BF16),
                   jax.ShapeDtypeStruct((BATCH, SSD_INNER, SSD_STATE), F32)],
        scratch_shapes=[pltpu.VMEM((tr + 8, CONV_DIM), F32), pltpu.VMEM((SSD_INNER, SSD_STATE), F32)],
        compiler_params=_params(("arbitrary", "arbitrary")), name="ssd_prompt",
    )(xbc, dt, zs, *consts)


SEQ_PER_STEP = 4
Q_ROWS = SEQ_PER_STEP * DEC_SEQ
N_QUARTER = ROWS // Q_ROWS


def _ssd_sample_kernel(*refs, n_alias):
    (xbc_ref, prev_ref, dt_ref, zs_ref, st_in, alog_ref, cw_ref, cb_ref, sh_ref, l_ref, ones_ref, e_ref, et_ref,
     dskip_ref, ng_ref) = refs[:15]
    (y_ref, st_out, ypre_s, eacs_s, cm_s, bm_s, xwt_s, eat_s) = refs[15 + n_alias:]
    q = pl.program_id(1)

    @pl.when(q == 0)
    def _():
        x_cur = xbc_ref[...]
        p_hi, p_lo = _split2(prev_ref[...])
        back = _dot(sh_ref[...], jnp.concatenate([x_cur, p_hi, p_lo], axis=0))
        xc = _conv_silu(lambda s: x_cur.astype(F32) if s == 0 else back[(s - 1) * ROWS:s * ROWS, :],
                        cw_ref[...], cb_ref[...])
        r = _ssd_block(xc, dt_ref[...], alog_ref[...], l_ref[...], ones_ref[...], e_ref[...], dskip_ref[...])
        ypre_s[...] = r["y_pre"]
        eacs_s[...] = r["eacs_x"]
        cm_s[...] = r["cm"].astype(F32)
        bm_s[...] = r["bm"].astype(F32)
        xwt_s[...] = r["xw_t"]
        eat_s[...] = jnp.exp(r["a_tot"].T)

    q0 = pl.multiple_of(q * Q_ROWS, Q_ROWS)
    cq = cm_s[pl.ds(q0, Q_ROWS), :].astype(BF16)
    row_q = lax.broadcasted_iota(jnp.int32, (Q_ROWS, 1), 0) // DEC_SEQ
    row_b = lax.broadcasted_iota(jnp.int32, (ROWS, 1), 0) // DEC_SEQ
    lane_b = lax.broadcasted_iota(jnp.int32, (1, LANES), 1) // DEC_SEQ
    e_atot_t = eat_s[...]
    et = et_ref[...]
    y_off = jnp.zeros((Q_ROWS, SSD_INNER), F32)
    for s in range(SEQ_PER_STEP):
        seq = q * SEQ_PER_STEP + s
        st = st_in[0, s]
        st_b = st.astype(BF16)
        bsel = row_b == seq
        yo, s_new = [], []
        for g in range(SSD_GROUPS):
            ns = slice(g * SSD_STATE, (g + 1) * SSD_STATE)
            gs = slice(g * GROUP_W, (g + 1) * GROUP_W)
            yo.append(_dot_nt(cq[:, ns], st_b[gs, :]))
            bm_g = jnp.where(bsel, bm_s[:, ns], 0.0).astype(BF16)
            s_new.append(_dot(xwt_s[gs, :], bm_g))
        y_off = y_off + jnp.where(row_q == s, jnp.concatenate(yo, axis=1), 0.0)
        dec_col = jnp.sum(jnp.where(lane_b == seq, e_atot_t, 0.0), axis=1, keepdims=True) * (1.0 / DEC_SEQ)
        decay = _dot_exact_lhs(et, jnp.broadcast_to(dec_col, (LANES, SSD_STATE)), pieces=2)
        new_state = decay * st + jnp.concatenate(s_new, axis=0)
        for d in range(st_out.shape[0]):
            st_out[d, s] = new_state

    y = ypre_s[pl.ds(q0, Q_ROWS), :] + y_off * eacs_s[pl.ds(q0, Q_ROWS), :]
    y_ref[...] = _ssd_finish(y, zs_ref[...].astype(F32), ng_ref[...]).astype(y_ref.dtype)


def _ssd_sample(layer, xbc, prev_rows, dt, zs, st_all, y_all, st_out_prev, alog, conv_w, conv_b, dskip, norm_g):
    lmat, bones, emat, emat_t, shift = _ssd_consts("sample")
    blk0 = N_PROMPT // ROWS
    qblk0 = N_PROMPT // Q_ROWS
    consts = [alog, conv_w, conv_b, shift, lmat, bones, emat, emat_t, dskip, norm_g]
    st_spec = pl.BlockSpec((1, SEQ_PER_STEP, SSD_INNER, SSD_STATE), lambda b, q: (layer, b * N_QUARTER + q, 0, 0))
    if st_out_prev is None:
        assert layer == 0
        st_out_spec = pl.BlockSpec((DEPTH, SEQ_PER_STEP, SSD_INNER, SSD_STATE),
                                   lambda b, q: (0, b * N_QUARTER + q, 0, 0))
    else:
        st_out_spec = st_spec
    aliased = [y_all] + ([] if st_out_prev is None else [st_out_prev])
    n_in = 5 + len(consts)
    aliases = {n_in: 0} if st_out_prev is None else {n_in: 0, n_in + 1: 1}
    return pl.pallas_call(
        functools.partial(_ssd_sample_kernel, n_alias=len(aliased)),
        grid=(N_SAMPLE // ROWS, N_QUARTER),
        in_specs=[pl.BlockSpec((ROWS, CONV_DIM), lambda b, q: (blk0 + b, 0)),
                  pl.BlockSpec((ROWS, CONV_DIM), lambda b, q: (b, 0)),
                  pl.BlockSpec((ROWS, LANES), lambda b, q: (blk0 + b, 0)),
                  pl.BlockSpec((Q_ROWS, SSD_INNER), lambda b, q: (qblk0 + b * N_QUARTER + q, 0)),
                  st_spec] + [_const_spec(a.shape) for a in consts]
                 + [pl.BlockSpec(memory_space=pl.ANY) for _ in aliased],
        out_specs=[pl.BlockSpec((Q_ROWS, SSD_INNER), lambda b, q: (qblk0 + b * N_QUARTER + q, 0)), st_out_spec],
        out_shape=[jax.ShapeDtypeStruct((N_TOK, SSD_INNER), BF16),
                   jax.ShapeDtypeStruct((DEPTH, DEC_BATCH, SSD_INNER, SSD_STATE), F32)],
        scratch_shapes=[pltpu.VMEM((ROWS, SSD_INNER), F32), pltpu.VMEM((ROWS, SSD_INNER), F32),
                        pltpu.VMEM((ROWS, SSD_GROUPS * SSD_STATE), F32),
                        pltpu.VMEM((ROWS, SSD_GROUPS * SSD_STATE), F32),
                        pltpu.VMEM((SSD_INNER, ROWS), BF16), pltpu.VMEM((LANES, ROWS), F32)],
        input_output_aliases=aliases,
        compiler_params=_params(("arbitrary", "arbitrary")), name="ssd_sample",
    )(xbc, prev_rows, dt, zs, st_all, *consts, *aliased)


def _xattn_prompt_kernel(q_ref, k_ref, v_ref, o_ref):
    @pl.when(pl.program_id(0) < BATCH)
    def _():
        q = q_ref[...]
        for h in range(XA_HEADS):
            hs = slice(h * XA_HEADDIM, (h + 1) * XA_HEADDIM)
            sc = _dot_nt(q[:, hs], k_ref[0, :, hs])
            p = jnp.exp(sc - jnp.max(sc, axis=-1, keepdims=True))
            p = (p / jnp.sum(p, axis=-1, keepdims=True)).astype(BF16)
            o_ref[:, hs] = _dot(p, v_ref[0, :, hs]).astype(o_ref.dtype)

    @pl.when(pl.program_id(0) == BATCH)
    def _():
        o_ref[...] = jnp.zeros_like(o_ref)


XA_Q_TILE = 512
XA_SAMPLE_SEQS = 4
XA_SAMPLE_ROWS = XA_SAMPLE_SEQS * DEC_SEQ


def _xattn_prompt(layer, q, mem_k, mem_v):
    nq = SEQ // XA_Q_TILE
    n_blk = N_PROMPT // XA_Q_TILE
    kv_spec = pl.BlockSpec((1, N_MEM, XA_WIDTH), lambda b, j: (layer * BATCH + jnp.minimum(b, BATCH - 1), 0, 0))
    row_out = lambda b, j: (jnp.where(b < BATCH, b * nq + j, n_blk + jnp.minimum(j, N_SAMPLE // XA_Q_TILE - 1)), 0)
    return pl.pallas_call(
        _xattn_prompt_kernel,
        grid=(BATCH + 1, nq),
        in_specs=[pl.BlockSpec((XA_Q_TILE, XA_WIDTH), lambda b, j: (jnp.minimum(b * nq + j, n_blk - 1), 0)),
                  kv_spec, kv_spec],
        out_specs=pl.BlockSpec((XA_Q_TILE, XA_WIDTH), row_out),
        out_shape=jax.ShapeDtypeStruct((N_TOK, XA_WIDTH), BF16),
        compiler_params=_params(("arbitrary", "arbitrary")), name="xattn_prompt",
    )(q, mem_k, mem_v)


def _xattn_sample_kernel(q_ref, k_ref, v_ref, y_all, o_ref):
    del y_all
    rows = XA_SAMPLE_ROWS
    q = q_ref[...]
    qblk = jnp.concatenate([q[:, h * XA_HEADDIM:(h + 1) * XA_HEADDIM] for h in range(XA_HEADS)], axis=0)
    mem_head = lax.broadcasted_iota(jnp.int32, (N_MEM * XA_HEADS, 1), 0) % XA_HEADS
    col_head = lax.broadcasted_iota(jnp.int32, (1, XA_HEADS * rows), 1) // rows
    same_head = mem_head == col_head
    row_seq = (lax.broadcasted_iota(jnp.int32, (XA_HEADS * rows, 1), 0) % rows) // DEC_SEQ
    out = jnp.zeros((XA_HEADS * rows, XA_HEADDIM), F32)
    for s in range(XA_SAMPLE_SEQS):
        k2 = k_ref[0, s].reshape(N_MEM * XA_HEADS, XA_HEADDIM).astype(BF16)
        v2 = v_ref[0, s].reshape(N_MEM * XA_HEADS, XA_HEADDIM).astype(BF16)
        sc = jnp.where(same_head, _dot_nt(k2, qblk), -1e30)
        p = jnp.exp(sc - jnp.max(sc, axis=0, keepdims=True))
        p = (p / jnp.sum(p, axis=0, keepdims=True)).astype(BF16)
        y = lax.dot_general(p, v2, (((0,), (0,)), ((), ())), preferred_element_type=F32)
        out = jnp.where(row_seq == s, y, out)
    for h in range(XA_HEADS):
        o_ref[:, h * XA_HEADDIM:(h + 1) * XA_HEADDIM] = out[h * rows:(h + 1) * rows].astype(o_ref.dtype)


def _xattn_sample(layer, q, cache_k, cache_v, y_all):
    rows = XA_SAMPLE_ROWS
    blk0 = N_PROMPT // rows
    kv_spec = pl.BlockSpec((1, XA_SAMPLE_SEQS, N_MEM, XA_HEADS, XA_HEADDIM), lambda j: (layer, j, 0, 0, 0))
    return pl.pallas_call(
        _xattn_sample_kernel,
        grid=(DEC_BATCH // XA_SAMPLE_SEQS,),
        in_specs=[pl.BlockSpec((rows, XA_WIDTH), lambda j: (blk0 + j, 0)), kv_spec, kv_spec,
                  pl.BlockSpec(memory_space=pl.ANY)],
        out_specs=pl.BlockSpec((rows, XA_WIDTH), lambda j: (blk0 + j, 0)),
        out_shape=jax.ShapeDtypeStruct((N_TOK, XA_WIDTH), BF16),
        input_output_aliases={3: 0},
        compiler_params=_params(("parallel",)), name="xattn_sample",
    )(q, cache_k, cache_v, y_all)


MERGE_TILE = 512
MERGE_SUB = 256


def _merge_kernel(*refs, route):
    ygm, yssd, yxa, gates, x_p, x_s, pgm, pssd, pxa, wout, lng, lnb = refs[:12]
    if route:
        rw, rb, lower, o_f, ri_ref, rg_ref, cnt_ref, count = refs[12:]

        @pl.when(pl.program_id(0) == 0)
        def _():
            count[...] = jnp.zeros_like(count)
    else:
        o_f, o_b = refs[12:]
    is_prompt = pl.program_id(0) < N_PROMPT // MERGE_TILE
    for r in range(MERGE_TILE // MERGE_SUB):
        rows = slice(r * MERGE_SUB, (r + 1) * MERGE_SUB)
        g = gates[rows, :].astype(F32)
        m = _dot(ygm[rows, :], pgm[...]) * g[:, :D_MODEL]
        m = m + _dot(yssd[rows, :], pssd[...]) * g[:, D_MODEL:2 * D_MODEL]
        m = m + _dot(yxa[rows, :], pxa[...]) * g[:, 2 * D_MODEL:]
        h = _dot(m.astype(BF16), wout[...])
        x = jnp.where(is_prompt, x_p[rows, :], x_s[rows, :])
        y = _layer_norm(ALPHA * x + h, lng[...], lnb[...])
        o_f[rows, :] = y
        if route:
            ri_ref[rows, :], rg_ref[rows, :] = _route(y, rw[...], rb[...], lower[...], count)
        else:
            o_b[rows, :] = y.astype(BF16)
    if route:
        cnt_ref[...] = count[...]


def _merge(ygm, yssd, yxa, gates, x_p, x_s, x_s_row0, pgm, pssd, pxa, wout, lng, lnb, router=None):
    tm = MERGE_TILE
    npt = N_PROMPT // tm
    row = lambda i: (i, 0)
    acts = [ygm, yssd, yxa, gates]
    weights = [pgm, pssd, pxa, wout]
    consts = [lng, lnb]
    out_specs = [pl.BlockSpec((tm, D_MODEL), row)]
    out_shape = [jax.ShapeDtypeStruct((N_TOK, D_MODEL), F32)]
    scratch = []
    if router is None:
        out_specs.append(pl.BlockSpec((tm, D_MODEL), row))
        out_shape.append(jax.ShapeDtypeStruct((N_TOK, D_MODEL), BF16))
    else:
        consts += [*router, jnp.tril(jnp.ones((MERGE_SUB, MERGE_SUB), F32), k=-1).astype(BF16)]
        out_specs += [pl.BlockSpec((tm, LANES), row), pl.BlockSpec((tm, LANES), row), _const_spec((1, LANES))]
        out_shape += [jax.ShapeDtypeStruct((N_TOK, LANES), jnp.int32), jax.ShapeDtypeStruct((N_TOK, LANES), F32),
                      jax.ShapeDtypeStruct((1, LANES), F32)]
        scratch.append(pltpu.VMEM((1, LANES), F32))
    return pl.pallas_call(
        functools.partial(_merge_kernel, route=router is not None),
        grid=(N_TOK // tm,),
        in_specs=[pl.BlockSpec((tm, a.shape[1]), row) for a in acts]
                 + [pl.BlockSpec((tm, D_MODEL), lambda i: (jnp.minimum(i, npt - 1), 0)),
                    pl.BlockSpec((tm, D_MODEL), lambda i: (x_s_row0 // tm + jnp.maximum(i - npt, 0), 0))]
                 + [_resident_spec(a.shape) for a in weights] + [_const_spec(a.shape) for a in consts],
        out_specs=out_specs, out_shape=out_shape, scratch_shapes=scratch,
        compiler_params=_params(("arbitrary",)), name="merge_out_ln",
    )(*acts, x_p, x_s, *weights, *consts)


FFN_TILE = 512
FFN_SUB = 256


def _ffn_kernel(xb, xf, wg, wu, wd, lng, lnb, o_f, o_b):
    for r in range(FFN_TILE // FFN_SUB):
        rows = slice(r * FFN_SUB, (r + 1) * FFN_SUB)
        x = xb[rows, :]
        h = (_silu(_dot(x, wg[...])) * _dot(x, wu[...])).astype(BF16)
        y = _layer_norm(ALPHA * xf[rows, :] + _dot(h, wd[...]), lng[...], lnb[...])
        o_f[rows, :] = y
        o_b[rows, :] = y.astype(BF16)


def _resident_spec(shape):
    nd = len(shape)
    return pl.BlockSpec(shape, lambda *_: (0,) * nd, pipeline_mode=pl.Buffered(1))


def _ffn(xb, xf, wg, wu, wd, lng, lnb):
    tm = FFN_TILE
    row = lambda i: (i, 0)
    return pl.pallas_call(
        _ffn_kernel,
        grid=(N_TOK // tm,),
        in_specs=[pl.BlockSpec((tm, D_MODEL), row), pl.BlockSpec((tm, D_MODEL), row),
                  _resident_spec(wg.shape), _resident_spec(wu.shape), _resident_spec(wd.shape),
                  _const_spec(lng.shape), _const_spec(lnb.shape)],
        out_specs=[pl.BlockSpec((tm, D_MODEL), row), pl.BlockSpec((tm, D_MODEL), row)],
        out_shape=[jax.ShapeDtypeStruct((N_TOK, D_MODEL), F32), jax.ShapeDtypeStruct((N_TOK, D_MODEL), BF16)],
        compiler_params=_params(("parallel",)), name="ffn_swiglu_ln",
    )(xb, xf, wg, wu, wd, lng, lnb)


def _route(x, w, b, strict_lower, count):
    xs = _split2(x)
    ws = _split2(w)
    logits = b + _dot(xs[0], ws[0]) + _dot(xs[0], ws[1]) + _dot(xs[1], ws[0])
    lane = lax.broadcasted_iota(jnp.int32, logits.shape, 1)
    logits = jnp.where(lane < N_EXPERTS, logits, -1e30)
    m1 = jnp.max(logits, axis=-1, keepdims=True)
    i1 = jnp.min(jnp.where(logits == m1, lane, LANES), axis=-1, keepdims=True)
    rest = jnp.where(lane == i1, -1e30, logits)
    m2 = jnp.max(rest, axis=-1, keepdims=True)
    i2 = jnp.min(jnp.where(rest == m2, lane, LANES), axis=-1, keepdims=True)
    e2 = jnp.exp(m2 - m1)
    den = 1.0 + e2

    hit1, hit2 = lane == i1, lane == i2
    assigned = jnp.where(hit1 | hit2, 1.0, 0.0)
    rank = _dot(strict_lower, assigned.astype(BF16)) + count[...]
    r1 = jnp.sum(jnp.where(hit1, rank, 0.0), axis=-1, keepdims=True).astype(jnp.int32)
    r2 = jnp.sum(jnp.where(hit2, rank, 0.0), axis=-1, keepdims=True).astype(jnp.int32)
    ri = jnp.where(lane == 0, i1, jnp.where(lane == 1, i2, jnp.where(lane == 2, r1, jnp.where(lane == 3, r2, 0))))
    rg = jnp.where(lane == 0, 1.0 / den, jnp.where(lane == 1, e2 / den, 0.0))
    count[...] = count[...] + jnp.sum(assigned, axis=0, keepdims=True)
    return ri, rg


EXPERT_TILE = 1024
N_SLOTS = N_TOK * 2 + N_EXPERTS * EXPERT_TILE
N_SLOT_TILES = N_SLOTS // EXPERT_TILE
MOE_TOK_TILE = 1024
MOE_FF_TILE = 512


def _row_copy(src, src_row, dst, dst_row, sem):
    return pltpu.make_async_copy(src.at[pl.ds(src_row, 1)], dst.at[pl.ds(dst_row, 1)], sem)


N_ZERO_TILES = 2 * N_EXPERTS


def _dispatch_kernel(zt_ref, zf_ref, s1_ref, s2_ref, x_ref, xs_out, zeros, sem, zsem):
    def zero_copy(k):
        dst = xs_out.at[pl.ds(pl.multiple_of(zt_ref[k] * EXPERT_TILE, EXPERT_TILE), EXPERT_TILE)]
        return pltpu.make_async_copy(zeros, dst, zsem)

    @pl.when(pl.program_id(0) == 0)
    def _():
        zeros[...] = jnp.zeros_like(zeros)
        for k in range(N_ZERO_TILES):
            @pl.when(zf_ref[k] == 1)
            def _():
                zero_copy(k).start()
        for k in range(N_ZERO_TILES):
            @pl.when(zf_ref[k] == 1)
            def _():
                zero_copy(k).wait()

    def body(r, carry):
        _row_copy(x_ref, r, xs_out, s1_ref[r], sem).start()
        _row_copy(x_ref, r, xs_out, s2_ref[r], sem).start()
        return carry

    lax.fori_loop(0, MOE_TOK_TILE, body, 0, unroll=8)
    for _ in range(2):
        pltpu.make_async_copy(x_ref, xs_out.at[pl.ds(0, MOE_TOK_TILE)], sem).wait()


def _dispatch(zero_tiles, zero_flags, slot1, slot2, x):
    tm = MOE_TOK_TILE
    smem = lambda: pl.BlockSpec((tm,), lambda i, zt, zf: (i,), memory_space=pltpu.SMEM)
    grid_spec = pltpu.PrefetchScalarGridSpec(
        num_scalar_prefetch=2,
        grid=(N_TOK // tm,),
        in_specs=[smem(), smem(), pl.BlockSpec((tm, D_MODEL), lambda i, zt, zf: (i, 0))],
        out_specs=pl.BlockSpec(memory_space=pl.ANY),
        scratch_shapes=[pltpu.VMEM((EXPERT_TILE, D_MODEL), F32), pltpu.SemaphoreType.DMA(()),
                        pltpu.SemaphoreType.DMA(())],
    )
    return pl.pallas_call(
        _dispatch_kernel, grid_spec=grid_spec,
        out_shape=jax.ShapeDtypeStruct((N_SLOTS, D_MODEL), F32),
        compiler_params=_params(("arbitrary",)), name="moe_dispatch",
    )(zero_tiles, zero_flags, slot1, slot2, x)


def _expert_ffn_kernel(te_ref, nv_ref, xs_ref, wg, wu, wd, ys_ref, xb_s):
    del te_ref
    t = pl.program_id(0)
    k = pl.program_id(1)

    @pl.when(t < nv_ref[0])
    def _():
        @pl.when(k == 0)
        def _():
            xb_s[...] = xs_ref[...].astype(BF16)
            ys_ref[...] = jnp.zeros_like(ys_ref)

        xb = xb_s[...]
        h = _silu(_dot(xb, wg[0, 0].astype(BF16))) * _dot(xb, wu[0, 0].astype(BF16))
        ys_ref[...] += _dot(h.astype(BF16), wd[0, 0].astype(BF16))

    @pl.when((t >= nv_ref[0]) & (k == 0))
    def _():
        ys_ref[...] = jnp.zeros_like(ys_ref)


def _expert_ffn(layer, tile_expert, n_valid, xs, wg, wu, wd):
    tm, tf = EXPERT_TILE, MOE_FF_TILE
    nk = E_FF // tf

    def tile(t, nv):
        return jnp.minimum(t, nv[0] - 1)

    def chunk(t, k, nv):
        return jnp.where(t < nv[0], k, nk - 1)

    grid_spec = pltpu.PrefetchScalarGridSpec(
        num_scalar_prefetch=2,
        grid=(N_SLOT_TILES, nk),
        in_specs=[pl.BlockSpec((tm, D_MODEL), lambda t, k, te, nv: (tile(t, nv), 0)),
                  pl.BlockSpec((1, 1, D_MODEL, tf), lambda t, k, te, nv: (layer, te[tile(t, nv)], 0, chunk(t, k, nv))),
                  pl.BlockSpec((1, 1, D_MODEL, tf), lambda t, k, te, nv: (layer, te[tile(t, nv)], 0, chunk(t, k, nv))),
                  pl.BlockSpec((1, 1, tf, D_MODEL), lambda t, k, te, nv: (layer, te[tile(t, nv)], chunk(t, k, nv), 0))],
        out_specs=pl.BlockSpec((tm, D_MODEL), lambda t, k, te, nv: (t, 0)),
        scratch_shapes=[pltpu.VMEM((tm, D_MODEL), BF16)],
    )
    return pl.pallas_call(
        _expert_ffn_kernel, grid_spec=grid_spec,
        out_shape=jax.ShapeDtypeStruct((N_SLOTS, D_MODEL), F32),
        compiler_params=_params(("arbitrary", "arbitrary")), name="moe_expert_ffn",
    )(tile_expert, n_valid, xs, wg, wu, wd)


def _combine_kernel(s1_ref, s2_ref, s1n_ref, s2n_ref, x_ref, rg_ref, ys_hbm, lng, lnb, o_p, o_s, buf, sem):
    tm = MOE_TOK_TILE
    i = pl.program_id(0)

    def gather(sa_ref, sb_ref, slot):
        def body(r, carry):
            _row_copy(ys_hbm, sa_ref[r], buf.at[slot, 0], r, sem.at[slot]).start()
            _row_copy(ys_hbm, sb_ref[r], buf.at[slot, 1], r, sem.at[slot]).start()
            return carry

        lax.fori_loop(0, tm, body, 0, unroll=8)

    @pl.when(i == 0)
    def _():
        gather(s1_ref, s2_ref, 0)

    @pl.when(i + 1 < pl.num_programs(0))
    def _():
        gather(s1n_ref, s2n_ref, (i + 1) % 2)

    slot = i % 2
    for j in range(2):
        pltpu.make_async_copy(ys_hbm.at[pl.ds(0, tm)], buf.at[slot, j], sem.at[slot]).wait()

    g = rg_ref[...]
    f = g[:, 0:1] * buf[slot, 0] + g[:, 1:2] * buf[slot, 1]
    y = _layer_norm(ALPHA * x_ref[...] + f, lng[...], lnb[...])

    @pl.when(i < N_PROMPT // tm)
    def _():
        o_p[...] = y

    @pl.when(i >= N_PROMPT // tm)
    def _():
        o_s[...] = y


def _combine(slot1, slot2, x, rg, ys, lng, lnb):
    tm = MOE_TOK_TILE
    npt = N_PROMPT // tm
    n_tiles = N_TOK // tm
    smem = lambda: pl.BlockSpec((tm,), lambda i: (i,), memory_space=pltpu.SMEM)
    smem_next = lambda: pl.BlockSpec((tm,), lambda i: (jnp.minimum(i + 1, n_tiles - 1),), memory_space=pltpu.SMEM)
    return pl.pallas_call(
        _combine_kernel,
        grid=(n_tiles,),
        in_specs=[smem(), smem(), smem_next(), smem_next(), pl.BlockSpec((tm, D_MODEL), lambda i: (i, 0)),
                  pl.BlockSpec((tm, LANES), lambda i: (i, 0)), pl.BlockSpec(memory_space=pl.ANY),
                  _const_spec(lng.shape), _const_spec(lnb.shape)],
        out_specs=[pl.BlockSpec((tm, D_MODEL), lambda i: (jnp.minimum(i, npt - 1), 0)),
                   pl.BlockSpec((tm, D_MODEL), lambda i: (jnp.maximum(i - npt, 0), 0))],
        out_shape=[jax.ShapeDtypeStruct((N_PROMPT, D_MODEL), F32), jax.ShapeDtypeStruct((N_SAMPLE, D_MODEL), F32)],
        scratch_shapes=[pltpu.VMEM((2, 2, tm, D_MODEL), F32), pltpu.SemaphoreType.DMA((2,))],
        compiler_params=_params(("arbitrary",)), name="moe_combine_ln",
    )(slot1, slot2, slot1, slot2, x, rg, ys, lng, lnb)


def _moe(layer, x, ri, rg, cnt, wg, wu, wd, lng, lnb):
    counts = cnt[0, :N_EXPERTS].astype(jnp.int32)
    padded = (counts + EXPERT_TILE - 1) // EXPERT_TILE * EXPERT_TILE
    ends = jnp.cumsum(padded)
    starts = ends - padded
    slot1 = starts[ri[:, 0]] + ri[:, 2]
    slot2 = starts[ri[:, 1]] + ri[:, 3]
    n_valid = (ends[-1:] // EXPERT_TILE).astype(jnp.int32)
    tile_start = jnp.arange(N_SLOT_TILES, dtype=jnp.int32) * EXPERT_TILE
    tile_expert = jnp.minimum(jnp.sum(tile_start[:, None] >= ends[None, :], axis=1), N_EXPERTS - 1).astype(jnp.int32)

    tail = n_valid[0] + jnp.arange(N_EXPERTS, dtype=jnp.int32)
    zero_tiles = jnp.concatenate([(ends // EXPERT_TILE - 1).astype(jnp.int32), tail])
    zero_flags = jnp.concatenate([padded > 0, tail < N_SLOT_TILES]).astype(jnp.int32)
    zero_tiles = jnp.where(zero_flags == 1, zero_tiles, 0)

    xs = _dispatch(zero_tiles, zero_flags, slot1, slot2, x)
    ys = _expert_ffn(layer, tile_expert, n_valid, xs, wg, wu, wd)
    return _combine(slot1, slot2, x, rg, ys, lng, lnb)


def _row(a):
    return a.reshape(1, -1).astype(F32)


def _pad_lanes(a):
    return jnp.pad(a, ((0, 0), (0, LANES - a.shape[1])))


def _gmlp_weights(w_s, b_s):
    tril = jnp.tril(jnp.ones((GM_CHUNK, GM_CHUNK), dtype=bool))
    w_p = jnp.where(tril, w_s, 0.0)
    n_seq = ROWS // DEC_SEQ
    w_8 = w_p[:, :DEC_SEQ, :DEC_SEQ]
    w_d = jnp.einsum("ab,gij->gaibj", jnp.eye(n_seq, dtype=F32), w_8).reshape(GM_GROUPS, ROWS, ROWS)
    bias_p = jnp.repeat(b_s.T, GM_GROUP, axis=1)
    bias_d = jnp.tile(bias_p[:DEC_SEQ], (n_seq, 1))
    return jnp.stack([w_p, w_d]).astype(BF16), jnp.stack([bias_p, bias_d])


def kernel(x_prompt, x_sample, mem_prompt, cache_mem_k, cache_mem_v, state_conv, state_ssm, w_in, conv_w, conv_b, dt_bias, a_log, d_skip, ssd_norm_g, v_ln_g, v_ln_b, w_s, b_s, p_gm, p_ssd, p_xa, w_out, w_mem_k, w_mem_v, ln1_g, ln1_b, ln2_g, ln2_b, ffn_wg, ffn_wu, ffn_wd, router_w, router_b, moe_wg, moe_wu, moe_wd):
    assert DEPTH % 2 == 0
    x_p = x_prompt.reshape(N_PROMPT, D_MODEL)
    x_s = x_sample.reshape(N_SAMPLE, D_MODEL)
    x_s_row0 = 0
    xb = jnp.concatenate([x_p.astype(BF16), x_s.astype(BF16)], axis=0)
    mem_b = mem_prompt.reshape(BATCH * N_MEM, D_MODEL).astype(BF16)
    w_in_t = jnp.swapaxes(w_in, 1, 2)
    st_all = state_ssm.reshape(DEPTH, DEC_BATCH, SSD_INNER, SSD_STATE)

    mem_k_out, mem_v_out, mem_kb, mem_vb = _mem_kv(mem_b, w_mem_k, w_mem_v)
    mem_kb = mem_kb.reshape(DEPTH * BATCH, N_MEM, XA_WIDTH)
    mem_vb = mem_vb.reshape(DEPTH * BATCH, N_MEM, XA_WIDTH)

    ssm_p_out, conv_p_out, conv_s_out, v_out = [], [], [], []
    ssm_s = None
    for i in range(DEPTH):
        (u,) = _mm(xb, w_in_t, i, 0, GM_WIDTH, 1024, [], _epi_gelu, [BF16], "in_u")
        (v,) = _mm(xb, w_in_t, i, GM_WIDTH, GM_WIDTH, 1024, [_row(v_ln_g[i]), _row(v_ln_b[i])], _epi_gelu_ln, [F32],
                   "in_v", tm=1024)
        (zs,) = _mm(xb, w_in_t, i, OFF_Z, SSD_INNER, 1024, [], _epi_silu, [BF16], "in_z")
        (xbc,) = _mm(xb, w_in_t, i, OFF_XBC, CONV_DIM, 1024, [], _epi_id, [BF16], "in_xbc")
        (dt,) = _mm(xb, w_in_t, i, OFF_DT, LANES, LANES, [_pad_lanes(_row(dt_bias[i]))], _epi_softplus, [F32], "in_dt")
        (q,) = _mm(xb, w_in_t, i, OFF_Q, XA_WIDTH, 1024, [], _epi_qscale, [BF16], "in_q")
        (gates,) = _mm(xb, w_in_t, i, OFF_GATE, N_BRANCH * D_MODEL, 1024, [], _epi_sigmoid, [BF16], "in_gates")

        gm_w, gm_b = _gmlp_weights(w_s[i], b_s[i])
        y_gm = _gmlp(v, u, gm_w, gm_b)

        alog = _pad_lanes(_row(a_log[i]))
        dskip = _row(jnp.repeat(d_skip[i], SSD_HEADDIM))
        ssd_args = (alog, conv_w[i], _row(conv_b[i]), dskip, _row(ssd_norm_g[i]))
        y_ssd, ssm_p = _ssd_prompt(xbc, dt, zs, *ssd_args)
        prev_rows = jnp.pad(state_conv[i], ((0, 0), (DEC_SEQ - (SSD_CONV - 1), 0), (0, 0))).reshape(N_SAMPLE, CONV_DIM)
        y_ssd, ssm_s = _ssd_sample(i, xbc, prev_rows, dt, zs, st_all, y_ssd, ssm_s, *ssd_args)
        ssm_p_out.append(ssm_p.reshape(BATCH, SSD_HEADS, SSD_HEADDIM, SSD_STATE))
        slots = xbc.reshape(N_TOK // DEC_SEQ, DEC_SEQ, CONV_DIM)
        n_p, per_seq, keep = N_PROMPT // DEC_SEQ, SEQ // DEC_SEQ, SSD_CONV - 1
        conv_p_out.append(lax.slice(slots, (per_seq - 1, DEC_SEQ - keep, 0), (n_p, DEC_SEQ, CONV_DIM),
                                    (per_seq, 1, 1)).astype(F32))
        conv_s_out.append(lax.slice(slots, (n_p, DEC_SEQ - keep, 0), slots.shape).astype(F32))
        v_out.append(v[N_PROMPT:].reshape(DEC_BATCH, DEC_SEQ, GM_WIDTH))

        y_xa = _xattn_prompt(i, q, mem_kb, mem_vb)
        y_xa = _xattn_sample(i, q, cache_mem_k, cache_mem_v, y_xa)

        j = i // 2
        merge_args = (y_gm, y_ssd, y_xa, gates, x_p, x_s, x_s_row0, p_gm[i].astype(BF16), p_ssd[i].astype(BF16),
                      p_xa[i].astype(BF16), w_out[i].astype(BF16), _row(ln1_g[i]), _row(ln1_b[i]))
        if i % 2 == 0:
            x, xb = _merge(*merge_args)
            x, xb = _ffn(xb, x, ffn_wg[j].astype(BF16), ffn_wu[j].astype(BF16), ffn_wd[j].astype(BF16),
                         _row(ln2_g[i]), _row(ln2_b[i]))
            x_p, x_s, x_s_row0 = x, x, N_PROMPT
        else:
            x, ri, rg, cnt = _merge(*merge_args, router=(_pad_lanes(router_w[j]), _pad_lanes(_row(router_b[j]))))
            x_p, x_s = _moe(j, x, ri, rg, cnt, moe_wg, moe_wu, moe_wd, _row(ln2_g[i]), _row(ln2_b[i]))
            x_s_row0 = 0
            if i + 1 < DEPTH:
                xb = jnp.concatenate([x_p.astype(BF16), x_s.astype(BF16)], axis=0)

    y_prompt = x_p.reshape(BATCH, SEQ, D_MODEL)
    y_sample = x_s.reshape(DEC_BATCH, DEC_SEQ, D_MODEL)
    ssm_s_out = ssm_s.reshape(DEPTH, DEC_BATCH, SSD_HEADS, SSD_HEADDIM, SSD_STATE)
    return (y_prompt, y_sample, mem_k_out, mem_v_out, jnp.stack(conv_p_out),
            jnp.stack(ssm_p_out), jnp.stack(conv_s_out), ssm_s_out, jnp.stack(v_out))
```

```python
import functools

import jax
import jax.numpy as jnp
from jax import lax
from jax.experimental import pallas as pl
from jax.experimental.pallas import tpu as pltpu

F32 = jnp.float32
BF16 = jnp.bfloat16

D_MODEL = 1024
BATCH = 8
SEQ = 2048
DEPTH = 2
DEC_BATCH = 128
DEC_SEQ = 8
N_MEM = 256
GM_WIDTH = D_MODEL
GM_CHUNK = 128
GM_GROUP = 128
GM_GROUPS = GM_WIDTH // GM_GROUP
SSD_INNER = 2 * D_MODEL
SSD_HEADDIM = 64
SSD_HEADS = SSD_INNER // SSD_HEADDIM
SSD_STATE = 128
SSD_GROUPS = 4
SSD_HPG = SSD_HEADS // SSD_GROUPS
SSD_CONV = 4
SSD_CHUNK = 128
CONV_DIM = SSD_INNER + 2 * SSD_GROUPS * SSD_STATE
XA_HEADS = 4
XA_HEADDIM = D_MODEL // XA_HEADS
XA_WIDTH = XA_HEADS * XA_HEADDIM
N_BRANCH = 3
D_FF = ((8 * D_MODEL // 3 + 127) // 128) * 128
N_EXPERTS = 8
E_FF = 7 * D_MODEL // 2
ALPHA = (2 * DEPTH) ** 0.25
LN_EPS = 1e-5

N_PROMPT = BATCH * SEQ
N_SAMPLE = DEC_BATCH * DEC_SEQ
N_TOK = N_PROMPT + N_SAMPLE

OFF_Z = 2 * GM_WIDTH
OFF_XBC = OFF_Z + SSD_INNER
OFF_DT = OFF_XBC + CONV_DIM
OFF_Q = OFF_DT + SSD_HEADS
OFF_GATE = OFF_Q + XA_WIDTH

LANES = 128
ROWS = 128
GROUP_W = SSD_HPG * SSD_HEADDIM
VMEM_LIMIT = 56 * 1024 * 1024


def _params(sem):
    return pltpu.CompilerParams(dimension_semantics=sem, vmem_limit_bytes=VMEM_LIMIT)


def _dot(a, b):
    return jnp.dot(a, b, preferred_element_type=F32)


def _dot_nt(a, b):
    return lax.dot_general(a, b, (((1,), (1,)), ((), ())), preferred_element_type=F32)


def _split2(x):
    hi = x.astype(BF16)
    lo = (x - hi.astype(F32)).astype(BF16)
    return hi, lo


def _split3(x):
    hi = x.astype(BF16)
    r = x - hi.astype(F32)
    mid = r.astype(BF16)
    lo = (r - mid.astype(F32)).astype(BF16)
    return hi, mid, lo


def _dot_exact_lhs(m_bf16, x, pieces=3):
    parts = _split3(x) if pieces == 3 else _split2(x)
    n = x.shape[1]
    out = _dot(m_bf16, jnp.concatenate(parts, axis=1))
    return sum(out[:, i * n:(i + 1) * n] for i in range(1, pieces)) + out[:, :n]


def _expand_rhs2(xs, m_bf16):
    rows = xs[0].shape[0]
    out = _dot(jnp.concatenate([p for x in xs for p in _split2(x)], axis=0), m_bf16)
    return [out[2 * i * rows:(2 * i + 1) * rows] + out[(2 * i + 1) * rows:(2 * i + 2) * rows] for i in range(len(xs))]


def _layer_norm(r, g, b):
    mu = jnp.mean(r, axis=-1, keepdims=True)
    c = r - mu
    var = jnp.mean(c * c, axis=-1, keepdims=True)
    return c * lax.rsqrt(var + LN_EPS) * g + b


def _gelu_tanh(x):
    return x * (0.5 * (1.0 + jnp.tanh(0.7978845608028654 * (x + 0.044715 * (x * x * x)))))


def _sigmoid(x):
    return 1.0 / (1.0 + jnp.exp(-x))


def _silu(x):
    return x * _sigmoid(x)


def _softplus(x):
    return jnp.maximum(x, 0.0) + jnp.log(1.0 + jnp.exp(-jnp.abs(x)))


def _mm_kernel(x_ref, wt_ref, *rest, epilogue, n_extra, n_out):
    extras = [r[...] for r in rest[:n_extra]]
    outs = rest[n_extra:n_extra + n_out]
    w_bf16 = rest[n_extra + n_out]

    @pl.when(pl.program_id(1) == 0)
    def _():
        w_bf16[...] = wt_ref[0].T.astype(BF16)

    acc = _dot(x_ref[...], w_bf16[...])
    res = epilogue(acc, *extras)
    for o, r in zip(outs, res):
        o[...] = r.astype(o.dtype)


MM_TILE = 2176


def _mm(x, wt, layer, row0, n_cols, tn, extras, epilogue, out_dtypes, name, tm=MM_TILE):
    t, k = x.shape
    assert row0 % 8 == 0
    grid = (n_cols // tn, t // tm)
    in_specs = [pl.BlockSpec((tm, k), lambda j, i: (i, 0)),
                pl.BlockSpec((pl.Element(1), pl.Element(tn), pl.Element(k)),
                             lambda j, i: (layer, pl.multiple_of(row0 + j * tn, 8), 0))]
    in_specs += [pl.BlockSpec((1, tn), lambda j, i: (0, j)) for _ in extras]
    out_specs = [pl.BlockSpec((tm, tn), lambda j, i: (i, j)) for _ in out_dtypes]
    out_shape = [jax.ShapeDtypeStruct((t, n_cols), d) for d in out_dtypes]
    return pl.pallas_call(
        functools.partial(_mm_kernel, epilogue=epilogue, n_extra=len(extras), n_out=len(out_dtypes)),
        grid=grid, in_specs=in_specs, out_specs=out_specs, out_shape=out_shape,
        scratch_shapes=[pltpu.VMEM((k, tn), BF16)],
        compiler_params=_params(("parallel", "arbitrary")), name=name,
    )(x, wt, *extras)


MEM_TILE_SEQS = 4


def _mem_kv_kernel(m_ref, wk_ref, wv_ref, k5_ref, v5_ref, kb_ref, vb_ref):
    m = m_ref[...]
    for w_ref, o5_ref, ob_ref in ((wk_ref, k5_ref, kb_ref), (wv_ref, v5_ref, vb_ref)):
        acc = _dot(m, w_ref[0].astype(BF16))
        ob_ref[0] = acc.astype(BF16)
        for b in range(MEM_TILE_SEQS):
            for h in range(XA_HEADS):
                o5_ref[0, b, :, h, :] = acc[b * N_MEM:(b + 1) * N_MEM, h * XA_HEADDIM:(h + 1) * XA_HEADDIM]


def _mem_kv(mem_b, w_mem_k, w_mem_v):
    tm = MEM_TILE_SEQS * N_MEM
    n5 = (DEPTH, BATCH, N_MEM, XA_HEADS, XA_HEADDIM)
    w_spec = pl.BlockSpec((1, D_MODEL, XA_WIDTH), lambda l, r: (l, 0, 0))
    o5_spec = pl.BlockSpec((1, MEM_TILE_SEQS, N_MEM, XA_HEADS, XA_HEADDIM), lambda l, r: (l, r, 0, 0, 0))
    ob_spec = pl.BlockSpec((1, tm, XA_WIDTH), lambda l, r: (l, r, 0))
    return pl.pallas_call(
        _mem_kv_kernel,
        grid=(DEPTH, BATCH // MEM_TILE_SEQS),
        in_specs=[pl.BlockSpec((tm, D_MODEL), lambda l, r: (r, 0)), w_spec, w_spec],
        out_specs=[o5_spec, o5_spec, ob_spec, ob_spec],
        out_shape=[jax.ShapeDtypeStruct(n5, F32), jax.ShapeDtypeStruct(n5, F32),
                   jax.ShapeDtypeStruct((DEPTH, BATCH * N_MEM, XA_WIDTH), BF16),
                   jax.ShapeDtypeStruct((DEPTH, BATCH * N_MEM, XA_WIDTH), BF16)],
        compiler_params=_params(("parallel", "parallel")), name="mem_kv",
    )(mem_b, w_mem_k, w_mem_v)


def _epi_gelu(acc):
    return (_gelu_tanh(acc),)


def _epi_gelu_ln(acc, g, b):
    return (_layer_norm(_gelu_tanh(acc), g, b),)


def _epi_silu(acc):
    return (_silu(acc),)


def _epi_id(acc):
    return (acc,)


def _epi_softplus(acc, bias):
    return (_softplus(acc + bias),)


def _epi_qscale(acc):
    return (acc * (XA_HEADDIM ** -0.5),)


def _epi_sigmoid(acc):
    return (_sigmoid(acc),)


GM_TILE = 512


def _gmlp_kernel(v_ref, u_ref, w_ref, b_ref, o_ref):
    for c in range(GM_TILE // ROWS):
        rs = slice(c * ROWS, (c + 1) * ROWS)
        for g in range(GM_GROUPS):
            cs = slice(g * GM_GROUP, (g + 1) * GM_GROUP)
            z = _dot(w_ref[0, g], v_ref[rs, cs].astype(BF16)) + b_ref[0, :, cs]
            o_ref[rs, cs] = (u_ref[rs, cs].astype(F32) * z).astype(o_ref.dtype)


def _gmlp(v, u, w2, b2):
    n_prompt_tiles = N_PROMPT // GM_TILE

    def sel(i):
        return jnp.where(i >= n_prompt_tiles, 1, 0)

    return pl.pallas_call(
        _gmlp_kernel,
        grid=(N_TOK // GM_TILE,),
        in_specs=[pl.BlockSpec((GM_TILE, GM_WIDTH), lambda i: (i, 0)),
                  pl.BlockSpec((GM_TILE, GM_WIDTH), lambda i: (i, 0)),
                  pl.BlockSpec((1, GM_GROUPS, ROWS, ROWS), lambda i: (sel(i), 0, 0, 0)),
                  pl.BlockSpec((1, ROWS, GM_WIDTH), lambda i: (sel(i), 0, 0))],
        out_specs=pl.BlockSpec((GM_TILE, GM_WIDTH), lambda i: (i, 0)),
        out_shape=jax.ShapeDtypeStruct((N_TOK, GM_WIDTH), BF16),
        compiler_params=_params(("parallel",)), name="gmlp_spatial",
    )(v, u, w2, b2)


def _conv_silu(x_back, conv_w, conv_b):
    conv = conv_b + x_back(0) * conv_w[3:4, :]
    for s in range(1, SSD_CONV):
        conv = conv + x_back(s) * conv_w[3 - s:4 - s, :]
    return _silu(conv)


def _ssd_block(xc, dt, alog, lmat, bones, emat, dskip):
    xs = xc[:, :SSD_INNER].astype(F32)
    bm = xc[:, SSD_INNER:SSD_INNER + SSD_GROUPS * SSD_STATE].astype(BF16)
    cm = xc[:, SSD_INNER + SSD_GROUPS * SSD_STATE:].astype(BF16)

    lane = lax.broadcasted_iota(jnp.int32, (1, LANES), 1)
    a_neg = jnp.where(lane < SSD_HEADS, -jnp.exp(alog), 0.0)
    adt = dt * a_neg
    sums = _dot_exact_lhs(jnp.concatenate([lmat, bones], axis=0), adt)
    a_cs, a_tot = sums[:ROWS], sums[ROWS:]
    a_cs_t = a_cs.T
    mask = lmat.astype(F32) > 0.5

    dt_x, eacs_x, te_x = _expand_rhs2([dt, jnp.exp(a_cs), jnp.exp(a_tot - a_cs)], emat)
    xdt = xs * dt_x
    xw_t = (xdt * te_x).T.astype(BF16)

    lane_r = lax.broadcasted_iota(jnp.int32, (ROWS, LANES), 1)
    lo_half = lane_r < SSD_HEADDIM
    yd = []
    for g in range(SSD_GROUPS):
        ns = slice(g * SSD_STATE, (g + 1) * SSD_STATE)
        cb = _dot_nt(cm[:, ns], bm[:, ns])
        for hp in range(SSD_HPG // 2):
            h0 = g * SSD_HPG + 2 * hp
            ms = []
            for h in (h0, h0 + 1):
                seg = a_cs[:, h:h + 1] - a_cs_t[h:h + 1, :]
                ms.append(cb * jnp.exp(jnp.where(mask, seg, -1e30)))
            lhs = jnp.concatenate(ms, axis=1).astype(BF16)
            xp = xdt[:, h0 * SSD_HEADDIM:(h0 + 2) * SSD_HEADDIM]
            rhs = jnp.concatenate([jnp.where(lo_half, xp, 0.0), jnp.where(lo_half, 0.0, xp)],
                                  axis=0).astype(BF16)
            yd.append(_dot(lhs, rhs))
    y_pre = jnp.concatenate(yd, axis=1) + xs * dskip
    return dict(y_pre=y_pre, eacs_x=eacs_x, cm=cm, bm=bm, xw_t=xw_t, a_tot=a_tot)


def _ssd_finish(y, zs, norm_g):
    y = y * zs
    outs = []
    for g in range(SSD_GROUPS):
        yg = y[:, g * GROUP_W:(g + 1) * GROUP_W]
        ms = jnp.mean(yg * yg, axis=-1, keepdims=True)
        outs.append(yg * lax.rsqrt(ms + LN_EPS))
    return jnp.concatenate(outs, axis=1) * norm_g


PROMPT_STEP_ROWS = 2 * ROWS


def _ssd_prompt_kernel(*refs):
    y_ref = refs[12]

    @pl.when(pl.program_id(0) < BATCH)
    def _():
        _ssd_prompt_body(*refs)

    @pl.when(pl.program_id(0) == BATCH)
    def _():
        y_ref[...] = jnp.zeros_like(y_ref)


def _ssd_prompt_body(xbc_ref, dt_ref, zs_ref, alog_ref, cw_ref, cb_ref, l_ref, ones_ref, e_ref, et_ref, dskip_ref,
                     ng_ref, y_ref, st_ref, xpad, state):
    @pl.when(pl.program_id(1) == 0)
    def _():
        xpad[0:8, :] = jnp.zeros((8, CONV_DIM), F32)
        state[...] = jnp.zeros_like(state)

    xpad[8:8 + PROMPT_STEP_ROWS, :] = xbc_ref[...].astype(F32)
    for j in range(PROMPT_STEP_ROWS // ROWS):
        rows = slice(j * ROWS, (j + 1) * ROWS)
        base = 8 + j * ROWS
        xc = _conv_silu(lambda s: xpad[base - s:base - s + ROWS, :], cw_ref[...], cb_ref[...])
        r = _ssd_block(xc, dt_ref[rows, :], alog_ref[...], l_ref[...], ones_ref[...], e_ref[...], dskip_ref[...])

        st = state[...]
        st_b = st.astype(BF16)
        y_off, s_new = [], []
        for g in range(SSD_GROUPS):
            ns = slice(g * SSD_STATE, (g + 1) * SSD_STATE)
            gs = slice(g * GROUP_W, (g + 1) * GROUP_W)
            y_off.append(_dot_nt(r["cm"][:, ns], st_b[gs, :]))
            s_new.append(_dot(r["xw_t"][gs, :], r["bm"][:, ns]))
        y = r["y_pre"] + jnp.concatenate(y_off, axis=1) * r["eacs_x"]
        y_ref[rows, :] = _ssd_finish(y, zs_ref[rows, :].astype(F32), ng_ref[...]).astype(y_ref.dtype)

        decay = _dot_exact_lhs(et_ref[...], jnp.exp(r["a_tot"].T), pieces=2)
        state[...] = decay * st + jnp.concatenate(s_new, axis=0)
    xpad[0:8, :] = xpad[PROMPT_STEP_ROWS:PROMPT_STEP_ROWS + 8, :]
    st_ref[0] = state[...]


def _ssd_consts(kind):
    tril = jnp.tril(jnp.ones((ROWS, ROWS), F32))
    if kind == "prompt":
        lmat, bones = tril, jnp.ones((ROWS, ROWS), F32)
    else:
        eye = jnp.eye(ROWS // DEC_SEQ, dtype=F32)
        blk = jnp.kron(eye, jnp.ones((DEC_SEQ, DEC_SEQ), F32))
        lmat, bones = tril * blk, blk
    head = jnp.arange(SSD_INNER) // SSD_HEADDIM
    emat = (jnp.arange(LANES)[:, None] == head[None, :]).astype(BF16)
    if kind == "prompt":
        return lmat.astype(BF16), bones.astype(BF16), emat, emat.T, None
    t = jnp.arange(ROWS)[:, None]
    col = jnp.arange(3 * ROWS)[None, :]
    shifts = []
    for s in range(1, SSD_CONV):
        inside = t % DEC_SEQ >= s
        earlier = (col % ROWS == t + DEC_SEQ - s) & (col >= ROWS)
        shifts.append(jnp.where(inside, col == t - s, earlier))
    shift = jnp.concatenate(shifts, axis=0).astype(BF16)
    return lmat.astype(BF16), bones.astype(BF16), emat, emat.T, shift


def _const_spec(shape):
    nd = len(shape)
    return pl.BlockSpec(shape, lambda *_: (0,) * nd)


def _ssd_prompt(xbc, dt, zs, alog, conv_w, conv_b, dskip, norm_g):
    lmat, bones, emat, emat_t, _ = _ssd_consts("prompt")
    tr = PROMPT_STEP_ROWS
    nc = SEQ // tr
    n_blk = N_PROMPT // tr
    row = lambda b, c: (jnp.minimum(b * nc + c, n_blk - 1), 0)
    row_out = lambda b, c: (jnp.where(b < BATCH, b * nc + c, n_blk + jnp.minimum(c, N_SAMPLE // tr - 1)), 0)
    consts = [alog, conv_w, conv_b, lmat, bones, emat, emat_t, dskip, norm_g]
    return pl.pallas_call(
        _ssd_prompt_kernel,
        grid=(BATCH + 1, nc),
        in_specs=[pl.BlockSpec((tr, CONV_DIM), row), pl.BlockSpec((tr, LANES), row),
                  pl.BlockSpec((tr, SSD_INNER), row)] + [_const_spec(a.shape) for a in consts],
        out_specs=[pl.BlockSpec((tr, SSD_INNER), row_out),
                   pl.BlockSpec((1, SSD_INNER, SSD_STATE), lambda b, c: (jnp.minimum(b, BATCH - 1), 0, 0))],
        out_shape=[jax.ShapeDtypeStruct((N_TOK, SSD_INNER), BF16),
                   jax.ShapeDtypeStruct((BATCH, SSD_INNER, SSD_STATE), F32)],
        scratch_shapes=[pltpu.VMEM((tr + 8, CONV_DIM), F32), pltpu.VMEM((SSD_INNER, SSD_STATE), F32)],
        compiler_params=_params(("arbitrary", "arbitrary")), name="ssd_prompt",
    )(xbc, dt, zs, *consts)


SEQ_PER_STEP = 4
Q_ROWS = SEQ_PER_STEP * DEC_SEQ
N_QUARTER = ROWS // Q_ROWS


def _ssd_sample_kernel(*refs, n_alias):
    (xbc_ref, prev_ref, dt_ref, zs_ref, st_in, alog_ref, cw_ref, cb_ref, sh_ref, l_ref, ones_ref, e_ref, et_ref,
     dskip_ref, ng_ref) = refs[:15]
    (y_ref, st_out, ypre_s, eacs_s, cm_s, bm_s, xwt_s, eat_s) = refs[15 + n_alias:]
    q = pl.program_id(1)

    @pl.when(q == 0)
    def _():
        x_cur = xbc_ref[...]
        p_hi, p_lo = _split2(prev_ref[...])
        back = _dot(sh_ref[...], jnp.concatenate([x_cur, p_hi, p_lo], axis=0))
        xc = _conv_silu(lambda s: x_cur.astype(F32) if s == 0 else back[(s - 1) * ROWS:s * ROWS, :],
                        cw_ref[...], cb_ref[...])
        r = _ssd_block(xc, dt_ref[...], alog_ref[...], l_ref[...], ones_ref[...], e_ref[...], dskip_ref[...])
        ypre_s[...] = r["y_pre"]
        eacs_s[...] = r["eacs_x"]
        cm_s[...] = r["cm"].astype(F32)
        bm_s[...] = r["bm"].astype(F32)
        xwt_s[...] = r["xw_t"]
        eat_s[...] = jnp.exp(r["a_tot"].T)

    q0 = pl.multiple_of(q * Q_ROWS, Q_ROWS)
    cq = cm_s[pl.ds(q0, Q_ROWS), :].astype(BF16)
    row_q = lax.broadcasted_iota(jnp.int32, (Q_ROWS, 1), 0) // DEC_SEQ
    row_b = lax.broadcasted_iota(jnp.int32, (ROWS, 1), 0) // DEC_SEQ
    lane_b = lax.broadcasted_iota(jnp.int32, (1, LANES), 1) // DEC_SEQ
    e_atot_t = eat_s[...]
    et = et_ref[...]
    y_off = jnp.zeros((Q_ROWS, SSD_INNER), F32)
    for s in range(SEQ_PER_STEP):
        seq = q * SEQ_PER_STEP + s
        st = st_in[0, s]
        st_b = st.astype(BF16)
        bsel = row_b == seq
        yo, s_new = [], []
        for g in range(SSD_GROUPS):
            ns = slice(g * SSD_STATE, (g + 1) * SSD_STATE)
            gs = slice(g * GROUP_W, (g + 1) * GROUP_W)
            yo.append(_dot_nt(cq[:, ns], st_b[gs, :]))
            bm_g = jnp.where(bsel, bm_s[:, ns], 0.0).astype(BF16)
            s_new.append(_dot(xwt_s[gs, :], bm_g))
        y_off = y_off + jnp.where(row_q == s, jnp.concatenate(yo, axis=1), 0.0)
        dec_col = jnp.sum(jnp.where(lane_b == seq, e_atot_t, 0.0), axis=1, keepdims=True) * (1.0 / DEC_SEQ)
        decay = _dot_exact_lhs(et, jnp.broadcast_to(dec_col, (LANES, SSD_STATE)), pieces=2)
        new_state = decay * st + jnp.concatenate(s_new, axis=0)
        for d in range(st_out.shape[0]):
            st_out[d, s] = new_state

    y = ypre_s[pl.ds(q0, Q_ROWS), :] + y_off * eacs_s[pl.ds(q0, Q_ROWS), :]
    y_ref[...] = _ssd_finish(y, zs_ref[...].astype(F32), ng_ref[...]).astype(y_ref.dtype)


def _ssd_sample(layer, xbc, prev_rows, dt, zs, st_all, y_all, st_out_prev, alog, conv_w, conv_b, dskip, norm_g):
    lmat, bones, emat, emat_t, shift = _ssd_consts("sample")
    blk0 = N_PROMPT // ROWS
    qblk0 = N_PROMPT // Q_ROWS
    consts = [alog, conv_w, conv_b, shift, lmat, bones, emat, emat_t, dskip, norm_g]
    st_spec = pl.BlockSpec((1, SEQ_PER_STEP, SSD_INNER, SSD_STATE), lambda b, q: (layer, b * N_QUARTER + q, 0, 0))
    if st_out_prev is None:
        assert layer == 0
        st_out_spec = pl.BlockSpec((DEPTH, SEQ_PER_STEP, SSD_INNER, SSD_STATE),
                                   lambda b, q: (0, b * N_QUARTER + q, 0, 0))
    else:
        st_out_spec = st_spec
    aliased = [y_all] + ([] if st_out_prev is None else [st_out_prev])
    n_in = 5 + len(consts)
    aliases = {n_in: 0} if st_out_prev is None else {n_in: 0, n_in + 1: 1}
    return pl.pallas_call(
        functools.partial(_ssd_sample_kernel, n_alias=len(aliased)),
        grid=(N_SAMPLE // ROWS, N_QUARTER),
        in_specs=[pl.BlockSpec((ROWS, CONV_DIM), lambda b, q: (blk0 + b, 0)),
                  pl.BlockSpec((ROWS, CONV_DIM), lambda b, q: (b, 0)),
                  pl.BlockSpec((ROWS, LANES), lambda b, q: (blk0 + b, 0)),
                  pl.BlockSpec((Q_ROWS, SSD_INNER), lambda b, q: (qblk0 + b * N_QUARTER + q, 0)),
                  st_spec] + [_const_spec(a.shape) for a in consts]
                 + [pl.BlockSpec(memory_space=pl.ANY) for _ in aliased],
        out_specs=[pl.BlockSpec((Q_ROWS, SSD_INNER), lambda b, q: (qblk0 + b * N_QUARTER + q, 0)), st_out_spec],
        out_shape=[jax.ShapeDtypeStruct((N_TOK, SSD_INNER), BF16),
                   jax.ShapeDtypeStruct((DEPTH, DEC_BATCH, SSD_INNER, SSD_STATE), F32)],
        scratch_shapes=[pltpu.VMEM((ROWS, SSD_INNER), F32), pltpu.VMEM((ROWS, SSD_INNER), F32),
                        pltpu.VMEM((ROWS, SSD_GROUPS * SSD_STATE), F32),
                        pltpu.VMEM((ROWS, SSD_GROUPS * SSD_STATE), F32),
                        pltpu.VMEM((SSD_INNER, ROWS), BF16), pltpu.VMEM((LANES, ROWS), F32)],
        input_output_aliases=aliases,
        compiler_params=_params(("arbitrary", "arbitrary")), name="ssd_sample",
    )(xbc, prev_rows, dt, zs, st_all, *consts, *aliased)


def _xattn_prompt_kernel(q_ref, k_ref, v_ref, o_ref):
    @pl.when(pl.program_id(0) < BATCH)
    def _():
        q = q_ref[...]
        for h in range(XA_HEADS):
            hs = slice(h * XA_HEADDIM, (h + 1) * XA_HEADDIM)
            sc = _dot_nt(q[:, hs], k_ref[0, :, hs])
            p = jnp.exp(sc - jnp.max(sc, axis=-1, keepdims=True))
            p = (p / jnp.sum(p, axis=-1, keepdims=True)).astype(BF16)
            o_ref[:, hs] = _dot(p, v_ref[0, :, hs]).astype(o_ref.dtype)

    @pl.when(pl.program_id(0) == BATCH)
    def _():
        o_ref[...] = jnp.zeros_like(o_ref)


XA_Q_TILE = 512
XA_SAMPLE_SEQS = 4
XA_SAMPLE_ROWS = XA_SAMPLE_SEQS * DEC_SEQ


def _xattn_prompt(layer, q, mem_k, mem_v):
    nq = SEQ // XA_Q_TILE
    n_blk = N_PROMPT // XA_Q_TILE
    kv_spec = pl.BlockSpec((1, N_MEM, XA_WIDTH), lambda b, j: (layer * BATCH + jnp.minimum(b, BATCH - 1), 0, 0))
    row_out = lambda b, j: (jnp.where(b < BATCH, b * nq + j, n_blk + jnp.minimum(j, N_SAMPLE // XA_Q_TILE - 1)), 0)
    return pl.pallas_call(
        _xattn_prompt_kernel,
        grid=(BATCH + 1, nq),
        in_specs=[pl.BlockSpec((XA_Q_TILE, XA_WIDTH), lambda b, j: (jnp.minimum(b * nq + j, n_blk - 1), 0)),
                  kv_spec, kv_spec],
        out_specs=pl.BlockSpec((XA_Q_TILE, XA_WIDTH), row_out),
        out_shape=jax.ShapeDtypeStruct((N_TOK, XA_WIDTH), BF16),
        compiler_params=_params(("arbitrary", "arbitrary")), name="xattn_prompt",
    )(q, mem_k, mem_v)


def _xattn_sample_kernel(q_ref, k_ref, v_ref, y_all, o_ref):
    del y_all
    rows = XA_SAMPLE_ROWS
    q = q_ref[...]
    qblk = jnp.concatenate([q[:, h * XA_HEADDIM:(h + 1) * XA_HEADDIM] for h in range(XA_HEADS)], axis=0)
    mem_head = lax.broadcasted_iota(jnp.int32, (N_MEM * XA_HEADS, 1), 0) % XA_HEADS
    col_head = lax.broadcasted_iota(jnp.int32, (1, XA_HEADS * rows), 1) // rows
    same_head = mem_head == col_head
    row_seq = (lax.broadcasted_iota(jnp.int32, (XA_HEADS * rows, 1), 0) % rows) // DEC_SEQ
    out = jnp.zeros((XA_HEADS * rows, XA_HEADDIM), F32)
    for s in range(XA_SAMPLE_SEQS):
        k2 = k_ref[0, s].reshape(N_MEM * XA_HEADS, XA_HEADDIM).astype(BF16)
        v2 = v_ref[0, s].reshape(N_MEM * XA_HEADS, XA_HEADDIM).astype(BF16)
        sc = jnp.where(same_head, _dot_nt(k2, qblk), -1e30)
        p = jnp.exp(sc - jnp.max(sc, axis=0, keepdims=True))
        p = (p / jnp.sum(p, axis=0, keepdims=True)).astype(BF16)
        y = lax.dot_general(p, v2, (((0,), (0,)), ((), ())), preferred_element_type=F32)
        out = jnp.where(row_seq == s, y, out)
    for h in range(XA_HEADS):
        o_ref[:, h * XA_HEADDIM:(h + 1) * XA_HEADDIM] = out[h * rows:(h + 1) * rows].astype(o_ref.dtype)


def _xattn_sample(layer, q, cache_k, cache_v, y_all):
    rows = XA_SAMPLE_ROWS
    blk0 = N_PROMPT // rows
    kv_spec = pl.BlockSpec((1, XA_SAMPLE_SEQS, N_MEM, XA_HEADS, XA_HEADDIM), lambda j: (layer, j, 0, 0, 0))
    return pl.pallas_call(
        _xattn_sample_kernel,
        grid=(DEC_BATCH // XA_SAMPLE_SEQS,),
        in_specs=[pl.BlockSpec((rows, XA_WIDTH), lambda j: (blk0 + j, 0)), kv_spec, kv_spec,
                  pl.BlockSpec(memory_space=pl.ANY)],
        out_specs=pl.BlockSpec((rows, XA_WIDTH), lambda j: (blk0 + j, 0)),
        out_shape=jax.ShapeDtypeStruct((N_TOK, XA_WIDTH), BF16),
        input_output_aliases={3: 0},
        compiler_params=_params(("parallel",)), name="xattn_sample",
    )(q, cache_k, cache_v, y_all)


MERGE_TILE = 512
MERGE_SUB = 256


def _merge_kernel(*refs, route):
    ygm, yssd, yxa, gates, x_p, x_s, pgm, pssd, pxa, wout, lng, lnb = refs[:12]
    if route:
        rw, rb, lower, o_f, ri_ref, rg_ref, cnt_ref, count = refs[12:]

        @pl.when(pl.program_id(0) == 0)
        def _():
            count[...] = jnp.zeros_like(count)
    else:
        o_f, o_b = refs[12:]
    is_prompt = pl.program_id(0) < N_PROMPT // MERGE_TILE
    for r in range(MERGE_TILE // MERGE_SUB):
        rows = slice(r * MERGE_SUB, (r + 1) * MERGE_SUB)
        g = gates[rows, :].astype(F32)
        m = _dot(ygm[rows, :], pgm[...]) * g[:, :D_MODEL]
        m = m + _dot(yssd[rows, :], pssd[...]) * g[:, D_MODEL:2 * D_MODEL]
        m = m + _dot(yxa[rows, :], pxa[...]) * g[:, 2 * D_MODEL:]
        h = _dot(m.astype(BF16), wout[...])
        x = jnp.where(is_prompt, x_p[rows, :], x_s[rows, :])
        y = _layer_norm(ALPHA * x + h, lng[...], lnb[...])
        o_f[rows, :] = y
        if route:
            ri_ref[rows, :], rg_ref[rows, :] = _route(y, rw[...], rb[...], lower[...], count)
        else:
            o_b[rows, :] = y.astype(BF16)
    if route:
        cnt_ref[...] = count[...]


def _merge(ygm, yssd, yxa, gates, x_p, x_s, x_s_row0, pgm, pssd, pxa, wout, lng, lnb, router=None):
    tm = MERGE_TILE
    npt = N_PROMPT // tm
    row = lambda i: (i, 0)
    acts = [ygm, yssd, yxa, gates]
    weights = [pgm, pssd, pxa, wout]
    consts = [lng, lnb]
    out_specs = [pl.BlockSpec((tm, D_MODEL), row)]
    out_shape = [jax.ShapeDtypeStruct((N_TOK, D_MODEL), F32)]
    scratch = []
    if router is None:
        out_specs.append(pl.BlockSpec((tm, D_MODEL), row))
        out_shape.append(jax.ShapeDtypeStruct((N_TOK, D_MODEL), BF16))
    else:
        consts += [*router, jnp.tril(jnp.ones((MERGE_SUB, MERGE_SUB), F32), k=-1).astype(BF16)]
        out_specs += [pl.BlockSpec((tm, LANES), row), pl.BlockSpec((tm, LANES), row), _const_spec((1, LANES))]
        out_shape += [jax.ShapeDtypeStruct((N_TOK, LANES), jnp.int32), jax.ShapeDtypeStruct((N_TOK, LANES), F32),
                      jax.ShapeDtypeStruct((1, LANES), F32)]
        scratch.append(pltpu.VMEM((1, LANES), F32))
    return pl.pallas_call(
        functools.partial(_merge_kernel, route=router is not None),
        grid=(N_TOK // tm,),
        in_specs=[pl.BlockSpec((tm, a.shape[1]), row) for a in acts]
                 + [pl.BlockSpec((tm, D_MODEL), lambda i: (jnp.minimum(i, npt - 1), 0)),
                    pl.BlockSpec((tm, D_MODEL), lambda i: (x_s_row0 // tm + jnp.maximum(i - npt, 0), 0))]
                 + [_resident_spec(a.shape) for a in weights] + [_const_spec(a.shape) for a in consts],
        out_specs=out_specs, out_shape=out_shape, scratch_shapes=scratch,
        compiler_params=_params(("arbitrary",)), name="merge_out_ln",
    )(*acts, x_p, x_s, *weights, *consts)


FFN_TILE = 256
FFN_SUB = 256


def _ffn_kernel(xb, xf, wg, wu, wd, lng, lnb, o_f, o_b):
    for r in range(FFN_TILE // FFN_SUB):
        rows = slice(r * FFN_SUB, (r + 1) * FFN_SUB)
        x = xb[rows, :]
        h = (_silu(_dot(x, wg[...])) * _dot(x, wu[...])).astype(BF16)
        y = _layer_norm(ALPHA * xf[rows, :] + _dot(h, wd[...]), lng[...], lnb[...])
        o_f[rows, :] = y
        o_b[rows, :] = y.astype(BF16)


def _resident_spec(shape):
    nd = len(shape)
    return pl.BlockSpec(shape, lambda *_: (0,) * nd, pipeline_mode=pl.Buffered(1))


def _ffn(xb, xf, wg, wu, wd, lng, lnb):
    tm = FFN_TILE
    row = lambda i: (i, 0)
    return pl.pallas_call(
        _ffn_kernel,
        grid=(N_TOK // tm,),
        in_specs=[pl.BlockSpec((tm, D_MODEL), row), pl.BlockSpec((tm, D_MODEL), row),
                  _const_spec(wg.shape), _const_spec(wu.shape), _const_spec(wd.shape),
                  _const_spec(lng.shape), _const_spec(lnb.shape)],
        out_specs=[pl.BlockSpec((tm, D_MODEL), row), pl.BlockSpec((tm, D_MODEL), row)],
        out_shape=[jax.ShapeDtypeStruct((N_TOK, D_MODEL), F32), jax.ShapeDtypeStruct((N_TOK, D_MODEL), BF16)],
        compiler_params=_params(("parallel",)), name="ffn_swiglu_ln",
    )(xb, xf, wg, wu, wd, lng, lnb)


def _route(x, w, b, strict_lower, count):
    xs = _split2(x)
    ws = _split2(w)
    logits = b + _dot(xs[0], ws[0]) + _dot(xs[0], ws[1]) + _dot(xs[1], ws[0])
    lane = lax.broadcasted_iota(jnp.int32, logits.shape, 1)
    logits = jnp.where(lane < N_EXPERTS, logits, -1e30)
    m1 = jnp.max(logits, axis=-1, keepdims=True)
    i1 = jnp.min(jnp.where(logits == m1, lane, LANES), axis=-1, keepdims=True)
    rest = jnp.where(lane == i1, -1e30, logits)
    m2 = jnp.max(rest, axis=-1, keepdims=True)
    i2 = jnp.min(jnp.where(rest == m2, lane, LANES), axis=-1, keepdims=True)
    e2 = jnp.exp(m2 - m1)
    den = 1.0 + e2

    hit1, hit2 = lane == i1, lane == i2
    assigned = jnp.where(hit1 | hit2, 1.0, 0.0)
    rank = _dot(strict_lower, assigned.astype(BF16)) + count[...]
    r1 = jnp.sum(jnp.where(hit1, rank, 0.0), axis=-1, keepdims=True).astype(jnp.int32)
    r2 = jnp.sum(jnp.where(hit2, rank, 0.0), axis=-1, keepdims=True).astype(jnp.int32)
    ri = jnp.where(lane == 0, i1, jnp.where(lane == 1, i2, jnp.where(lane == 2, r1, jnp.where(lane == 3, r2, 0))))
    rg = jnp.where(lane == 0, 1.0 / den, jnp.where(lane == 1, e2 / den, 0.0))
    count[...] = count[...] + jnp.sum(assigned, axis=0, keepdims=True)
    return ri, rg


EXPERT_TILE = 1024
N_SLOTS = N_TOK * 2 + N_EXPERTS * EXPERT_TILE
N_SLOT_TILES = N_SLOTS // EXPERT_TILE
MOE_TOK_TILE = 1024
MOE_FF_TILE = 512


def _row_copy(src, src_row, dst, dst_row, sem):
    return pltpu.make_async_copy(src.at[pl.ds(src_row, 1)], dst.at[pl.ds(dst_row, 1)], sem)


N_ZERO_TILES = 2 * N_EXPERTS


def _dispatch_kernel(zt_ref, zf_ref, s1_ref, s2_ref, x_ref, xs_out, zeros, sem, zsem):
    def zero_copy(k):
        dst = xs_out.at[pl.ds(pl.multiple_of(zt_ref[k] * EXPERT_TILE, EXPERT_TILE), EXPERT_TILE)]
        return pltpu.make_async_copy(zeros, dst, zsem)

    @pl.when(pl.program_id(0) == 0)
    def _():
        zeros[...] = jnp.zeros_like(zeros)
        for k in range(N_ZERO_TILES):
            @pl.when(zf_ref[k] == 1)
            def _():
                zero_copy(k).start()
        for k in range(N_ZERO_TILES):
            @pl.when(zf_ref[k] == 1)
            def _():
                zero_copy(k).wait()

    def body(r, carry):
        _row_copy(x_ref, r, xs_out, s1_ref[r], sem).start()
        _row_copy(x_ref, r, xs_out, s2_ref[r], sem).start()
        return carry

    lax.fori_loop(0, MOE_TOK_TILE, body, 0, unroll=8)
    for _ in range(2):
        pltpu.make_async_copy(x_ref, xs_out.at[pl.ds(0, MOE_TOK_TILE)], sem).wait()


def _dispatch(zero_tiles, zero_flags, slot1, slot2, x):
    tm = MOE_TOK_TILE
    smem = lambda: pl.BlockSpec((tm,), lambda i, zt, zf: (i,), memory_space=pltpu.SMEM)
    grid_spec = pltpu.PrefetchScalarGridSpec(
        num_scalar_prefetch=2,
        grid=(N_TOK // tm,),
        in_specs=[smem(), smem(), pl.BlockSpec((tm, D_MODEL), lambda i, zt, zf: (i, 0))],
        out_specs=pl.BlockSpec(memory_space=pl.ANY),
        scratch_shapes=[pltpu.VMEM((EXPERT_TILE, D_MODEL), F32), pltpu.SemaphoreType.DMA(()),
                        pltpu.SemaphoreType.DMA(())],
    )
    return pl.pallas_call(
        _dispatch_kernel, grid_spec=grid_spec,
        out_shape=jax.ShapeDtypeStruct((N_SLOTS, D_MODEL), F32),
        compiler_params=_params(("arbitrary",)), name="moe_dispatch",
    )(zero_tiles, zero_flags, slot1, slot2, x)


def _expert_ffn_kernel(te_ref, nv_ref, xs_ref, wg, wu, wd, ys_ref, xb_s):
    del te_ref
    t = pl.program_id(0)
    k = pl.program_id(1)

    @pl.when(t < nv_ref[0])
    def _():
        @pl.when(k == 0)
        def _():
            xb_s[...] = xs_ref[...].astype(BF16)
            ys_ref[...] = jnp.zeros_like(ys_ref)

        xb = xb_s[...]
        h = _silu(_dot(xb, wg[0, 0].astype(BF16))) * _dot(xb, wu[0, 0].astype(BF16))
        ys_ref[...] += _dot(h.astype(BF16), wd[0, 0].astype(BF16))

    @pl.when((t >= nv_ref[0]) & (k == 0))
    def _():
        ys_ref[...] = jnp.zeros_like(ys_ref)


def _expert_ffn(layer, tile_expert, n_valid, xs, wg, wu, wd):
    tm, tf = EXPERT_TILE, MOE_FF_TILE
    nk = E_FF // tf

    def tile(t, nv):
        return jnp.minimum(t, nv[0] - 1)

    def chunk(t, k, nv):
        return jnp.where(t < nv[0], k, nk - 1)

    grid_spec = pltpu.PrefetchScalarGridSpec(
        num_scalar_prefetch=2,
        grid=(N_SLOT_TILES, nk),
        in_specs=[pl.BlockSpec((tm, D_MODEL), lambda t, k, te, nv: (tile(t, nv), 0)),
                  pl.BlockSpec((1, 1, D_MODEL, tf), lambda t, k, te, nv: (layer, te[tile(t, nv)], 0, chunk(t, k, nv))),
                  pl.BlockSpec((1, 1, D_MODEL, tf), lambda t, k, te, nv: (layer, te[tile(t, nv)], 0, chunk(t, k, nv))),
                  pl.BlockSpec((1, 1, tf, D_MODEL), lambda t, k, te, nv: (layer, te[tile(t, nv)], chunk(t, k, nv), 0))],
        out_specs=pl.BlockSpec((tm, D_MODEL), lambda t, k, te, nv: (t, 0)),
        scratch_shapes=[pltpu.VMEM((tm, D_MODEL), BF16)],
    )
    return pl.pallas_call(
        _expert_ffn_kernel, grid_spec=grid_spec,
        out_shape=jax.ShapeDtypeStruct((N_SLOTS, D_MODEL), F32),
        compiler_params=_params(("arbitrary", "arbitrary")), name="moe_expert_ffn",
    )(tile_expert, n_valid, xs, wg, wu, wd)


def _combine_kernel(s1_ref, s2_ref, s1n_ref, s2n_ref, x_ref, rg_ref, ys_hbm, lng, lnb, o_p, o_s, buf, sem):
    tm = MOE_TOK_TILE
    i = pl.program_id(0)

    def gather(sa_ref, sb_ref, slot):
        def body(r, carry):
            _row_copy(ys_hbm, sa_ref[r], buf.at[slot, 0], r, sem.at[slot]).start()
            _row_copy(ys_hbm, sb_ref[r], buf.at[slot, 1], r, sem.at[slot]).start()
            return carry

        lax.fori_loop(0, tm, body, 0, unroll=8)

    @pl.when(i == 0)
    def _():
        gather(s1_ref, s2_ref, 0)

    @pl.when(i + 1 < pl.num_programs(0))
    def _():
        gather(s1n_ref, s2n_ref, (i + 1) % 2)

    slot = i % 2
    for j in range(2):
        pltpu.make_async_copy(ys_hbm.at[pl.ds(0, tm)], buf.at[slot, j], sem.at[slot]).wait()

    g = rg_ref[...]
    f = g[:, 0:1] * buf[slot, 0] + g[:, 1:2] * buf[slot, 1]
    y = _layer_norm(ALPHA * x_ref[...] + f, lng[...], lnb[...])

    @pl.when(i < N_PROMPT // tm)
    def _():
        o_p[...] = y

    @pl.when(i >= N_PROMPT // tm)
    def _():
        o_s[...] = y


def _combine(slot1, slot2, x, rg, ys, lng, lnb):
    tm = MOE_TOK_TILE
    npt = N_PROMPT // tm
    n_tiles = N_TOK // tm
    smem = lambda: pl.BlockSpec((tm,), lambda i: (i,), memory_space=pltpu.SMEM)
    smem_next = lambda: pl.BlockSpec((tm,), lambda i: (jnp.minimum(i + 1, n_tiles - 1),), memory_space=pltpu.SMEM)
    return pl.pallas_call(
        _combine_kernel,
        grid=(n_tiles,),
        in_specs=[smem(), smem(), smem_next(), smem_next(), pl.BlockSpec((tm, D_MODEL), lambda i: (i, 0)),
                  pl.BlockSpec((tm, LANES), lambda i: (i, 0)), pl.BlockSpec(memory_space=pl.ANY),
                  _const_spec(lng.shape), _const_spec(lnb.shape)],
        out_specs=[pl.BlockSpec((tm, D_MODEL), lambda i: (jnp.minimum(i, npt - 1), 0)),
                   pl.BlockSpec((tm, D_MODEL), lambda i: (jnp.maximum(i - npt, 0), 0))],
        out_shape=[jax.ShapeDtypeStruct((N_PROMPT, D_MODEL), F32), jax.ShapeDtypeStruct((N_SAMPLE, D_MODEL), F32)],
        scratch_shapes=[pltpu.VMEM((2, 2, tm, D_MODEL), F32), pltpu.SemaphoreType.DMA((2,))],
        compiler_params=_params(("arbitrary",)), name="moe_combine_ln",
    )(slot1, slot2, slot1, slot2, x, rg, ys, lng, lnb)


def _moe(layer, x, ri, rg, cnt, wg, wu, wd, lng, lnb):
    counts = cnt[0, :N_EXPERTS].astype(jnp.int32)
    padded = (counts + EXPERT_TILE - 1) // EXPERT_TILE * EXPERT_TILE
    ends = jnp.cumsum(padded)
    starts = ends - padded
    slot1 = starts[ri[:, 0]] + ri[:, 2]
    slot2 = starts[ri[:, 1]] + ri[:, 3]
    n_valid = (ends[-1:] // EXPERT_TILE).astype(jnp.int32)
    tile_start = jnp.arange(N_SLOT_TILES, dtype=jnp.int32) * EXPERT_TILE
    tile_expert = jnp.minimum(jnp.sum(tile_start[:, None] >= ends[None, :], axis=1), N_EXPERTS - 1).astype(jnp.int32)

    tail = n_valid[0] + jnp.arange(N_EXPERTS, dtype=jnp.int32)
    zero_tiles = jnp.concatenate([(ends // EXPERT_TILE - 1).astype(jnp.int32), tail])
    zero_flags = jnp.concatenate([padded > 0, tail < N_SLOT_TILES]).astype(jnp.int32)
    zero_tiles = jnp.where(zero_flags == 1, zero_tiles, 0)

    xs = _dispatch(zero_tiles, zero_flags, slot1, slot2, x)
    ys = _expert_ffn(layer, tile_expert, n_valid, xs, wg, wu, wd)
    return _combine(slot1, slot2, x, rg, ys, lng, lnb)


def _row(a):
    return a.reshape(1, -1).astype(F32)


def _pad_lanes(a):
    return jnp.pad(a, ((0, 0), (0, LANES - a.shape[1])))


def _gmlp_weights(w_s, b_s):
    tril = jnp.tril(jnp.ones((GM_CHUNK, GM_CHUNK), dtype=bool))
    w_p = jnp.where(tril, w_s, 0.0)
    n_seq = ROWS // DEC_SEQ
    w_8 = w_p[:, :DEC_SEQ, :DEC_SEQ]
    w_d = jnp.einsum("ab,gij->gaibj", jnp.eye(n_seq, dtype=F32), w_8).reshape(GM_GROUPS, ROWS, ROWS)
    bias_p = jnp.repeat(b_s.T, GM_GROUP, axis=1)
    bias_d = jnp.tile(bias_p[:DEC_SEQ], (n_seq, 1))
    return jnp.stack([w_p, w_d]).astype(BF16), jnp.stack([bias_p, bias_d])


def kernel(x_prompt, x_sample, mem_prompt, cache_mem_k, cache_mem_v, state_conv, state_ssm, w_in, conv_w, conv_b, dt_bias, a_log, d_skip, ssd_norm_g, v_ln_g, v_ln_b, w_s, b_s, p_gm, p_ssd, p_xa, w_out, w_mem_k, w_mem_v, ln1_g, ln1_b, ln2_g, ln2_b, ffn_wg, ffn_wu, ffn_wd, router_w, router_b, moe_wg, moe_wu, moe_wd):
    assert DEPTH % 2 == 0
    x_p = x_prompt.reshape(N_PROMPT, D_MODEL)
    x_s = x_sample.reshape(N_SAMPLE, D_MODEL)
    x_s_row0 = 0
    xb = jnp.concatenate([x_p.astype(BF16), x_s.astype(BF16)], axis=0)
    mem_b = mem_prompt.reshape(BATCH * N_MEM, D_MODEL).astype(BF16)
    w_in_t = jnp.swapaxes(w_in, 1, 2)
    st_all = state_ssm.reshape(DEPTH, DEC_BATCH, SSD_INNER, SSD_STATE)

    mem_k_out, mem_v_out, mem_kb, mem_vb = _mem_kv(mem_b, w_mem_k, w_mem_v)
    mem_kb = mem_kb.reshape(DEPTH * BATCH, N_MEM, XA_WIDTH)
    mem_vb = mem_vb.reshape(DEPTH * BATCH, N_MEM, XA_WIDTH)

    ssm_p_out, conv_p_out, conv_s_out, v_out = [], [], [], []
    ssm_s = None
    for i in range(DEPTH):
        (u,) = _mm(xb, w_in_t, i, 0, GM_WIDTH, 1024, [], _epi_gelu, [BF16], "in_u")
        (v,) = _mm(xb, w_in_t, i, GM_WIDTH, GM_WIDTH, 1024, [_row(v_ln_g[i]), _row(v_ln_b[i])], _epi_gelu_ln, [F32],
                   "in_v", tm=1024)
        (zs,) = _mm(xb, w_in_t, i, OFF_Z, SSD_INNER, 1024, [], _epi_silu, [BF16], "in_z")
        (xbc,) = _mm(xb, w_in_t, i, OFF_XBC, CONV_DIM, 1024, [], _epi_id, [BF16], "in_xbc")
        (dt,) = _mm(xb, w_in_t, i, OFF_DT, LANES, LANES, [_pad_lanes(_row(dt_bias[i]))], _epi_softplus, [F32], "in_dt")
        (q,) = _mm(xb, w_in_t, i, OFF_Q, XA_WIDTH, 1024, [], _epi_qscale, [BF16], "in_q")
        (gates,) = _mm(xb, w_in_t, i, OFF_GATE, N_BRANCH * D_MODEL, 1024, [], _epi_sigmoid, [BF16], "in_gates")

        gm_w, gm_b = _gmlp_weights(w_s[i], b_s[i])
        y_gm = _gmlp(v, u, gm_w, gm_b)

        alog = _pad_lanes(_row(a_log[i]))
        dskip = _row(jnp.repeat(d_skip[i], SSD_HEADDIM))
        ssd_args = (alog, conv_w[i], _row(conv_b[i]), dskip, _row(ssd_norm_g[i]))
        y_ssd, ssm_p = _ssd_prompt(xbc, dt, zs, *ssd_args)
        prev_rows = jnp.pad(state_conv[i], ((0, 0), (DEC_SEQ - (SSD_CONV - 1), 0), (0, 0))).reshape(N_SAMPLE, CONV_DIM)
        y_ssd, ssm_s = _ssd_sample(i, xbc, prev_rows, dt, zs, st_all, y_ssd, ssm_s, *ssd_args)
        ssm_p_out.append(ssm_p.reshape(BATCH, SSD_HEADS, SSD_HEADDIM, SSD_STATE))
        slots = xbc.reshape(N_TOK // DEC_SEQ, DEC_SEQ, CONV_DIM)
        n_p, per_seq, keep = N_PROMPT // DEC_SEQ, SEQ // DEC_SEQ, SSD_CONV - 1
        conv_p_out.append(lax.slice(slots, (per_seq - 1, DEC_SEQ - keep, 0), (n_p, DEC_SEQ, CONV_DIM),
                                    (per_seq, 1, 1)).astype(F32))
        conv_s_out.append(lax.slice(slots, (n_p, DEC_SEQ - keep, 0), slots.shape).astype(F32))
        v_out.append(v[N_PROMPT:].reshape(DEC_BATCH, DEC_SEQ, GM_WIDTH))

        y_xa = _xattn_prompt(i, q, mem_kb, mem_vb)
        y_xa = _xattn_sample(i, q, cache_mem_k, cache_mem_v, y_xa)

        j = i // 2
        merge_args = (y_gm, y_ssd, y_xa, gates, x_p, x_s, x_s_row0, p_gm[i].astype(BF16), p_ssd[i].astype(BF16),
                      p_xa[i].astype(BF16), w_out[i].astype(BF16), _row(ln1_g[i]), _row(ln1_b[i]))
        if i % 2 == 0:
            x, xb = _merge(*merge_args)
            x, xb = _ffn(xb, x, ffn_wg[j].astype(BF16), ffn_wu[j].astype(BF16), ffn_wd[j].astype(BF16),
                         _row(ln2_g[i]), _row(ln2_b[i]))
            x_p, x_s, x_s_row0 = x, x, N_PROMPT
        else:
            x, ri, rg, cnt = _merge(*merge_args, router=(_pad_lanes(router_w[j]), _pad_lanes(_row(router_b[j]))))
            x_p, x_s = _moe(j, x, ri, rg, cnt, moe_wg, moe_wu, moe_wd, _row(ln2_g[i]), _row(ln2_b[i]))
            x_s_row0 = 0
            if i + 1 < DEPTH:
                xb = jnp.concatenate([x_p.astype(BF16), x_s.astype(BF16)], axis=0)

    y_prompt = x_p.reshape(BATCH, SEQ, D_MODEL)
    y_sample = x_s.reshape(DEC_BATCH, DEC_SEQ, D_MODEL)
    ssm_s_out = ssm_s.reshape(DEPTH, DEC_BATCH, SSD_HEADS, SSD_HEADDIM, SSD_STATE)
    return (y_prompt, y_sample, mem_k_out, mem_v_out, jnp.stack(conv_p_out),
            jnp.stack(ssm_p_out), jnp.stack(conv_s_out), ssm_s_out, jnp.stack(v_out))
```

```python
import functools

import jax
import jax.numpy as jnp
from jax import lax
from jax.experimental import pallas as pl
from jax.experimental.pallas import tpu as pltpu

F32 = jnp.float32
BF16 = jnp.bfloat16

D_MODEL = 1024
BATCH = 8
SEQ = 2048
DEPTH = 2
DEC_BATCH = 128
DEC_SEQ = 8
N_MEM = 256
GM_WIDTH = D_MODEL
GM_CHUNK = 128
GM_GROUP = 128
GM_GROUPS = GM_WIDTH // GM_GROUP
SSD_INNER = 2 * D_MODEL
SSD_HEADDIM = 64
SSD_HEADS = SSD_INNER // SSD_HEADDIM
SSD_STATE = 128
SSD_GROUPS = 4
SSD_HPG = SSD_HEADS // SSD_GROUPS
SSD_CONV = 4
SSD_CHUNK = 128
CONV_DIM = SSD_INNER + 2 * SSD_GROUPS * SSD_STATE
XA_HEADS = 4
XA_HEADDIM = D_MODEL // XA_HEADS
XA_WIDTH = XA_HEADS * XA_HEADDIM
N_BRANCH = 3
D_FF = ((8 * D_MODEL // 3 + 127) // 128) * 128
N_EXPERTS = 8
E_FF = 7 * D_MODEL // 2
ALPHA = (2 * DEPTH) ** 0.25
LN_EPS = 1e-5

N_PROMPT = BATCH * SEQ
N_SAMPLE = DEC_BATCH * DEC_SEQ
N_TOK = N_PROMPT + N_SAMPLE

OFF_Z = 2 * GM_WIDTH
OFF_XBC = OFF_Z + SSD_INNER
OFF_DT = OFF_XBC + CONV_DIM
OFF_Q = OFF_DT + SSD_HEADS
OFF_GATE = OFF_Q + XA_WIDTH

LANES = 128
ROWS = 128
GROUP_W = SSD_HPG * SSD_HEADDIM
VMEM_LIMIT = 56 * 1024 * 1024


def _params(sem):
    return pltpu.CompilerParams(dimension_semantics=sem, vmem_limit_bytes=VMEM_LIMIT)


def _dot(a, b):
    return jnp.dot(a, b, preferred_element_type=F32)


def _dot_nt(a, b):
    return lax.dot_general(a, b, (((1,), (1,)), ((), ())), preferred_element_type=F32)


def _split2(x):
    hi = x.astype(BF16)
    lo = (x - hi.astype(F32)).astype(BF16)
    return hi, lo


def _split3(x):
    hi = x.astype(BF16)
    r = x - hi.astype(F32)
    mid = r.astype(BF16)
    lo = (r - mid.astype(F32)).astype(BF16)
    return hi, mid, lo


def _dot_exact_lhs(m_bf16, x, pieces=3):
    parts = _split3(x) if pieces == 3 else _split2(x)
    n = x.shape[1]
    out = _dot(m_bf16, jnp.concatenate(parts, axis=1))
    return sum(out[:, i * n:(i + 1) * n] for i in range(1, pieces)) + out[:, :n]


def _expand_rhs2(xs, m_bf16):
    rows = xs[0].shape[0]
    out = _dot(jnp.concatenate([p for x in xs for p in _split2(x)], axis=0), m_bf16)
    return [out[2 * i * rows:(2 * i + 1) * rows] + out[(2 * i + 1) * rows:(2 * i + 2) * rows] for i in range(len(xs))]


def _layer_norm(r, g, b):
    mu = jnp.mean(r, axis=-1, keepdims=True)
    c = r - mu
    var = jnp.mean(c * c, axis=-1, keepdims=True)
    return c * lax.rsqrt(var + LN_EPS) * g + b


def _gelu_tanh(x):
    return x * (0.5 * (1.0 + jnp.tanh(0.7978845608028654 * (x + 0.044715 * (x * x * x)))))


def _sigmoid(x):
    return 1.0 / (1.0 + jnp.exp(-x))


def _silu(x):
    return x * _sigmoid(x)


def _softplus(x):
    return jnp.maximum(x, 0.0) + jnp.log(1.0 + jnp.exp(-jnp.abs(x)))


def _mm_kernel(x_ref, wt_ref, *rest, epilogue, n_extra, n_out):
    extras = [r[...] for r in rest[:n_extra]]
    outs = rest[n_extra:n_extra + n_out]
    w_bf16 = rest[n_extra + n_out]

    @pl.when(pl.program_id(1) == 0)
    def _():
        w_bf16[...] = wt_ref[0].T.astype(BF16)

    acc = _dot(x_ref[...], w_bf16[...])
    res = epilogue(acc, *extras)
    for o, r in zip(outs, res):
        o[...] = r.astype(o.dtype)


MM_TILE = 2176


def _mm(x, wt, layer, row0, n_cols, tn, extras, epilogue, out_dtypes, name, tm=MM_TILE):
    t, k = x.shape
    assert row0 % 8 == 0
    grid = (n_cols // tn, t // tm)
    in_specs = [pl.BlockSpec((tm, k), lambda j, i: (i, 0)),
                pl.BlockSpec((pl.Element(1), pl.Element(tn), pl.Element(k)),
                             lambda j, i: (layer, pl.multiple_of(row0 + j * tn, 8), 0))]
    in_specs += [pl.BlockSpec((1, tn), lambda j, i: (0, j)) for _ in extras]
    out_specs = [pl.BlockSpec((tm, tn), lambda j, i: (i, j)) for _ in out_dtypes]
    out_shape = [jax.ShapeDtypeStruct((t, n_cols), d) for d in out_dtypes]
    return pl.pallas_call(
        functools.partial(_mm_kernel, epilogue=epilogue, n_extra=len(extras), n_out=len(out_dtypes)),
        grid=grid, in_specs=in_specs, out_specs=out_specs, out_shape=out_shape,
        scratch_shapes=[pltpu.VMEM((k, tn), BF16)],
        compiler_params=_params(("parallel", "arbitrary")), name=name,
    )(x, wt, *extras)


MEM_TILE_SEQS = 4


def _mem_kv_kernel(m_ref, wk_ref, wv_ref, k5_ref, v5_ref, kb_ref, vb_ref):
    m = m_ref[...]
    for w_ref, o5_ref, ob_ref in ((wk_ref, k5_ref, kb_ref), (wv_ref, v5_ref, vb_ref)):
        acc = _dot(m, w_ref[0].astype(BF16))
        ob_ref[0] = acc.astype(BF16)
        for b in range(MEM_TILE_SEQS):
            for h in range(XA_HEADS):
                o5_ref[0, b, :, h, :] = acc[b * N_MEM:(b + 1) * N_MEM, h * XA_HEADDIM:(h + 1) * XA_HEADDIM]


def _mem_kv(mem_b, w_mem_k, w_mem_v):
    tm = MEM_TILE_SEQS * N_MEM
    n5 = (DEPTH, BATCH, N_MEM, XA_HEADS, XA_HEADDIM)
    w_spec = pl.BlockSpec((1, D_MODEL, XA_WIDTH), lambda l, r: (l, 0, 0))
    o5_spec = pl.BlockSpec((1, MEM_TILE_SEQS, N_MEM, XA_HEADS, XA_HEADDIM), lambda l, r: (l, r, 0, 0, 0))
    ob_spec = pl.BlockSpec((1, tm, XA_WIDTH), lambda l, r: (l, r, 0))
    return pl.pallas_call(
        _mem_kv_kernel,
        grid=(DEPTH, BATCH // MEM_TILE_SEQS),
        in_specs=[pl.BlockSpec((tm, D_MODEL), lambda l, r: (r, 0)), w_spec, w_spec],
        out_specs=[o5_spec, o5_spec, ob_spec, ob_spec],
        out_shape=[jax.ShapeDtypeStruct(n5, F32), jax.ShapeDtypeStruct(n5, F32),
                   jax.ShapeDtypeStruct((DEPTH, BATCH * N_MEM, XA_WIDTH), BF16),
                   jax.ShapeDtypeStruct((DEPTH, BATCH * N_MEM, XA_WIDTH), BF16)],
        compiler_params=_params(("parallel", "parallel")), name="mem_kv",
    )(mem_b, w_mem_k, w_mem_v)


def _epi_gelu(acc):
    return (_gelu_tanh(acc),)


def _epi_gelu_ln(acc, g, b):
    return (_layer_norm(_gelu_tanh(acc), g, b),)


def _epi_silu(acc):
    return (_silu(acc),)


def _epi_id(acc):
    return (acc,)


def _epi_softplus(acc, bias):
    return (_softplus(acc + bias),)


def _epi_qscale(acc):
    return (acc * (XA_HEADDIM ** -0.5),)


def _epi_sigmoid(acc):
    return (_sigmoid(acc),)


GM_TILE = 512


def _gmlp_kernel(v_ref, u_ref, w_ref, b_ref, o_ref):
    for c in range(GM_TILE // ROWS):
        rs = slice(c * ROWS, (c + 1) * ROWS)
        for g in range(GM_GROUPS):
            cs = slice(g * GM_GROUP, (g + 1) * GM_GROUP)
            z = _dot(w_ref[0, g], v_ref[rs, cs].astype(BF16)) + b_ref[0, :, cs]
            o_ref[rs, cs] = (u_ref[rs, cs].astype(F32) * z).astype(o_ref.dtype)


def _gmlp(v, u, w2, b2):
    n_prompt_tiles = N_PROMPT // GM_TILE

    def sel(i):
        return jnp.where(i >= n_prompt_tiles, 1, 0)

    return pl.pallas_call(
        _gmlp_kernel,
        grid=(N_TOK // GM_TILE,),
        in_specs=[pl.BlockSpec((GM_TILE, GM_WIDTH), lambda i: (i, 0)),
                  pl.BlockSpec((GM_TILE, GM_WIDTH), lambda i: (i, 0)),
                  pl.BlockSpec((1, GM_GROUPS, ROWS, ROWS), lambda i: (sel(i), 0, 0, 0)),
                  pl.BlockSpec((1, ROWS, GM_WIDTH), lambda i: (sel(i), 0, 0))],
        out_specs=pl.BlockSpec((GM_TILE, GM_WIDTH), lambda i: (i, 0)),
        out_shape=jax.ShapeDtypeStruct((N_TOK, GM_WIDTH), BF16),
        compiler_params=_params(("parallel",)), name="gmlp_spatial",
    )(v, u, w2, b2)


def _conv_silu(x_back, conv_w, conv_b):
    conv = conv_b + x_back(0) * conv_w[3:4, :]
    for s in range(1, SSD_CONV):
        conv = conv + x_back(s) * conv_w[3 - s:4 - s, :]
    return _silu(conv)


def _ssd_block(xc, dt, alog, lmat, bones, emat, dskip):
    xs = xc[:, :SSD_INNER].astype(F32)
    bm = xc[:, SSD_INNER:SSD_INNER + SSD_GROUPS * SSD_STATE].astype(BF16)
    cm = xc[:, SSD_INNER + SSD_GROUPS * SSD_STATE:].astype(BF16)

    lane = lax.broadcasted_iota(jnp.int32, (1, LANES), 1)
    a_neg = jnp.where(lane < SSD_HEADS, -jnp.exp(alog), 0.0)
    adt = dt * a_neg
    sums = _dot_exact_lhs(jnp.concatenate([lmat, bones], axis=0), adt)
    a_cs, a_tot = sums[:ROWS], sums[ROWS:]
    a_cs_t = a_cs.T
    mask = lmat.astype(F32) > 0.5

    dt_x, eacs_x, te_x = _expand_rhs2([dt, jnp.exp(a_cs), jnp.exp(a_tot - a_cs)], emat)
    xdt = xs * dt_x
    xw_t = (xdt * te_x).T.astype(BF16)

    lane_r = lax.broadcasted_iota(jnp.int32, (ROWS, LANES), 1)
    lo_half = lane_r < SSD_HEADDIM
    yd = []
    for g in range(SSD_GROUPS):
        ns = slice(g * SSD_STATE, (g + 1) * SSD_STATE)
        cb = _dot_nt(cm[:, ns], bm[:, ns])
        for hp in range(SSD_HPG // 2):
            h0 = g * SSD_HPG + 2 * hp
            ms = []
            for h in (h0, h0 + 1):
                seg = a_cs[:, h:h + 1] - a_cs_t[h:h + 1, :]
                ms.append(cb * jnp.exp(jnp.where(mask, seg, -1e30)))
            lhs = jnp.concatenate(ms, axis=1).astype(BF16)
            xp = xdt[:, h0 * SSD_HEADDIM:(h0 + 2) * SSD_HEADDIM]
            rhs = jnp.concatenate([jnp.where(lo_half, xp, 0.0), jnp.where(lo_half, 0.0, xp)],
                                  axis=0).astype(BF16)
            yd.append(_dot(lhs, rhs))
    y_pre = jnp.concatenate(yd, axis=1) + xs * dskip
    return dict(y_pre=y_pre, eacs_x=eacs_x, cm=cm, bm=bm, xw_t=xw_t, a_tot=a_tot)


def _ssd_finish(y, zs, norm_g):
    y = y * zs
    outs = []
    for g in range(SSD_GROUPS):
        yg = y[:, g * GROUP_W:(g + 1) * GROUP_W]
        ms = jnp.mean(yg * yg, axis=-1, keepdims=True)
        outs.append(yg * lax.rsqrt(ms + LN_EPS))
    return jnp.concatenate(outs, axis=1) * norm_g


PROMPT_STEP_ROWS = 2 * ROWS


def _ssd_prompt_kernel(*refs):
    y_ref = refs[12]

    @pl.when(pl.program_id(0) < BATCH)
    def _():
        _ssd_prompt_body(*refs)

    @pl.when(pl.program_id(0) == BATCH)
    def _():
        y_ref[...] = jnp.zeros_like(y_ref)


def _ssd_prompt_body(xbc_ref, dt_ref, zs_ref, alog_ref, cw_ref, cb_ref, l_ref, ones_ref, e_ref, et_ref, dskip_ref,
                     ng_ref, y_ref, st_ref, xpad, state):
    @pl.when(pl.program_id(1) == 0)
    def _():
        xpad[0:8, :] = jnp.zeros((8, CONV_DIM), F32)
        state[...] = jnp.zeros_like(state)

    xpad[8:8 + PROMPT_STEP_ROWS, :] = xbc_ref[...].astype(F32)
    for j in range(PROMPT_STEP_ROWS // ROWS):
        rows = slice(j * ROWS, (j + 1) * ROWS)
        base = 8 + j * ROWS
        xc = _conv_silu(lambda s: xpad[base - s:base - s + ROWS, :], cw_ref[...], cb_ref[...])
        r = _ssd_block(xc, dt_ref[rows, :], alog_ref[...], l_ref[...], ones_ref[...], e_ref[...], dskip_ref[...])

        st = state[...]
        st_b = st.astype(BF16)
        y_off, s_new = [], []
        for g in range(SSD_GROUPS):
            ns = slice(g * SSD_STATE, (g + 1) * SSD_STATE)
            gs = slice(g * GROUP_W, (g + 1) * GROUP_W)
            y_off.append(_dot_nt(r["cm"][:, ns], st_b[gs, :]))
            s_new.append(_dot(r["xw_t"][gs, :], r["bm"][:, ns]))
        y = r["y_pre"] + jnp.concatenate(y_off, axis=1) * r["eacs_x"]
        y_ref[rows, :] = _ssd_finish(y, zs_ref[rows, :].astype(F32), ng_ref[...]).astype(y_ref.dtype)

        decay = _dot_exact_lhs(et_ref[...], jnp.exp(r["a_tot"].T), pieces=2)
        state[...] = decay * st + jnp.concatenate(s_new, axis=0)
    xpad[0:8, :] = xpad[PROMPT_STEP_ROWS:PROMPT_STEP_ROWS + 8, :]
    st_ref[0] = state[...]


def _ssd_consts(kind):
    tril = jnp.tril(jnp.ones((ROWS, ROWS), F32))
    if kind == "prompt":
        lmat, bones = tril, jnp.ones((ROWS, ROWS), F32)
    else:
        eye = jnp.eye(ROWS // DEC_SEQ, dtype=F32)
        blk = jnp.kron(eye, jnp.ones((DEC_SEQ, DEC_SEQ), F32))
        lmat, bones = tril * blk, blk
    head = jnp.arange(SSD_INNER) // SSD_HEADDIM
    emat = (jnp.arange(LANES)[:, None] == head[None, :]).astype(BF16)
    if kind == "prompt":
        return lmat.astype(BF16), bones.astype(BF16), emat, emat.T, None
    t = jnp.arange(ROWS)[:, None]
    col = jnp.arange(3 * ROWS)[None, :]
    shifts = []
    for s in range(1, SSD_CONV):
        inside = t % DEC_SEQ >= s
        earlier = (col % ROWS == t + DEC_SEQ - s) & (col >= ROWS)
        shifts.append(jnp.where(inside, col == t - s, earlier))
    shift = jnp.concatenate(shifts, axis=0).astype(BF16)
    return lmat.astype(BF16), bones.astype(BF16), emat, emat.T, shift


def _const_spec(shape):
    nd = len(shape)
    return pl.BlockSpec(shape, lambda *_: (0,) * nd)


def _ssd_prompt(xbc, dt, zs, alog, conv_w, conv_b, dskip, norm_g):
    lmat, bones, emat, emat_t, _ = _ssd_consts("prompt")
    tr = PROMPT_STEP_ROWS
    nc = SEQ // tr
    n_blk = N_PROMPT // tr
    row = lambda b, c: (jnp.minimum(b * nc + c, n_blk - 1), 0)
    row_out = lambda b, c: (jnp.where(b < BATCH, b * nc + c, n_blk + jnp.minimum(c, N_SAMPLE // tr - 1)), 0)
    consts = [alog, conv_w, conv_b, lmat, bones, emat, emat_t, dskip, norm_g]
    return pl.pallas_call(
        _ssd_prompt_kernel,
        grid=(BATCH + 1, nc),
        in_specs=[pl.BlockSpec((tr, CONV_DIM), row), pl.BlockSpec((tr, LANES), row),
                  pl.BlockSpec((tr, SSD_INNER), row)] + [_const_spec(a.shape) for a in consts],
        out_specs=[pl.BlockSpec((tr, SSD_INNER), row_out),
                   pl.BlockSpec((1, SSD_INNER, SSD_STATE), lambda b, c: (jnp.minimum(b, BATCH - 1), 0, 0))],
        out_shape=[jax.ShapeDtypeStruct((N_TOK, SSD_INNER), BF16),
                   jax.ShapeDtypeStruct((BATCH, SSD_INNER, SSD_STATE), F32)],
        scratch_shapes=[pltpu.VMEM((tr + 8, CONV_DIM), F32), pltpu.VMEM((SSD_INNER, SSD_STATE), F32)],
        compiler_params=_params(("arbitrary", "arbitrary")), name="ssd_prompt",
    )(xbc, dt, zs, *consts)


SEQ_PER_STEP = 4
Q_ROWS = SEQ_PER_STEP * DEC_SEQ
N_QUARTER = ROWS // Q_ROWS


def _ssd_sample_kernel(*refs, n_alias):
    (xbc_ref, prev_ref, dt_ref, zs_ref, st_in, alog_ref, cw_ref, cb_ref, sh_ref, l_ref, ones_ref, e_ref, et_ref,
     dskip_ref, ng_ref) = refs[:15]
    (y_ref, st_out, ypre_s, eacs_s, cm_s, bm_s, xwt_s, eat_s) = refs[15 + n_alias:]
    q = pl.program_id(1)

    @pl.when(q == 0)
    def _():
        x_cur = xbc_ref[...]
        p_hi, p_lo = _split2(prev_ref[...])
        back = _dot(sh_ref[...], jnp.concatenate([x_cur, p_hi, p_lo], axis=0))
        xc = _conv_silu(lambda s: x_cur.astype(F32) if s == 0 else back[(s - 1) * ROWS:s * ROWS, :],
                        cw_ref[...], cb_ref[...])
        r = _ssd_block(xc, dt_ref[...], alog_ref[...], l_ref[...], ones_ref[...], e_ref[...], dskip_ref[...])
        ypre_s[...] = r["y_pre"]
        eacs_s[...] = r["eacs_x"]
        cm_s[...] = r["cm"].astype(F32)
        bm_s[...] = r["bm"].astype(F32)
        xwt_s[...] = r["xw_t"]
        eat_s[...] = jnp.exp(r["a_tot"].T)

    q0 = pl.multiple_of(q * Q_ROWS, Q_ROWS)
    cq = cm_s[pl.ds(q0, Q_ROWS), :].astype(BF16)
    row_q = lax.broadcasted_iota(jnp.int32, (Q_ROWS, 1), 0) // DEC_SEQ
    row_b = lax.broadcasted_iota(jnp.int32, (ROWS, 1), 0) // DEC_SEQ
    lane_b = lax.broadcasted_iota(jnp.int32, (1, LANES), 1) // DEC_SEQ
    e_atot_t = eat_s[...]
    et = et_ref[...]
    y_off = jnp.zeros((Q_ROWS, SSD_INNER), F32)
    for s in range(SEQ_PER_STEP):
        seq = q * SEQ_PER_STEP + s
        st = st_in[0, s]
        st_b = st.astype(BF16)
        bsel = row_b == seq
        yo, s_new = [], []
        for g in range(SSD_GROUPS):
            ns = slice(g * SSD_STATE, (g + 1) * SSD_STATE)
            gs = slice(g * GROUP_W, (g + 1) * GROUP_W)
            yo.append(_dot_nt(cq[:, ns], st_b[gs, :]))
            bm_g = jnp.where(bsel, bm_s[:, ns], 0.0).astype(BF16)
            s_new.append(_dot(xwt_s[gs, :], bm_g))
        y_off = y_off + jnp.where(row_q == s, jnp.concatenate(yo, axis=1), 0.0)
        dec_col = jnp.sum(jnp.where(lane_b == seq, e_atot_t, 0.0), axis=1, keepdims=True) * (1.0 / DEC_SEQ)
        decay = _dot_exact_lhs(et, jnp.broadcast_to(dec_col, (LANES, SSD_STATE)), pieces=2)
        new_state = decay * st + jnp.concatenate(s_new, axis=0)
        for d in range(st_out.shape[0]):
            st_out[d, s] = new_state

    y = ypre_s[pl.ds(q0, Q_ROWS), :] + y_off * eacs_s[pl.ds(q0, Q_ROWS), :]
    y_ref[...] = _ssd_finish(y, zs_ref[...].astype(F32), ng_ref[...]).astype(y_ref.dtype)


def _ssd_sample(layer, xbc, prev_rows, dt, zs, st_all, y_all, st_out_prev, alog, conv_w, conv_b, dskip, norm_g):
    lmat, bones, emat, emat_t, shift = _ssd_consts("sample")
    blk0 = N_PROMPT // ROWS
    qblk0 = N_PROMPT // Q_ROWS
    consts = [alog, conv_w, conv_b, shift, lmat, bones, emat, emat_t, dskip, norm_g]
    st_spec = pl.BlockSpec((1, SEQ_PER_STEP, SSD_INNER, SSD_STATE), lambda b, q: (layer, b * N_QUARTER + q, 0, 0))
    if st_out_prev is None:
        assert layer == 0
        st_out_spec = pl.BlockSpec((DEPTH, SEQ_PER_STEP, SSD_INNER, SSD_STATE),
                                   lambda b, q: (0, b * N_QUARTER + q, 0, 0))
    else:
        st_out_spec = st_spec
    aliased = [y_all] + ([] if st_out_prev is None else [st_out_prev])
    n_in = 5 + len(consts)
    aliases = {n_in: 0} if st_out_prev is None else {n_in: 0, n_in + 1: 1}
    return pl.pallas_call(
        functools.partial(_ssd_sample_kernel, n_alias=len(aliased)),
        grid=(N_SAMPLE // ROWS, N_QUARTER),
        in_specs=[pl.BlockSpec((ROWS, CONV_DIM), lambda b, q: (blk0 + b, 0)),
                  pl.BlockSpec((ROWS, CONV_DIM), lambda b, q: (b, 0)),
                  pl.BlockSpec((ROWS, LANES), lambda b, q: (blk0 + b, 0)),
                  pl.BlockSpec((Q_ROWS, SSD_INNER), lambda b, q: (qblk0 + b * N_QUARTER + q, 0)),
                  st_spec] + [_const_spec(a.shape) for a in consts]
                 + [pl.BlockSpec(memory_space=pl.ANY) for _ in aliased],
        out_specs=[pl.BlockSpec((Q_ROWS, SSD_INNER), lambda b, q: (qblk0 + b * N_QUARTER + q, 0)), st_out_spec],
        out_shape=[jax.ShapeDtypeStruct((N_TOK, SSD_INNER), BF16),
                   jax.ShapeDtypeStruct((DEPTH, DEC_BATCH, SSD_INNER, SSD_STATE), F32)],
        scratch_shapes=[pltpu.VMEM((ROWS, SSD_INNER), F32), pltpu.VMEM((ROWS, SSD_INNER), F32),
                        pltpu.VMEM((ROWS, SSD_GROUPS * SSD_STATE), F32),
                        pltpu.VMEM((ROWS, SSD_GROUPS * SSD_STATE), F32),
                        pltpu.VMEM((SSD_INNER, ROWS), BF16), pltpu.VMEM((LANES, ROWS), F32)],
        input_output_aliases=aliases,
        compiler_params=_params(("arbitrary", "arbitrary")), name="ssd_sample",
    )(xbc, prev_rows, dt, zs, st_all, *consts, *aliased)


def _xattn_prompt_kernel(q_ref, k_ref, v_ref, o_ref):
    @pl.when(pl.program_id(0) < BATCH)
    def _():
        q = q_ref[...]
        for h in range(XA_HEADS):
            hs = slice(h * XA_HEADDIM, (h + 1) * XA_HEADDIM)
            sc = _dot_nt(q[:, hs], k_ref[0, :, hs])
            p = jnp.exp(sc - jnp.max(sc, axis=-1, keepdims=True))
            p = (p / jnp.sum(p, axis=-1, keepdims=True)).astype(BF16)
            o_ref[:, hs] = _dot(p, v_ref[0, :, hs]).astype(o_ref.dtype)

    @pl.when(pl.program_id(0) == BATCH)
    def _():
        o_ref[...] = jnp.zeros_like(o_ref)


XA_Q_TILE = 512
XA_SAMPLE_SEQS = 4
XA_SAMPLE_ROWS = XA_SAMPLE_SEQS * DEC_SEQ


def _xattn_prompt(layer, q, mem_k, mem_v):
    nq = SEQ // XA_Q_TILE
    n_blk = N_PROMPT // XA_Q_TILE
    kv_spec = pl.BlockSpec((1, N_MEM, XA_WIDTH), lambda b, j: (layer * BATCH + jnp.minimum(b, BATCH - 1), 0, 0))
    row_out = lambda b, j: (jnp.where(b < BATCH, b * nq + j, n_blk + jnp.minimum(j, N_SAMPLE // XA_Q_TILE - 1)), 0)
    return pl.pallas_call(
        _xattn_prompt_kernel,
        grid=(BATCH + 1, nq),
        in_specs=[pl.BlockSpec((XA_Q_TILE, XA_WIDTH), lambda b, j: (jnp.minimum(b * nq + j, n_blk - 1), 0)),
                  kv_spec, kv_spec],
        out_specs=pl.BlockSpec((XA_Q_TILE, XA_WIDTH), row_out),
        out_shape=jax.ShapeDtypeStruct((N_TOK, XA_WIDTH), BF16),
        compiler_params=_params(("arbitrary", "arbitrary")), name="xattn_prompt",
    )(q, mem_k, mem_v)


def _xattn_sample_kernel(q_ref, k_ref, v_ref, y_all, o_ref):
    del y_all
    rows = XA_SAMPLE_ROWS
    q = q_ref[...]
    qblk = jnp.concatenate([q[:, h * XA_HEADDIM:(h + 1) * XA_HEADDIM] for h in range(XA_HEADS)], axis=0)
    mem_head = lax.broadcasted_iota(jnp.int32, (N_MEM * XA_HEADS, 1), 0) % XA_HEADS
    col_head = lax.broadcasted_iota(jnp.int32, (1, XA_HEADS * rows), 1) // rows
    same_head = mem_head == col_head
    row_seq = (lax.broadcasted_iota(jnp.int32, (XA_HEADS * rows, 1), 0) % rows) // DEC_SEQ
    out = jnp.zeros((XA_HEADS * rows, XA_HEADDIM), F32)
    for s in range(XA_SAMPLE_SEQS):
        k2 = k_ref[0, s].reshape(N_MEM * XA_HEADS, XA_HEADDIM).astype(BF16)
        v2 = v_ref[0, s].reshape(N_MEM * XA_HEADS, XA_HEADDIM).astype(BF16)
        sc = jnp.where(same_head, _dot_nt(k2, qblk), -1e30)
        p = jnp.exp(sc - jnp.max(sc, axis=0, keepdims=True))
        p = (p / jnp.sum(p, axis=0, keepdims=True)).astype(BF16)
        y = lax.dot_general(p, v2, (((0,), (0,)), ((), ())), preferred_element_type=F32)
        out = jnp.where(row_seq == s, y, out)
    for h in range(XA_HEADS):
        o_ref[:, h * XA_HEADDIM:(h + 1) * XA_HEADDIM] = out[h * rows:(h + 1) * rows].astype(o_ref.dtype)


def _xattn_sample(layer, q, cache_k, cache_v, y_all):
    rows = XA_SAMPLE_ROWS
    blk0 = N_PROMPT // rows
    kv_spec = pl.BlockSpec((1, XA_SAMPLE_SEQS, N_MEM, XA_HEADS, XA_HEADDIM), lambda j: (layer, j, 0, 0, 0))
    return pl.pallas_call(
        _xattn_sample_kernel,
        grid=(DEC_BATCH // XA_SAMPLE_SEQS,),
        in_specs=[pl.BlockSpec((rows, XA_WIDTH), lambda j: (blk0 + j, 0)), kv_spec, kv_spec,
                  pl.BlockSpec(memory_space=pl.ANY)],
        out_specs=pl.BlockSpec((rows, XA_WIDTH), lambda j: (blk0 + j, 0)),
        out_shape=jax.ShapeDtypeStruct((N_TOK, XA_WIDTH), BF16),
        input_output_aliases={3: 0},
        compiler_params=_params(("parallel",)), name="xattn_sample",
    )(q, cache_k, cache_v, y_all)


MERGE_TILE = 512
MERGE_SUB = 256


def _merge_kernel(*refs, route):
    ygm, yssd, yxa, gates, x_p, x_s, pgm, pssd, pxa, wout, lng, lnb = refs[:12]
    if route:
        rw, rb, lower, o_f, ri_ref, rg_ref, cnt_ref, count = refs[12:]

        @pl.when(pl.program_id(0) == 0)
        def _():
            count[...] = jnp.zeros_like(count)
    else:
        o_f, o_b = refs[12:]
    is_prompt = pl.program_id(0) < N_PROMPT // MERGE_TILE
    for r in range(MERGE_TILE // MERGE_SUB):
        rows = slice(r * MERGE_SUB, (r + 1) * MERGE_SUB)
        g = gates[rows, :].astype(F32)
        m = _dot(ygm[rows, :], pgm[...]) * g[:, :D_MODEL]
        m = m + _dot(yssd[rows, :], pssd[...]) * g[:, D_MODEL:2 * D_MODEL]
        m = m + _dot(yxa[rows, :], pxa[...]) * g[:, 2 * D_MODEL:]
        h = _dot(m.astype(BF16), wout[...])
        x = jnp.where(is_prompt, x_p[rows, :], x_s[rows, :])
        y = _layer_norm(ALPHA * x + h, lng[...], lnb[...])
        o_f[rows, :] = y
        if route:
            ri_ref[rows, :], rg_ref[rows, :] = _route(y, rw[...], rb[...], lower[...], count)
        else:
            o_b[rows, :] = y.astype(BF16)
    if route:
        cnt_ref[...] = count[...]


def _merge(ygm, yssd, yxa, gates, x_p, x_s, x_s_row0, pgm, pssd, pxa, wout, lng, lnb, router=None):
    tm = MERGE_TILE
    npt = N_PROMPT // tm
    row = lambda i: (i, 0)
    acts = [ygm, yssd, yxa, gates]
    weights = [pgm, pssd, pxa, wout]
    consts = [lng, lnb]
    out_specs = [pl.BlockSpec((tm, D_MODEL), row)]
    out_shape = [jax.ShapeDtypeStruct((N_TOK, D_MODEL), F32)]
    scratch = []
    if router is None:
        out_specs.append(pl.BlockSpec((tm, D_MODEL), row))
        out_shape.append(jax.ShapeDtypeStruct((N_TOK, D_MODEL), BF16))
    else:
        consts += [*router, jnp.tril(jnp.ones((MERGE_SUB, MERGE_SUB), F32), k=-1).astype(BF16)]
        out_specs += [pl.BlockSpec((tm, LANES), row), pl.BlockSpec((tm, LANES), row), _const_spec((1, LANES))]
        out_shape += [jax.ShapeDtypeStruct((N_TOK, LANES), jnp.int32), jax.ShapeDtypeStruct((N_TOK, LANES), F32),
                      jax.ShapeDtypeStruct((1, LANES), F32)]
        scratch.append(pltpu.VMEM((1, LANES), F32))
    return pl.pallas_call(
        functools.partial(_merge_kernel, route=router is not None),
        grid=(N_TOK // tm,),
        in_specs=[pl.BlockSpec((tm, a.shape[1]), row) for a in acts]
                 + [pl.BlockSpec((tm, D_MODEL), lambda i: (jnp.minimum(i, npt - 1), 0)),
                    pl.BlockSpec((tm, D_MODEL), lambda i: (x_s_row0 // tm + jnp.maximum(i - npt, 0), 0))]
                 + [_resident_spec(a.shape) for a in weights] + [_const_spec(a.shape) for a in consts],
        out_specs=out_specs, out_shape=out_shape, scratch_shapes=scratch,
        compiler_params=_params(("arbitrary",)), name="merge_out_ln",
    )(*acts, x_p, x_s, *weights, *consts)


FFN_TILE = 512
FFN_SUB = 256


def _ffn_kernel(xb, xf, wg, wu, wd, lng, lnb, o_f, o_b):
    for r in range(FFN_TILE // FFN_SUB):
        rows = slice(r * FFN_SUB, (r + 1) * FFN_SUB)
        x = xb[rows, :]
        h = (_silu(_dot(x, wg[...])) * _dot(x, wu[...])).astype(BF16)
        y = _layer_norm(ALPHA * xf[rows, :] + _dot(h, wd[...]), lng[...], lnb[...])
        o_f[rows, :] = y
        o_b[rows, :] = y.astype(BF16)


def _resident_spec(shape):
    nd = len(shape)
    return pl.BlockSpec(shape, lambda *_: (0,) * nd, pipeline_mode=pl.Buffered(1))


def _ffn(xb, xf, wg, wu, wd, lng, lnb):
    tm = FFN_TILE
    row = lambda i: (i, 0)
    return pl.pallas_call(
        _ffn_kernel,
        grid=(N_TOK // tm,),
        in_specs=[pl.BlockSpec((tm, D_MODEL), row), pl.BlockSpec((tm, D_MODEL), row),
                  _resident_spec(wg.shape), _resident_spec(wu.shape), _resident_spec(wd.shape),
                  _const_spec(lng.shape), _const_spec(lnb.shape)],
        out_specs=[pl.BlockSpec((tm, D_MODEL), row), pl.BlockSpec((tm, D_MODEL), row)],
        out_shape=[jax.ShapeDtypeStruct((N_TOK, D_MODEL), F32), jax.ShapeDtypeStruct((N_TOK, D_MODEL), BF16)],
        compiler_params=_params(("parallel",)), name="ffn_swiglu_ln",
    )(xb, xf, wg, wu, wd, lng, lnb)


def _route(x, w, b, strict_lower, count):
    xs = _split2(x)
    ws = _split2(w)
    logits = b + _dot(xs[0], ws[0]) + _dot(xs[0], ws[1]) + _dot(xs[1], ws[0])
    lane = lax.broadcasted_iota(jnp.int32, logits.shape, 1)
    logits = jnp.where(lane < N_EXPERTS, logits, -1e30)
    m1 = jnp.max(logits, axis=-1, keepdims=True)
    i1 = jnp.min(jnp.where(logits == m1, lane, LANES), axis=-1, keepdims=True)
    rest = jnp.where(lane == i1, -1e30, logits)
    m2 = jnp.max(rest, axis=-1, keepdims=True)
    i2 = jnp.min(jnp.where(rest == m2, lane, LANES), axis=-1, keepdims=True)
    e2 = jnp.exp(m2 - m1)
    den = 1.0 + e2

    hit1, hit2 = lane == i1, lane == i2
    assigned = jnp.where(hit1 | hit2, 1.0, 0.0)
    rank = _dot(strict_lower, assigned.astype(BF16)) + count[...]
    r1 = jnp.sum(jnp.where(hit1, rank, 0.0), axis=-1, keepdims=True).astype(jnp.int32)
    r2 = jnp.sum(jnp.where(hit2, rank, 0.0), axis=-1, keepdims=True).astype(jnp.int32)
    ri = jnp.where(lane == 0, i1, jnp.where(lane == 1, i2, jnp.where(lane == 2, r1, jnp.where(lane == 3, r2, 0))))
    rg = jnp.where(lane == 0, 1.0 / den, jnp.where(lane == 1, e2 / den, 0.0))
    count[...] = count[...] + jnp.sum(assigned, axis=0, keepdims=True)
    return ri, rg


EXPERT_TILE = 1024
N_SLOTS = N_TOK * 2 + N_EXPERTS * EXPERT_TILE
N_SLOT_TILES = N_SLOTS // EXPERT_TILE
MOE_TOK_TILE = 1024
MOE_FF_TILE = 512


def _row_copy(src, src_row, dst, dst_row, sem):
    return pltpu.make_async_copy(src.at[pl.ds(src_row, 1)], dst.at[pl.ds(dst_row, 1)], sem)


N_ZERO_TILES = 2 * N_EXPERTS


def _dispatch_kernel(zt_ref, zf_ref, s1_ref, s2_ref, x_ref, xs_out, zeros, sem, zsem):
    def zero_copy(k):
        dst = xs_out.at[pl.ds(pl.multiple_of(zt_ref[k] * EXPERT_TILE, EXPERT_TILE), EXPERT_TILE)]
        return pltpu.make_async_copy(zeros, dst, zsem)

    @pl.when(pl.program_id(0) == 0)
    def _():
        zeros[...] = jnp.zeros_like(zeros)
        for k in range(N_ZERO_TILES):
            @pl.when(zf_ref[k] == 1)
            def _():
                zero_copy(k).start()
        for k in range(N_ZERO_TILES):
            @pl.when(zf_ref[k] == 1)
            def _():
                zero_copy(k).wait()

    def body(r, carry):
        _row_copy(x_ref, r, xs_out, s1_ref[r], sem).start()
        _row_copy(x_ref, r, xs_out, s2_ref[r], sem).start()
        return carry

    lax.fori_loop(0, MOE_TOK_TILE, body, 0, unroll=8)
    for _ in range(2):
        pltpu.make_async_copy(x_ref, xs_out.at[pl.ds(0, MOE_TOK_TILE)], sem).wait()


def _dispatch(zero_tiles, zero_flags, slot1, slot2, x):
    tm = MOE_TOK_TILE
    smem = lambda: pl.BlockSpec((tm,), lambda i, zt, zf: (i,), memory_space=pltpu.SMEM)
    grid_spec = pltpu.PrefetchScalarGridSpec(
        num_scalar_prefetch=2,
        grid=(N_TOK // tm,),
        in_specs=[smem(), smem(), pl.BlockSpec((tm, D_MODEL), lambda i, zt, zf: (i, 0))],
        out_specs=pl.BlockSpec(memory_space=pl.ANY),
        scratch_shapes=[pltpu.VMEM((EXPERT_TILE, D_MODEL), F32), pltpu.SemaphoreType.DMA(()),
                        pltpu.SemaphoreType.DMA(())],
    )
    return pl.pallas_call(
        _dispatch_kernel, grid_spec=grid_spec,
        out_shape=jax.ShapeDtypeStruct((N_SLOTS, D_MODEL), F32),
        compiler_params=_params(("arbitrary",)), name="moe_dispatch",
    )(zero_tiles, zero_flags, slot1, slot2, x)


def _expert_ffn_kernel(te_ref, nv_ref, xs_ref, wg, wu, wd, ys_ref, xb_s):
    del te_ref
    t = pl.program_id(0)
    k = pl.program_id(1)

    @pl.when(t < nv_ref[0])
    def _():
        @pl.when(k == 0)
        def _():
            xb_s[...] = xs_ref[...].astype(BF16)
            ys_ref[...] = jnp.zeros_like(ys_ref)

        xb = xb_s[...]
        h = _silu(_dot(xb, wg[0, 0].astype(BF16))) * _dot(xb, wu[0, 0].astype(BF16))
        ys_ref[...] += _dot(h.astype(BF16), wd[0, 0].astype(BF16))

    @pl.when((t >= nv_ref[0]) & (k == 0))
    def _():
        ys_ref[...] = jnp.zeros_like(ys_ref)


def _expert_ffn(layer, tile_expert, n_valid, xs, wg, wu, wd):
    tm, tf = EXPERT_TILE, MOE_FF_TILE
    nk = E_FF // tf

    def tile(t, nv):
        return jnp.minimum(t, nv[0] - 1)

    def chunk(t, k, nv):
        return jnp.where(t < nv[0], k, nk - 1)

    grid_spec = pltpu.PrefetchScalarGridSpec(
        num_scalar_prefetch=2,
        grid=(N_SLOT_TILES, nk),
        in_specs=[pl.BlockSpec((tm, D_MODEL), lambda t, k, te, nv: (tile(t, nv), 0)),
                  pl.BlockSpec((1, 1, D_MODEL, tf), lambda t, k, te, nv: (layer, te[tile(t, nv)], 0, chunk(t, k, nv))),
                  pl.BlockSpec((1, 1, D_MODEL, tf), lambda t, k, te, nv: (layer, te[tile(t, nv)], 0, chunk(t, k, nv))),
                  pl.BlockSpec((1, 1, tf, D_MODEL), lambda t, k, te, nv: (layer, te[tile(t, nv)], chunk(t, k, nv), 0))],
        out_specs=pl.BlockSpec((tm, D_MODEL), lambda t, k, te, nv: (t, 0)),
        scratch_shapes=[pltpu.VMEM((tm, D_MODEL), BF16)],
    )
    return pl.pallas_call(
        _expert_ffn_kernel, grid_spec=grid_spec,
        out_shape=jax.ShapeDtypeStruct((N_SLOTS, D_MODEL), F32),
        compiler_params=_params(("arbitrary", "arbitrary")), name="moe_expert_ffn",
    )(tile_expert, n_valid, xs, wg, wu, wd)


def _combine_kernel(s1_ref, s2_ref, s1n_ref, s2n_ref, x_ref, rg_ref, ys_hbm, lng, lnb, o_p, o_s, buf, sem):
    tm = MOE_TOK_TILE
    i = pl.program_id(0)

    def gather(sa_ref, sb_ref, slot):
        def body(r, carry):
            _row_copy(ys_hbm, sa_ref[r], buf.at[slot, 0], r, sem.at[slot]).start()
            _row_copy(ys_hbm, sb_ref[r], buf.at[slot, 1], r, sem.at[slot]).start()
            return carry

        lax.fori_loop(0, tm, body, 0, unroll=8)

    @pl.when(i == 0)
    def _():
        gather(s1_ref, s2_ref, 0)

    @pl.when(i + 1 < pl.num_programs(0))
    def _():
        gather(s1n_ref, s2n_ref, (i + 1) % 2)

    slot = i % 2
    for j in range(2):
        pltpu.make_async_copy(ys_hbm.at[pl.ds(0, tm)], buf.at[slot, j], sem.at[slot]).wait()

    g = rg_ref[...]
    f = g[:, 0:1] * buf[slot, 0] + g[:, 1:2] * buf[slot, 1]
    y = _layer_norm(ALPHA * x_ref[...] + f, lng[...], lnb[...])

    @pl.when(i < N_PROMPT // tm)
    def _():
        o_p[...] = y

    @pl.when(i >= N_PROMPT // tm)
    def _():
        o_s[...] = y


def _combine(slot1, slot2, x, rg, ys, lng, lnb):
    tm = MOE_TOK_TILE
    npt = N_PROMPT // tm
    n_tiles = N_TOK // tm
    smem = lambda: pl.BlockSpec((tm,), lambda i: (i,), memory_space=pltpu.SMEM)
    smem_next = lambda: pl.BlockSpec((tm,), lambda i: (jnp.minimum(i + 1, n_tiles - 1),), memory_space=pltpu.SMEM)
    return pl.pallas_call(
        _combine_kernel,
        grid=(n_tiles,),
        in_specs=[smem(), smem(), smem_next(), smem_next(), pl.BlockSpec((tm, D_MODEL), lambda i: (i, 0)),
                  pl.BlockSpec((tm, LANES), lambda i: (i, 0)), pl.BlockSpec(memory_space=pl.ANY),
                  _const_spec(lng.shape), _const_spec(lnb.shape)],
        out_specs=[pl.BlockSpec((tm, D_MODEL), lambda i: (jnp.minimum(i, npt - 1), 0)),
                   pl.BlockSpec((tm, D_MODEL), lambda i: (jnp.maximum(i - npt, 0), 0))],
        out_shape=[jax.ShapeDtypeStruct((N_PROMPT, D_MODEL), F32), jax.ShapeDtypeStruct((N_SAMPLE, D_MODEL), F32)],
        scratch_shapes=[pltpu.VMEM((2, 2, tm, D_MODEL), F32), pltpu.SemaphoreType.DMA((2,))],
        compiler_params=_params(("arbitrary",)), name="moe_combine_ln",
    )(slot1, slot2, slot1, slot2, x, rg, ys, lng, lnb)


def _moe(layer, x, ri, rg, cnt, wg, wu, wd, lng, lnb):
    counts = cnt[0, :N_EXPERTS].astype(jnp.int32)
    padded = (counts + EXPERT_TILE - 1) // EXPERT_TILE * EXPERT_TILE
    ends = jnp.cumsum(padded)
    starts = ends - padded
    slot1 = starts[ri[:, 0]] + ri[:, 2]
    slot2 = starts[ri[:, 1]] + ri[:, 3]
    n_valid = (ends[-1:] // EXPERT_TILE).astype(jnp.int32)
    tile_start = jnp.arange(N_SLOT_TILES, dtype=jnp.int32) * EXPERT_TILE
    tile_expert = jnp.minimum(jnp.sum(tile_start[:, None] >= ends[None, :], axis=1), N_EXPERTS - 1).astype(jnp.int32)

    tail = n_valid[0] + jnp.arange(N_EXPERTS, dtype=jnp.int32)
    zero_tiles = jnp.concatenate([(ends // EXPERT_TILE - 1).astype(jnp.int32), tail])
    zero_flags = jnp.concatenate([padded > 0, tail < N_SLOT_TILES]).astype(jnp.int32)
    zero_tiles = jnp.where(zero_flags == 1, zero_tiles, 0)

    xs = _dispatch(zero_tiles, zero_flags, slot1, slot2, x)
    ys = _expert_ffn(layer, tile_expert, n_valid, xs, wg, wu, wd)
    return _combine(slot1, slot2, x, rg, ys, lng, lnb)


CAST_TILE = 1024


def _cast_rows_kernel(xp_ref, xs_ref, o_ref):
    x = jnp.where(pl.program_id(0) < N_PROMPT // CAST_TILE, xp_ref[...], xs_ref[...])
    o_ref[...] = x.astype(o_ref.dtype)


def _cast_rows(x_p, x_s):
    tm = CAST_TILE
    npt = N_PROMPT // tm
    return pl.pallas_call(
        _cast_rows_kernel,
        grid=(N_TOK // tm,),
        in_specs=[pl.BlockSpec((tm, D_MODEL), lambda i: (jnp.minimum(i, npt - 1), 0)),
                  pl.BlockSpec((tm, D_MODEL), lambda i: (jnp.maximum(i - npt, 0), 0))],
        out_specs=pl.BlockSpec((tm, D_MODEL), lambda i: (i, 0)),
        out_shape=jax.ShapeDtypeStruct((N_TOK, D_MODEL), BF16),
        compiler_params=_params(("parallel",)), name="cast_rows",
    )(x_p, x_s)


def _row(a):
    return a.reshape(1, -1).astype(F32)


def _pad_lanes(a):
    return jnp.pad(a, ((0, 0), (0, LANES - a.shape[1])))


def _gmlp_weights(w_s, b_s):
    tril = jnp.tril(jnp.ones((GM_CHUNK, GM_CHUNK), dtype=bool))
    w_p = jnp.where(tril, w_s, 0.0)
    n_seq = ROWS // DEC_SEQ
    w_8 = w_p[:, :DEC_SEQ, :DEC_SEQ]
    w_d = jnp.einsum("ab,gij->gaibj", jnp.eye(n_seq, dtype=F32), w_8).reshape(GM_GROUPS, ROWS, ROWS)
    bias_p = jnp.repeat(b_s.T, GM_GROUP, axis=1)
    bias_d = jnp.tile(bias_p[:DEC_SEQ], (n_seq, 1))
    return jnp.stack([w_p, w_d]).astype(BF16), jnp.stack([bias_p, bias_d])


def kernel(x_prompt, x_sample, mem_prompt, cache_mem_k, cache_mem_v, state_conv, state_ssm, w_in, conv_w, conv_b, dt_bias, a_log, d_skip, ssd_norm_g, v_ln_g, v_ln_b, w_s, b_s, p_gm, p_ssd, p_xa, w_out, w_mem_k, w_mem_v, ln1_g, ln1_b, ln2_g, ln2_b, ffn_wg, ffn_wu, ffn_wd, router_w, router_b, moe_wg, moe_wu, moe_wd):
    assert DEPTH % 2 == 0
    x_p = x_prompt.reshape(N_PROMPT, D_MODEL)
    x_s = x_sample.reshape(N_SAMPLE, D_MODEL)
    x_s_row0 = 0
    xb = _cast_rows(x_p, x_s)
    mem_b = mem_prompt.reshape(BATCH * N_MEM, D_MODEL).astype(BF16)
    w_in_t = jnp.swapaxes(w_in, 1, 2)
    st_all = state_ssm.reshape(DEPTH, DEC_BATCH, SSD_INNER, SSD_STATE)

    mem_k_out, mem_v_out, mem_kb, mem_vb = _mem_kv(mem_b, w_mem_k, w_mem_v)
    mem_kb = mem_kb.reshape(DEPTH * BATCH, N_MEM, XA_WIDTH)
    mem_vb = mem_vb.reshape(DEPTH * BATCH, N_MEM, XA_WIDTH)

    ssm_p_out, conv_p_out, conv_s_out, v_out = [], [], [], []
    ssm_s = None
    for i in range(DEPTH):
        (u,) = _mm(xb, w_in_t, i, 0, GM_WIDTH, 1024, [], _epi_gelu, [BF16], "in_u")
        (v,) = _mm(xb, w_in_t, i, GM_WIDTH, GM_WIDTH, 1024, [_row(v_ln_g[i]), _row(v_ln_b[i])], _epi_gelu_ln, [F32],
                   "in_v", tm=1024)
        (zs,) = _mm(xb, w_in_t, i, OFF_Z, SSD_INNER, 1024, [], _epi_silu, [BF16], "in_z")
        (xbc,) = _mm(xb, w_in_t, i, OFF_XBC, CONV_DIM, 1024, [], _epi_id, [BF16], "in_xbc")
        (dt,) = _mm(xb, w_in_t, i, OFF_DT, LANES, LANES, [_pad_lanes(_row(dt_bias[i]))], _epi_softplus, [F32], "in_dt")
        (q,) = _mm(xb, w_in_t, i, OFF_Q, XA_WIDTH, 1024, [], _epi_qscale, [BF16], "in_q")
        (gates,) = _mm(xb, w_in_t, i, OFF_GATE, N_BRANCH * D_MODEL, 1024, [], _epi_sigmoid, [BF16], "in_gates")

        gm_w, gm_b = _gmlp_weights(w_s[i], b_s[i])
        y_gm = _gmlp(v, u, gm_w, gm_b)

        alog = _pad_lanes(_row(a_log[i]))
        dskip = _row(jnp.repeat(d_skip[i], SSD_HEADDIM))
        ssd_args = (alog, conv_w[i], _row(conv_b[i]), dskip, _row(ssd_norm_g[i]))
        y_ssd, ssm_p = _ssd_prompt(xbc, dt, zs, *ssd_args)
        prev_rows = jnp.pad(state_conv[i], ((0, 0), (DEC_SEQ - (SSD_CONV - 1), 0), (0, 0))).reshape(N_SAMPLE, CONV_DIM)
        y_ssd, ssm_s = _ssd_sample(i, xbc, prev_rows, dt, zs, st_all, y_ssd, ssm_s, *ssd_args)
        ssm_p_out.append(ssm_p.reshape(BATCH, SSD_HEADS, SSD_HEADDIM, SSD_STATE))
        slots = xbc.reshape(N_TOK // DEC_SEQ, DEC_SEQ, CONV_DIM)
        n_p, per_seq, keep = N_PROMPT // DEC_SEQ, SEQ // DEC_SEQ, SSD_CONV - 1
        conv_p_out.append(lax.slice(slots, (per_seq - 1, DEC_SEQ - keep, 0), (n_p, DEC_SEQ, CONV_DIM),
                                    (per_seq, 1, 1)).astype(F32))
        conv_s_out.append(lax.slice(slots, (n_p, DEC_SEQ - keep, 0), slots.shape).astype(F32))
        v_out.append(v[N_PROMPT:].reshape(DEC_BATCH, DEC_SEQ, GM_WIDTH))

        y_xa = _xattn_prompt(i, q, mem_kb, mem_vb)
        y_xa = _xattn_sample(i, q, cache_mem_k, cache_mem_v, y_xa)

        j = i // 2
        merge_args = (y_gm, y_ssd, y_xa, gates, x_p, x_s, x_s_row0, p_gm[i].astype(BF16), p_ssd[i].astype(BF16),
                      p_xa[i].astype(BF16), w_out[i].astype(BF16), _row(ln1_g[i]), _row(ln1_b[i]))
        if i % 2 == 0:
            x, xb = _merge(*merge_args)
            x, xb = _ffn(xb, x, ffn_wg[j].astype(BF16), ffn_wu[j].astype(BF16), ffn_wd[j].astype(BF16),
                         _row(ln2_g[i]), _row(ln2_b[i]))
            x_p, x_s, x_s_row0 = x, x, N_PROMPT
        else:
            x, ri, rg, cnt = _merge(*merge_args, router=(_pad_lanes(router_w[j]), _pad_lanes(_row(router_b[j]))))
            x_p, x_s = _moe(j, x, ri, rg, cnt, moe_wg, moe_wu, moe_wd, _row(ln2_g[i]), _row(ln2_b[i]))
            x_s_row0 = 0
            if i + 1 < DEPTH:
                xb = _cast_rows(x_p, x_s)

    y_prompt = x_p.reshape(BATCH, SEQ, D_MODEL)
    y_sample = x_s.reshape(DEC_BATCH, DEC_SEQ, D_MODEL)
    ssm_s_out = ssm_s.reshape(DEPTH, DEC_BATCH, SSD_HEADS, SSD_HEADDIM, SSD_STATE)
    return (y_prompt, y_sample, mem_k_out, mem_v_out, jnp.stack(conv_p_out),
            jnp.stack(ssm_p_out), jnp.stack(conv_s_out), ssm_s_out, jnp.stack(v_out))
```
